```python
import jax
import jax.numpy as jnp
from jax import lax

D_MODEL = 4096
BATCH = 1
SEQ = 16384
DEPTH = 4

GRID_W = 64
CTX_LEN = 256
ROPE_BASE = 10000.0
NORM_EPS = 1e-6
NEG_INF = -1e30

A_HEADS = 16
A_KV_HEADS = 4
A_HEAD_DIM = 128
A_WINDOW = 128
A_BLOCK = 128
B_CHANNELS = 2048
B_CONV_W = 31
C_HEADS = 32
C_Q_RANK = 1024
C_KV_RANK = 512
C_NOPE = 128
C_ROPE = 64
C_VDIM = 128
C_QBLOCK = 128
N_EXPERTS = 16
EXPERT_FF = 384
EC_CAPACITY = 2

A_Q = A_HEADS * A_HEAD_DIM
A_KV = A_KV_HEADS * A_HEAD_DIM
EVEN_IN = A_Q + 2 * A_KV + 2 * B_CHANNELS
EVEN_OUT = A_Q + B_CHANNELS
ODD_DOWN = C_Q_RANK + C_KV_RANK + C_ROPE
ODD_OUT = C_HEADS * C_VDIM

kernel_name = 'hybrid_swa_conformer_mla_ecmoe_diffusion_trunk'


def rmsnorm(x, g):
    xf = x.astype(jnp.float32)
    y = xf * lax.rsqrt(jnp.mean(xf * xf, axis=-1, keepdims=True) + NORM_EPS)
    return (y * g.astype(jnp.float32)).astype(x.dtype)


def layernorm(x, g, b):
    xf = x.astype(jnp.float32)
    mu = jnp.mean(xf, axis=-1, keepdims=True)
    var = jnp.mean(jnp.square(xf - mu), axis=-1, keepdims=True)
    y = (xf - mu) * lax.rsqrt(var + NORM_EPS)
    return (y * g.astype(jnp.float32) + b.astype(jnp.float32)).astype(x.dtype)


def modulate(h, shift, scale):
    return h * (1 + scale[:, None, :]) + shift[:, None, :]


def rope_pairs(x, cos, sin):
    m = x.shape[-1] // 2
    x1, x2 = x[..., :m], x[..., m:]
    return jnp.concatenate([x1 * cos - x2 * sin, x2 * cos + x1 * sin], axis=-1)


def axial_rope(x, rows, cols):
    half = x.shape[-1] // 2
    inv = ROPE_BASE ** (-jnp.arange(0, half, 2, dtype=jnp.float32) / half)
    ang_r = rows[:, None] * inv[None, :]
    ang_c = cols[:, None] * inv[None, :]
    cr = jnp.cos(ang_r)[None, :, None, :].astype(x.dtype)
    sr = jnp.sin(ang_r)[None, :, None, :].astype(x.dtype)
    cc = jnp.cos(ang_c)[None, :, None, :].astype(x.dtype)
    scn = jnp.sin(ang_c)[None, :, None, :].astype(x.dtype)
    return jnp.concatenate([rope_pairs(x[..., :half], cr, sr), rope_pairs(x[..., half:], cc, scn)], axis=-1)


def band_blocks(t, nb):
    b = t.shape[0]
    tb = t.reshape((b, nb, A_BLOCK) + t.shape[2:])
    tp = jnp.pad(tb, ((0, 0), (1, 1)) + ((0, 0),) * (tb.ndim - 2))
    return jnp.concatenate([tp[:, :-2], tp[:, 1:-1], tp[:, 2:]], axis=2)


def window_context_gqa(q, k, v, k_ctx, v_ctx, sink):
    b, n, h, dh = q.shape
    hkv = k.shape[2]
    g = h // hkv
    nb = n // A_BLOCK
    lc = k_ctx.shape[1]
    w3 = 3 * A_BLOCK
    scale = dh ** -0.5
    qb = q.reshape(b, nb, A_BLOCK, hkv, g, dh)
    kb = band_blocks(k, nb)
    vb = band_blocks(v, nb)
    s_loc = jnp.einsum('bnqhgd,bnkhd->bnhgqk', qb, kb).astype(jnp.float32) * scale
    blk = jnp.arange(nb)[:, None, None]
    qpos = blk * A_BLOCK + jnp.arange(A_BLOCK)[None, :, None]
    kpos = (blk - 1) * A_BLOCK + jnp.arange(w3)[None, None, :]
    valid = (jnp.abs(kpos - qpos) <= A_WINDOW) & (kpos >= 0) & (kpos < n)
    s_loc = jnp.where(valid[None, :, None, None], s_loc, NEG_INF)
    s_ctx = jnp.einsum('bnqhgd,bchd->bnhgqc', qb, k_ctx).astype(jnp.float32) * scale
    s_sink = jnp.broadcast_to(sink.astype(jnp.float32).reshape(1, 1, hkv, g, 1, 1), s_loc.shape[:-1] + (1,))
    p = jax.nn.softmax(jnp.concatenate([s_loc, s_ctx, s_sink], axis=-1), axis=-1)
    p_loc = p[..., :w3].astype(v.dtype)
    p_ctx = p[..., w3:w3 + lc].astype(v.dtype)
    o = jnp.einsum('bnhgqk,bnkhd->bnqhgd', p_loc, vb) + jnp.einsum('bnhgqc,bchd->bnqhgd', p_ctx, v_ctx)
    return o.reshape(b, n, h * dh)


def context_gqa(q, k, v, sink):
    b, l, h, dh = q.shape
    hkv = k.shape[2]
    g = h // hkv
    qg = q.reshape(b, l, hkv, g, dh)
    s = jnp.einsum('bqhgd,bkhd->bhgqk', qg, k).astype(jnp.float32) * dh ** -0.5
    s_sink = jnp.broadcast_to(sink.astype(jnp.float32).reshape(1, hkv, g, 1, 1), s.shape[:-1] + (1,))
    p = jax.nn.softmax(jnp.concatenate([s, s_sink], axis=-1), axis=-1)[..., :-1].astype(v.dtype)
    return jnp.einsum('bhgqk,bkhd->bqhgd', p, v).reshape(b, l, h * dh)


def conformer_conv(a, gate, dw, ln_g, ln_b):
    u = a * jax.nn.sigmoid(gate)
    ch = u.shape[-1]
    pad = (dw.shape[0] - 1) // 2
    z = lax.conv_general_dilated(u, dw[:, None, :].astype(u.dtype), window_strides=(1,), padding=[(pad, pad)],
                                 dimension_numbers=('NWC', 'WIO', 'NWC'), feature_group_count=ch)
    return jax.nn.silu(layernorm(z, ln_g, ln_b))


def even_mixer(hx, hy, w_in, sink, dw, ln_g, ln_b, w_out, rows, cols, need_ctx):
    b, n, _ = hx.shape
    lc = hy.shape[1]
    cuts = [A_Q, A_Q + A_KV, A_Q + 2 * A_KV, A_Q + 2 * A_KV + B_CHANNELS]
    qx, kx, vx, ax, gx = jnp.split(hx @ w_in, cuts, axis=-1)
    if need_ctx:
        qy, ky, vy, ay, gy = jnp.split(hy @ w_in, cuts, axis=-1)
    else:
        ky, vy = jnp.split(hy @ w_in[:, A_Q:A_Q + 2 * A_KV], 2, axis=-1)
    qx = axial_rope(qx.reshape(b, n, A_HEADS, A_HEAD_DIM), rows, cols)
    kx = axial_rope(kx.reshape(b, n, A_KV_HEADS, A_HEAD_DIM), rows, cols)
    vx = vx.reshape(b, n, A_KV_HEADS, A_HEAD_DIM)
    ky = ky.reshape(b, lc, A_KV_HEADS, A_HEAD_DIM)
    vy = vy.reshape(b, lc, A_KV_HEADS, A_HEAD_DIM)
    att_x = window_context_gqa(qx, kx, vx, ky, vy, sink)
    conv_x = conformer_conv(ax, gx, dw, ln_g, ln_b)
    ox = jnp.concatenate([att_x, conv_x], axis=-1) @ w_out
    oy = None
    if need_ctx:
        att_y = context_gqa(qy.reshape(b, lc, A_HEADS, A_HEAD_DIM), ky, vy, sink)
        conv_y = conformer_conv(ay, gy, dw, ln_g, ln_b)
        oy = jnp.concatenate([att_y, conv_y], axis=-1) @ w_out
    return ox, oy


def mla_queries(dq, q_norm_g, w_uq, rows, cols):
    b, l, _ = dq.shape
    q = (rmsnorm(dq, q_norm_g) @ w_uq).reshape(b, l, C_HEADS, C_NOPE + C_ROPE)
    qn, qr = q[..., :C_NOPE], q[..., C_NOPE:]
    if rows is not None:
        qr = axial_rope(qr, rows, cols)
    return qn, qr


def mla_keys(dkv, kv_norm_g, w_ukv, rows, cols):
    b, l, _ = dkv.shape
    ckv = rmsnorm(dkv[..., :C_KV_RANK], kv_norm_g)
    kr = dkv[..., C_KV_RANK:][:, :, None, :]
    if rows is not None:
        kr = axial_rope(kr, rows, cols)
    kv = (ckv @ w_ukv).reshape(b, l, C_HEADS, C_NOPE + C_VDIM)
    return kv[..., :C_NOPE], kr[:, :, 0, :], kv[..., C_NOPE:]


def mla_attend(qn, qr, kn, kr, v):
    b, l, h, _ = qn.shape
    nb = l // C_QBLOCK
    scale = (C_NOPE + C_ROPE) ** -0.5

    def to_blocks(t):
        return jnp.moveaxis(t.reshape((b, nb, C_QBLOCK) + t.shape[2:]), 1, 0)

    def attend(qb):
        qn_b, qr_b = qb
        s = (jnp.einsum('bqhd,bkhd->bhqk', qn_b, kn) + jnp.einsum('bqhr,bkr->bhqk', qr_b, kr)).astype(jnp.float32) * scale
        p = jax.nn.softmax(s, axis=-1).astype(v.dtype)
        return jnp.einsum('bhqk,bkhd->bqhd', p, v)

    o = lax.map(attend, (to_blocks(qn), to_blocks(qr)))
    return jnp.moveaxis(o, 0, 1).reshape(b, l, h * C_VDIM)


def mla_mixer(hx, hy, w_dn, q_norm_g, kv_norm_g, w_uq, w_ukv, w_o, rows, cols, need_ctx):
    dx = hx @ w_dn
    qxn, qxr = mla_queries(dx[..., :C_Q_RANK], q_norm_g, w_uq, rows, cols)
    kxn, kxr, vx = mla_keys(dx[..., C_Q_RANK:], kv_norm_g, w_ukv, rows, cols)
    dy = hy @ (w_dn if need_ctx else w_dn[:, C_Q_RANK:])
    kyn, kyr, vy = mla_keys(dy[..., -(C_KV_RANK + C_ROPE):], kv_norm_g, w_ukv, None, None)
    kn = jnp.concatenate([kxn, kyn], axis=1)
    kr = jnp.concatenate([kxr, kyr], axis=1)
    v = jnp.concatenate([vx, vy], axis=1)
    ox = mla_attend(qxn, qxr, kn, kr, v) @ w_o
    oy = None
    if need_ctx:
        qyn, qyr = mla_queries(dy[..., :C_Q_RANK], q_norm_g, w_uq, None, None)
        oy = mla_attend(qyn, qyr, kyn, kyr, vy) @ w_o
    return ox, oy


def ec_moe(h, w_router, w_gate, w_up, w_down):
    b, l, d = h.shape
    cap = EC_CAPACITY * l // N_EXPERTS
    aff = jax.nn.softmax(jnp.einsum('bld,de->ble', h, w_router).astype(jnp.float32), axis=-1)
    gate, idx = lax.top_k(jnp.swapaxes(aff, 1, 2), cap)
    xs = jax.vmap(lambda hb, ib: hb[ib])(h, idx)
    a = jnp.einsum('becd,edf->becf', xs, w_gate)
    u = jnp.einsum('becd,edf->becf', xs, w_up)
    y = jnp.einsum('becf,efd->becd', jax.nn.silu(a) * u, w_down) * gate[..., None].astype(h.dtype)

    def combine(ib, yb):
        return jnp.zeros((l, d), yb.dtype).at[ib.reshape(-1)].add(yb.reshape(-1, d))

    return jax.vmap(combine)(idx, y)


def setup_inputs(seed: int = 0) -> dict:
    key = jax.random.key(seed)
    ks = jax.random.split(key, 26)
    ne = (DEPTH + 1) // 2
    no = DEPTH // 2
    d = D_MODEL

    def nrm(k, shape, std):
        return jax.random.normal(k, shape, jnp.float32) * std

    def gain(k, shape):
        return 1.0 + 0.02 * jax.random.normal(k, shape, jnp.float32)

    return {
        'x': nrm(ks[0], (BATCH, SEQ, d), 1.0),
        'c': nrm(ks[1], (BATCH, d), 1.0),
        'ctx': nrm(ks[2], (BATCH, CTX_LEN, d), 1.0),
        'c_ctx': nrm(ks[3], (d,), 1.0),
        'ada_w': nrm(ks[4], (DEPTH, d, 6 * d), 0.5 * d ** -0.5),
        'ada_b': nrm(ks[5], (DEPTH, 6 * d), 0.02),
        'norm1_g': gain(ks[6], (DEPTH, d)),
        'norm2_g': gain(ks[7], (DEPTH, d)),
        'ev_w_in': nrm(ks[8], (ne, d, EVEN_IN), d ** -0.5),
        'ev_sink': nrm(ks[9], (ne, A_HEADS), 0.5),
        'ev_dw': nrm(ks[10], (ne, B_CONV_W, B_CHANNELS), B_CONV_W ** -0.5),
        'ev_ln_g': gain(ks[11], (ne, B_CHANNELS)),
        'ev_ln_b': nrm(ks[12], (ne, B_CHANNELS), 0.02),
        'ev_w_out': nrm(ks[13], (ne, EVEN_OUT, d), EVEN_OUT ** -0.5),
        'od_w_dn': nrm(ks[14], (no, d, ODD_DOWN), d ** -0.5),
        'od_q_norm_g': gain(ks[15], (no, C_Q_RANK)),
        'od_kv_norm_g': gain(ks[16], (no, C_KV_RANK)),
        'od_w_uq': nrm(ks[17], (no, C_Q_RANK, C_HEADS * (C_NOPE + C_ROPE)), C_Q_RANK ** -0.5),
        'od_w_ukv': nrm(ks[18], (no, C_KV_RANK, C_HEADS * (C_NOPE + C_VDIM)), C_KV_RANK ** -0.5),
        'od_w_o': nrm(ks[19], (no, ODD_OUT, d), ODD_OUT ** -0.5),
        'moe_router': nrm(ks[20], (DEPTH, d, N_EXPERTS), d ** -0.5),
        'moe_w_gate': nrm(ks[21], (DEPTH, N_EXPERTS, d, EXPERT_FF), d ** -0.5),
        'moe_w_up': nrm(ks[22], (DEPTH, N_EXPERTS, d, EXPERT_FF), d ** -0.5),
        'moe_w_down': nrm(ks[23], (DEPTH, N_EXPERTS, EXPERT_FF, d), EXPERT_FF ** -0.5),
        'final_g': gain(ks[24], (d,)),
    }


def reference(x, c, ctx, c_ctx, ada_w, ada_b, norm1_g, norm2_g, ev_w_in, ev_sink, ev_dw, ev_ln_g, ev_ln_b,
              ev_w_out, od_w_dn, od_q_norm_g, od_kv_norm_g, od_w_uq, od_w_ukv, od_w_o, moe_router, moe_w_gate,
              moe_w_up, moe_w_down, final_g):
    b, n, d = x.shape
    ROWS = n // GRID_W
    rows = jnp.repeat(jnp.arange(ROWS, dtype=jnp.float32), GRID_W)
    cols = (jnp.arange(ROWS * GRID_W) % GRID_W).astype(jnp.float32)
    s_lat = jax.nn.silu(c)
    s_ctx = jax.nn.silu(c_ctx)[None, :]
    y = ctx
    for l in range(DEPTH):
        need_ctx = l < DEPTH - 1
        mx = jnp.split(s_lat @ ada_w[l] + ada_b[l], 6, axis=-1)
        my = jnp.split(s_ctx @ ada_w[l] + ada_b[l], 6, axis=-1)
        hx = modulate(rmsnorm(x, norm1_g[l]), mx[0], mx[1])
        hy = modulate(rmsnorm(y, norm1_g[l]), my[0], my[1])
        i = l // 2
        if l % 2 == 0:
            ox, oy = even_mixer(hx, hy, ev_w_in[i], ev_sink[i], ev_dw[i], ev_ln_g[i], ev_ln_b[i], ev_w_out[i],
                                rows, cols, need_ctx)
        else:
            ox, oy = mla_mixer(hx, hy, od_w_dn[i], od_q_norm_g[i], od_kv_norm_g[i], od_w_uq[i], od_w_ukv[i],
                               od_w_o[i], rows, cols, need_ctx)
        x = x + mx[2][:, None, :] * ox
        hx = modulate(rmsnorm(x, norm2_g[l]), mx[3], mx[4])
        x = x + mx[5][:, None, :] * ec_moe(hx, moe_router[l], moe_w_gate[l], moe_w_up[l], moe_w_down[l])
        if need_ctx:
            y = y + my[2][:, None, :] * oy
            hy = modulate(rmsnorm(y, norm2_g[l]), my[3], my[4])
            y = y + my[5][:, None, :] * ec_moe(hy, moe_router[l], moe_w_gate[l], moe_w_up[l], moe_w_down[l])
    return rmsnorm(x, final_g)
```

```python
import functools

import jax
import jax.numpy as jnp
from jax import lax
from jax.experimental import pallas as pl
from jax.experimental.pallas import tpu as pltpu

F32 = jnp.float32
BF16 = jnp.bfloat16

GRID_W = 64
ROPE_BASE = 10000.0
NORM_EPS = 1e-6
NEG_INF = -1e30
HEAD_DIM = 128
A_WINDOW = 128
EC_CAPACITY = 2

LANES = 128
SUBLANES = 8
BF16_ROWS = 16
MIB = 1024 * 1024


def _params(n_axes, vmem_mib):
    return pltpu.CompilerParams(dimension_semantics=("arbitrary",) * n_axes, vmem_limit_bytes=vmem_mib * MIB)


def _pick(total, target, mult):
    if total <= target:
        return total
    best = None
    for d in range(mult, target + 1, mult):
        if total % d == 0:
            best = d
    assert best is not None, (total, target, mult)
    return best


def _sigmoid(v):
    return 1.0 / (1.0 + jnp.exp(-v))


def _norm_mod(xv, g, shift, scale):
    yv = xv * lax.rsqrt(jnp.mean(xv * xv, axis=-1, keepdims=True) + NORM_EPS)
    return (yv * g) * (1.0 + scale) + shift


def _rope(xv, cos, sin, pair):
    lane = lax.broadcasted_iota(jnp.int32, xv.shape, 1)
    first = (lane % (2 * pair)) < pair
    partner = jnp.where(first, pltpu.roll(xv, LANES - pair, 1), pltpu.roll(xv, pair, 1))
    return xv * cos + partner * sin


def _adaln_kernel(cc_ref, w_ref, b_ref, o_ref):
    cc = cc_ref[...]
    s = cc * _sigmoid(cc)
    w = w_ref[...]
    b = b_ref[...]
    r0 = jnp.sum(s[:, 0:1] * w, axis=0, keepdims=True) + b
    r1 = jnp.sum(s[:, 1:2] * w, axis=0, keepdims=True) + b
    o_ref[...] = jnp.concatenate([r0, r1, jnp.zeros((SUBLANES - 2, w.shape[1]), F32)], axis=0)


def _adaln(c, c_ctx, ada_w, ada_b):
    depth, d, n6 = ada_w.shape
    cc = jnp.zeros((d, SUBLANES), F32).at[:, 0].set(c[0]).at[:, 1].set(c_ctx)
    tn = _pick(n6, 512, LANES)
    return pl.pallas_call(
        _adaln_kernel,
        out_shape=jax.ShapeDtypeStruct((depth, SUBLANES, n6), F32),
        grid=(depth, n6 // tn),
        in_specs=[pl.BlockSpec((d, SUBLANES), lambda l, j: (0, 0)),
                  pl.BlockSpec((None, d, tn), lambda l, j: (l, 0, j)),
                  pl.BlockSpec((None, 1, tn), lambda l, j: (l, 0, j))],
        out_specs=pl.BlockSpec((None, SUBLANES, tn), lambda l, j: (l, 0, j)),
        compiler_params=_params(2, 48),
        name="adaln",
    )(cc, ada_w, ada_b.reshape(depth, 1, n6))


def _normmod_kernel(x_ref, g_ref, sh_ref, sc_ref, o_ref):
    o_ref[...] = _norm_mod(x_ref[...], g_ref[...], sh_ref[...], sc_ref[...]).astype(o_ref.dtype)


def _normmod(xr, g, shift, scale, out_dtype):
    r, d = xr.shape
    tr = _pick(r, 512, BF16_ROWS)
    vec = pl.BlockSpec((1, d), lambda i: (0, 0))
    return pl.pallas_call(
        _normmod_kernel,
        out_shape=jax.ShapeDtypeStruct((r, d), out_dtype),
        grid=(r // tr,),
        in_specs=[pl.BlockSpec((tr, d), lambda i: (i, 0)), vec, vec, vec],
        out_specs=pl.BlockSpec((tr, d), lambda i: (i, 0)),
        compiler_params=_params(1, 48),
        name="normmod",
    )(xr, g.reshape(1, d), shift, scale)


def _mm_kernel(*refs, n_pairs, epilogue, scale, pair):
    a_refs, w_refs, rest = refs[:n_pairs], refs[n_pairs:2 * n_pairs], refs[2 * n_pairs:]
    acc = None
    for a_ref, w_ref in zip(a_refs, w_refs):
        part = jnp.dot(a_ref[...], w_ref[...], preferred_element_type=F32)
        acc = part if acc is None else acc + part
    if epilogue == "plain":
        (o_ref,) = rest
        o_ref[...] = (acc * scale if scale != 1.0 else acc).astype(o_ref.dtype)
    elif epilogue == "rope":
        cos_ref, sin_ref, o_ref = rest
        cos, sin = cos_ref[...], sin_ref[...]
        for h in range(acc.shape[1] // LANES):
            sl = slice(h * LANES, (h + 1) * LANES)
            r = _rope(acc[:, sl], cos, sin, pair)
            o_ref[:, sl] = (r * scale if scale != 1.0 else r).astype(o_ref.dtype)
    else:
        x_ref, gate_ref, o_ref = rest
        o_ref[...] = x_ref[...] + gate_ref[...] * acc


def _matmul(a_list, w_list, *, out_dtype, epilogue="plain", extras=(), scale=1.0, pair=0, bm=1024, bn=1024):
    m = a_list[0].shape[0]
    n = w_list[0].shape[1]
    bm = _pick(m, bm, BF16_ROWS)
    bn = _pick(n, bn, LANES)
    in_specs = [pl.BlockSpec((bm, a.shape[1]), lambda j, i: (i, 0)) for a in a_list]
    in_specs += [pl.BlockSpec((w.shape[0], bn), lambda j, i: (0, j)) for w in w_list]
    if epilogue == "rope":
        in_specs += [pl.BlockSpec((bm, LANES), lambda j, i: (i, 0))] * 2
    elif epilogue == "resid":
        in_specs += [pl.BlockSpec((bm, bn), lambda j, i: (i, j)), pl.BlockSpec((1, bn), lambda j, i: (0, j))]
    return pl.pallas_call(
        functools.partial(_mm_kernel, n_pairs=len(a_list), epilogue=epilogue, scale=scale, pair=pair),
        out_shape=jax.ShapeDtypeStruct((m, n), out_dtype),
        grid=(n // bn, m // bm),
        in_specs=in_specs,
        out_specs=pl.BlockSpec((bm, bn), lambda j, i: (i, j)),
        compiler_params=_params(2, 56),
        name="mm_" + epilogue,
    )(*a_list, *w_list, *extras)


def _gqa_kernel(sink_ref, *refs, groups, tq, local, n_tokens):
    if local:
        q_ref, k_ref, v_ref, ky_ref, vy_ref, o_ref = refs
    else:
        q_ref, ky_ref, vy_ref, o_ref = refs
    hk = pl.program_id(0)
    nt = (((1,), (1,)), ((), ()))
    ky = ky_ref[...]
    vy = vy_ref[...]
    if local:
        n = pl.program_id(1)
        win = tq + 2 * A_WINDOW
        start = pl.multiple_of(jnp.clip(n * tq - A_WINDOW, 0, n_tokens - win), BF16_ROWS)
        kw = k_ref[pl.ds(start, win), :]
        vw = v_ref[pl.ds(start, win), :]
        qpos = n * tq + lax.broadcasted_iota(jnp.int32, (tq, win), 0)
        kpos = start + lax.broadcasted_iota(jnp.int32, (tq, win), 1)
        band = jnp.abs(kpos - qpos) <= A_WINDOW
    for gi in range(groups):
        sl = slice(gi * HEAD_DIM, (gi + 1) * HEAD_DIM)
        qg = q_ref[:, sl]
        sink = sink_ref[hk * groups + gi]
        s_ctx = lax.dot_general(qg, ky, nt, preferred_element_type=F32)
        m = jnp.maximum(jnp.max(s_ctx, axis=1, keepdims=True), sink)
        if local:
            s_loc = jnp.where(band, lax.dot_general(qg, kw, nt, preferred_element_type=F32), NEG_INF)
            m = jnp.maximum(m, jnp.max(s_loc, axis=1, keepdims=True))
        p_ctx = jnp.exp(s_ctx - m)
        den = jnp.sum(p_ctx, axis=1, keepdims=True) + jnp.exp(sink - m)
        o = jnp.dot(p_ctx.astype(BF16), vy, preferred_element_type=F32)
        if local:
            p_loc = jnp.exp(s_loc - m)
            den = den + jnp.sum(p_loc, axis=1, keepdims=True)
            o = o + jnp.dot(p_loc.astype(BF16), vw, preferred_element_type=F32)
        o_ref[:, sl] = (o / den).astype(o_ref.dtype)


def _gqa(q, kx, vsrc_x, v_col0, ky, vsrc_y, sink, *, local):
    r, aq = q.shape
    hkv = ky.shape[1] // HEAD_DIM
    groups = aq // HEAD_DIM // hkv
    lc = ky.shape[0]
    gw = groups * HEAD_DIM
    tq = _pick(r, 256, BF16_ROWS)
    n_tokens = kx.shape[0] if local else 0
    if local:
        assert n_tokens >= tq + 2 * A_WINDOW
    q_spec = pl.BlockSpec((tq, gw), lambda h, i, s: (i, h))
    ctx_specs = [pl.BlockSpec((lc, HEAD_DIM), lambda h, i, s: (0, h)),
                 pl.BlockSpec((lc, HEAD_DIM), lambda h, i, s: (0, v_col0 + h))]
    if local:
        in_specs = [q_spec,
                    pl.BlockSpec((n_tokens, HEAD_DIM), lambda h, i, s: (0, h)),
                    pl.BlockSpec((n_tokens, HEAD_DIM), lambda h, i, s: (0, v_col0 + h))] + ctx_specs
        operands = (q, kx, vsrc_x, ky, vsrc_y)
    else:
        in_specs = [q_spec] + ctx_specs
        operands = (q, ky, vsrc_y)
    return pl.pallas_call(
        functools.partial(_gqa_kernel, groups=groups, tq=tq, local=local, n_tokens=n_tokens),
        out_shape=jax.ShapeDtypeStruct((r, aq), BF16),
        grid_spec=pltpu.PrefetchScalarGridSpec(
            num_scalar_prefetch=1, grid=(hkv, r // tq), in_specs=in_specs,
            out_specs=pl.BlockSpec((tq, gw), lambda h, i, s: (i, h))),
        compiler_params=_params(2, 48),
        name="gqa_local" if local else "gqa_ctx",
    )(sink, *operands)


CONV_HALO = 16
CONV_ROWS = 32
CONV_LANES = 256


def _conv_kernel(a_ref, g_ref, ap_ref, gp_ref, an_ref, gn_ref, dw_ref, lg_ref, lb_ref, o_ref, u_ref, z_ref,
                 *, taps, n_blocks):
    i = pl.program_id(0)
    tr, ch = z_ref.shape
    pad = (taps - 1) // 2

    def glu(a, g):
        return a.astype(F32) * _sigmoid(g.astype(F32))

    u_ref[0:CONV_HALO, :] = glu(ap_ref[...], gp_ref[...]) * (i > 0).astype(F32)
    u_ref[CONV_HALO:CONV_HALO + tr, :] = glu(a_ref[...], g_ref[...])
    u_ref[CONV_HALO + tr:, :] = glu(an_ref[...], gn_ref[...]) * (i < n_blocks - 1).astype(F32)

    def lane_chunk(cc, carry):
        c0 = pl.multiple_of(cc * CONV_LANES, CONV_LANES)
        for r0 in range(0, tr, CONV_ROWS):
            acc = jnp.zeros((CONV_ROWS, CONV_LANES), F32)
            for j in range(taps):
                acc = acc + (u_ref[pl.ds(r0 + j - pad + CONV_HALO, CONV_ROWS), pl.ds(c0, CONV_LANES)]
                             * dw_ref[pl.ds(j, 1), pl.ds(c0, CONV_LANES)])
            z_ref[pl.ds(r0, CONV_ROWS), pl.ds(c0, CONV_LANES)] = acc
        return carry

    lax.fori_loop(0, ch // CONV_LANES, lane_chunk, 0)
    z = z_ref[...]
    mu = jnp.mean(z, axis=-1, keepdims=True)
    zc = z - mu
    var = jnp.mean(zc * zc, axis=-1, keepdims=True)
    yv = zc * lax.rsqrt(var + NORM_EPS) * lg_ref[...] + lb_ref[...]
    o_ref[...] = (yv * _sigmoid(yv)).astype(o_ref.dtype)


def _conformer_conv(agv, dw, ln_g, ln_b):
    r = agv.shape[0]
    taps, ch = dw.shape
    assert (taps - 1) // 2 < CONV_HALO and ch % CONV_LANES == 0
    tr = _pick(r, 256, CONV_ROWS)
    nb = r // tr
    hb = tr // CONV_HALO
    last = r // CONV_HALO - 1
    dwp = jnp.zeros((2 * CONV_HALO, ch), F32).at[:taps].set(dw)
    cur = lambda col: pl.BlockSpec((tr, ch), lambda i: (i, col))
    prv = lambda col: pl.BlockSpec((CONV_HALO, ch), lambda i: (jnp.maximum(i * hb - 1, 0), col))
    nxt = lambda col: pl.BlockSpec((CONV_HALO, ch), lambda i: (jnp.minimum((i + 1) * hb, last), col))
    vec = pl.BlockSpec((1, ch), lambda i: (0, 0))
    return pl.pallas_call(
        functools.partial(_conv_kernel, taps=taps, n_blocks=nb),
        out_shape=jax.ShapeDtypeStruct((r, ch), BF16),
        grid=(nb,),
        in_specs=[cur(0), cur(1), prv(0), prv(1), nxt(0), nxt(1),
                  pl.BlockSpec((2 * CONV_HALO, ch), lambda i: (0, 0)), vec, vec],
        out_specs=pl.BlockSpec((tr, ch), lambda i: (i, 0)),
        scratch_shapes=[pltpu.VMEM((tr + 2 * CONV_HALO, ch), F32), pltpu.VMEM((tr, ch), F32)],
        compiler_params=_params(1, 48),
        name="conformer_conv",
    )(agv, agv, agv, agv, agv, agv, dwp, ln_g.reshape(1, ch), ln_b.reshape(1, ch))


def _mla_down_kernel(h_ref, w_ref, qg_ref, kg_ref, cos_ref, sin_ref, q_ref, kv_ref, kr_ref, *, q_rank, kv_rank,
                     pair, use_rope):
    acc = jnp.dot(h_ref[...], w_ref[...], preferred_element_type=F32)

    def rms(v, g):
        return v * lax.rsqrt(jnp.mean(v * v, axis=-1, keepdims=True) + NORM_EPS) * g

    q_ref[...] = rms(acc[:, :q_rank], qg_ref[...]).astype(q_ref.dtype)
    kv_ref[...] = rms(acc[:, q_rank:q_rank + kv_rank], kg_ref[...]).astype(kv_ref.dtype)
    kr = acc[:, q_rank + kv_rank:]
    if use_rope:
        kr = _rope(kr, cos_ref[...], sin_ref[...], pair)
    kr_ref[...] = kr.astype(kr_ref.dtype)


def _mla_down(h, w_dn_p, q_g, kv_g, cos, sin, *, pair, use_rope):
    m, d = h.shape
    q_rank, kv_rank = q_g.shape[0], kv_g.shape[0]
    wn = w_dn_p.shape[1]
    assert wn == q_rank + kv_rank + LANES
    bm = _pick(m, 512, BF16_ROWS)
    row = lambda w: pl.BlockSpec((bm, w), lambda i: (i, 0))
    return pl.pallas_call(
        functools.partial(_mla_down_kernel, q_rank=q_rank, kv_rank=kv_rank, pair=pair, use_rope=use_rope),
        out_shape=(jax.ShapeDtypeStruct((m, q_rank), BF16), jax.ShapeDtypeStruct((m, kv_rank), BF16),
                   jax.ShapeDtypeStruct((m, LANES), BF16)),
        grid=(m // bm,),
        in_specs=[row(d), pl.BlockSpec((d, wn), lambda i: (0, 0)),
                  pl.BlockSpec((1, q_rank), lambda i: (0, 0)), pl.BlockSpec((1, kv_rank), lambda i: (0, 0)),
                  row(LANES), row(LANES)],
        out_specs=(row(q_rank), row(kv_rank), row(LANES)),
        compiler_params=_params(1, 56),
        name="mla_down",
    )(h, w_dn_p, q_g.reshape(1, q_rank), kv_g.reshape(1, kv_rank), cos, sin)


def _mla_attn_kernel(qn_ref, qr_ref, kn_ref, kr_ref, v_ref, o_ref, kcat_ref, *, tk):
    nk = kcat_ref.shape[0]
    tq = qn_ref.shape[0]

    @pl.when(pl.program_id(1) == 0)
    def _():
        kcat_ref[:, :HEAD_DIM] = kn_ref[...]
        kcat_ref[:, HEAD_DIM:] = kr_ref[...]

    q = jnp.concatenate([qn_ref[...], qr_ref[...]], axis=1)
    nt = (((1,), (1,)), ((), ()))

    def chunk(c, carry):
        m, den, acc = carry
        off = pl.multiple_of(c * tk, tk)
        s = lax.dot_general(q, kcat_ref[pl.ds(off, tk), :], nt, preferred_element_type=F32)
        m_new = jnp.maximum(m, jnp.max(s, axis=1, keepdims=True))
        alpha = jnp.exp(m - m_new)
        p = jnp.exp(s - m_new)
        den = alpha * den + jnp.sum(p, axis=1, keepdims=True)
        acc = alpha * acc + jnp.dot(p.astype(BF16), v_ref[pl.ds(off, tk), :], preferred_element_type=F32)
        return m_new, den, acc

    init = (jnp.full((tq, 1), NEG_INF, F32), jnp.zeros((tq, 1), F32), jnp.zeros((tq, HEAD_DIM), F32))
    _, den, acc = lax.fori_loop(0, nk // tk, chunk, init)
    o_ref[...] = (acc / den).astype(o_ref.dtype)


def _mla_attn(qn, qr, kv, kr, *, n_heads, key_rows, key_block):
    r = qn.shape[0]
    tq = _pick(r, 512, BF16_ROWS)
    tk = _pick(key_rows, 1024, LANES)
    kspec = lambda col0: pl.BlockSpec((key_rows, HEAD_DIM), lambda h, i: (key_block, col0 + h))
    qspec = pl.BlockSpec((tq, HEAD_DIM), lambda h, i: (i, h))
    return pl.pallas_call(
        functools.partial(_mla_attn_kernel, tk=tk),
        out_shape=jax.ShapeDtypeStruct((r, n_heads * HEAD_DIM), BF16),
        grid=(n_heads, r // tq),
        in_specs=[qspec, qspec, kspec(0), pl.BlockSpec((key_rows, LANES), lambda h, i: (key_block, 0)),
                  kspec(n_heads)],
        out_specs=pl.BlockSpec((tq, HEAD_DIM), lambda h, i: (i, h)),
        scratch_shapes=[pltpu.VMEM((key_rows, 2 * HEAD_DIM), BF16)],
        compiler_params=_params(2, 56),
        name="mla_attn",
    )(qn, qr, kv, kr, kv)


def _router_kernel(x_ref, g_ref, sh_ref, sc_ref, wt_ref, aff_ref):
    h = _norm_mod(x_ref[...], g_ref[...], sh_ref[...], sc_ref[...])
    logits = lax.dot_general(wt_ref[...], h, (((1,), (1,)), ((), ())), precision=lax.Precision.HIGHEST,
                             preferred_element_type=F32)
    e = jnp.exp(logits - jnp.max(logits, axis=0, keepdims=True))
    aff_ref[...] = e / jnp.sum(e, axis=0, keepdims=True)


def _router(xr, g, shift, scale, w_router_t):
    r, d = xr.shape
    ne = w_router_t.shape[0]
    tr = _pick(r, 512, LANES)
    vec = pl.BlockSpec((1, d), lambda i: (0, 0))
    return pl.pallas_call(
        _router_kernel,
        out_shape=jax.ShapeDtypeStruct((ne, r), F32),
        grid=(r // tr,),
        in_specs=[pl.BlockSpec((tr, d), lambda i: (i, 0)), vec, vec, vec, pl.BlockSpec((ne, d), lambda i: (0, 0))],
        out_specs=pl.BlockSpec((ne, tr), lambda i: (0, i)),
        compiler_params=_params(1, 48),
        name="router",
    )(xr, g.reshape(1, d), shift, scale, w_router_t)


def _row_copy(src_hbm, buf, sem, slot, tok, r):
    return pltpu.make_async_copy(src_hbm.at[pl.ds(tok, 1), :], buf.at[slot, pl.ds(r, 1), :], sem.at[slot])


def _start_gather(idx_ref, src_hbm, buf, sem, step, slot, rows):
    def issue(r, carry):
        _row_copy(src_hbm, buf, sem, slot, idx_ref[step * rows + r], r).start()
        return carry
    lax.fori_loop(0, rows, issue, 0, unroll=8)


def _wait_rows(src_hbm, buf, sem, slot, rows):
    for _ in range(rows):
        _row_copy(src_hbm, buf, sem, slot, 0, 0).wait()


def _moe_ffn_kernel(idx_ref, gate_ref, g_ref, sh_ref, sc_ref, wg_ref, wu_ref, wd_ref, x_hbm, y_ref, xbuf, sem,
                    *, rows, n_steps):
    step = pl.program_id(0) * pl.num_programs(1) + pl.program_id(1)
    slot = step % 2

    @pl.when(step == 0)
    def _():
        _start_gather(idx_ref, x_hbm, xbuf, sem, 0, 0, rows)

    @pl.when(step + 1 < n_steps)
    def _():
        _start_gather(idx_ref, x_hbm, xbuf, sem, step + 1, 1 - slot, rows)

    _wait_rows(x_hbm, xbuf, sem, slot, rows)
    h = _norm_mod(xbuf[slot], g_ref[...], sh_ref[...], sc_ref[...]).astype(BF16)
    a = jnp.dot(h, wg_ref[...], preferred_element_type=F32)
    u = jnp.dot(h, wu_ref[...], preferred_element_type=F32)
    z = (a * _sigmoid(a) * u).astype(BF16)
    y_ref[...] = jnp.dot(z, wd_ref[...], preferred_element_type=F32) * gate_ref[...]


def _moe_ffn(idx_flat, gate_col, xr, g, shift, scale, wg, wu, wd, *, rows):
    r, d = xr.shape
    ne, _, ff = wg.shape
    slots = idx_flat.shape[0]
    nc = slots // ne // rows
    vec = pl.BlockSpec((1, d), lambda e, c, idx: (0, 0))
    return pl.pallas_call(
        functools.partial(_moe_ffn_kernel, rows=rows, n_steps=ne * nc),
        out_shape=jax.ShapeDtypeStruct((slots, d), F32),
        grid_spec=pltpu.PrefetchScalarGridSpec(
            num_scalar_prefetch=1, grid=(ne, nc),
            in_specs=[pl.BlockSpec((rows, 1), lambda e, c, idx: (e * nc + c, 0)), vec, vec, vec,
                      pl.BlockSpec((None, d, ff), lambda e, c, idx: (e, 0, 0)),
                      pl.BlockSpec((None, d, ff), lambda e, c, idx: (e, 0, 0)),
                      pl.BlockSpec((None, ff, d), lambda e, c, idx: (e, 0, 0)),
                      pl.BlockSpec(memory_space=pl.ANY)],
            out_specs=pl.BlockSpec((rows, d), lambda e, c, idx: (e * nc + c, 0)),
            scratch_shapes=[pltpu.VMEM((2, rows, d), F32), pltpu.SemaphoreType.DMA((2,))]),
        compiler_params=_params(2, 56),
        name="moe_ffn",
    )(idx_flat, gate_col, g.reshape(1, d), shift, scale, wg, wu, wd, xr)


def _moe_combine_kernel(idx_ref, y_ref, gate_ref, x_hbm, o_hbm, abuf, gsem, ssem, *, rows, n_steps):
    del x_hbm
    c = pl.program_id(1)
    nc = pl.num_programs(1)
    step = pl.program_id(0) * nc + c
    slot = step % 2

    def scatter_copy(s, sl, r):
        return pltpu.make_async_copy(abuf.at[sl, pl.ds(r, 1), :], o_hbm.at[pl.ds(idx_ref[s * rows + r], 1), :],
                                     ssem.at[sl])

    def wait_scatter(sl):
        for _ in range(rows):
            pltpu.make_async_copy(abuf.at[sl, pl.ds(0, 1), :], o_hbm.at[pl.ds(0, 1), :], ssem.at[sl]).wait()

    @pl.when(step == 0)
    def _():
        _start_gather(idx_ref, o_hbm, abuf, gsem, 0, 0, rows)

    _wait_rows(o_hbm, abuf, gsem, slot, rows)
    abuf[slot] = abuf[slot] + gate_ref[...] * y_ref[...]

    def issue(r, carry):
        scatter_copy(step, slot, r).start()
        return carry
    lax.fori_loop(0, rows, issue, 0, unroll=8)

    @pl.when(c > 0)
    def _():
        wait_scatter(1 - slot)

    @pl.when(c == nc - 1)
    def _():
        wait_scatter(slot)

    @pl.when(step + 1 < n_steps)
    def _():
        _start_gather(idx_ref, o_hbm, abuf, gsem, step + 1, 1 - slot, rows)


def _moe_combine(idx_flat, y, gate_vec, xr, *, n_experts, rows):
    r, d = xr.shape
    slots = idx_flat.shape[0]
    nc = slots // n_experts // rows
    return pl.pallas_call(
        functools.partial(_moe_combine_kernel, rows=rows, n_steps=n_experts * nc),
        out_shape=jax.ShapeDtypeStruct((r, d), F32),
        grid_spec=pltpu.PrefetchScalarGridSpec(
            num_scalar_prefetch=1, grid=(n_experts, nc),
            in_specs=[pl.BlockSpec((rows, d), lambda e, c, idx: (e * nc + c, 0)),
                      pl.BlockSpec((1, d), lambda e, c, idx: (0, 0)),
                      pl.BlockSpec(memory_space=pl.ANY)],
            out_specs=pl.BlockSpec(memory_space=pl.ANY),
            scratch_shapes=[pltpu.VMEM((2, rows, d), F32), pltpu.SemaphoreType.DMA((2,)),
                            pltpu.SemaphoreType.DMA((2,))]),
        input_output_aliases={3: 0},
        compiler_params=_params(2, 48),
        name="moe_combine",
    )(idx_flat, y, gate_vec, xr)


def _ec_moe(xr, g, shift, scale, out_gate, w_router_t, wg, wu, wd):
    r = xr.shape[0]
    ne = w_router_t.shape[0]
    cap = EC_CAPACITY * r // ne
    rows = _pick(cap, 256, SUBLANES)
    aff_t = _router(xr, g, shift, scale, w_router_t)
    gate, idx = lax.top_k(aff_t, cap)
    idx_flat = idx.reshape(-1).astype(jnp.int32)
    y = _moe_ffn(idx_flat, gate.reshape(-1, 1), xr, g, shift, scale, wg, wu, wd, rows=rows)
    return _moe_combine(idx_flat, y, out_gate, xr, n_experts=ne, rows=rows)


def _rope_tables(n, dim):
    half = dim // 2
    inv = ROPE_BASE ** (-jnp.arange(0, half, 2, dtype=F32) / half)
    rows = jnp.repeat(jnp.arange(n // GRID_W, dtype=F32), GRID_W)
    cols = (jnp.arange(n) % GRID_W).astype(F32)
    ar = rows[:, None] * inv[None, :]
    ac = cols[:, None] * inv[None, :]
    cos = jnp.concatenate([jnp.cos(ar), jnp.cos(ar), jnp.cos(ac), jnp.cos(ac)], axis=1)
    sin = jnp.concatenate([-jnp.sin(ar), jnp.sin(ar), -jnp.sin(ac), jnp.sin(ac)], axis=1)
    if dim < LANES:
        cos = jnp.concatenate([cos, jnp.ones((n, LANES - dim), F32)], axis=1)
        sin = jnp.concatenate([sin, jnp.zeros((n, LANES - dim), F32)], axis=1)
    return cos, sin


def _even_mixer(hx, hy, x, y, gate_x, gate_y, w_in, sink, dw, ln_g, ln_b, w_out, tables, need_ctx):
    ch = dw.shape[1]
    aq = w_out.shape[0] - ch
    akv = (w_in.shape[1] - aq - 2 * ch) // 2
    cos, sin = tables
    scale = HEAD_DIM ** -0.5
    pair = HEAD_DIM // 4
    wq = w_in[:, :aq].astype(BF16)
    wk = w_in[:, aq:aq + akv].astype(BF16)
    w_agv = jnp.concatenate([w_in[:, aq + 2 * akv:], w_in[:, aq + akv:aq + 2 * akv]], axis=1).astype(BF16)
    w_o1 = w_out[:aq].astype(BF16)
    w_o2 = w_out[aq:].astype(BF16)
    v_col0 = 2 * ch // HEAD_DIM

    qx = _matmul([hx], [wq], out_dtype=BF16, epilogue="rope", extras=(cos, sin), scale=scale, pair=pair)
    kx = _matmul([hx], [wk], out_dtype=BF16, epilogue="rope", extras=(cos, sin), pair=pair)
    agv_x = _matmul([hx], [w_agv], out_dtype=BF16)
    ky = _matmul([hy], [wk], out_dtype=BF16)
    agv_y = _matmul([hy], [w_agv], out_dtype=BF16)

    att_x = _gqa(qx, kx, agv_x, v_col0, ky, agv_y, sink, local=True)
    conv_x = _conformer_conv(agv_x, dw, ln_g, ln_b)
    x = _matmul([att_x, conv_x], [w_o1, w_o2], out_dtype=F32, epilogue="resid", extras=(x, gate_x), bn=512)
    if need_ctx:
        qy = _matmul([hy], [wq], out_dtype=BF16, scale=scale)
        att_y = _gqa(qy, None, None, v_col0, ky, agv_y, sink, local=False)
        conv_y = _conformer_conv(agv_y, dw, ln_g, ln_b)
        y = _matmul([att_y, conv_y], [w_o1, w_o2], out_dtype=F32, epilogue="resid", extras=(y, gate_y), bn=512)
    return x, y


def _mla_mixer(hx, hy, x, y, gate_x, gate_y, w_dn, q_g, kv_g, w_uq, w_ukv, w_o, tables, need_ctx):
    n, d = hx.shape
    lc = hy.shape[0]
    q_rank, kv_rank = q_g.shape[0], kv_g.shape[0]
    n_heads = w_o.shape[0] // HEAD_DIM
    rope_dim = w_dn.shape[1] - q_rank - kv_rank
    assert n % lc == 0 and rope_dim <= LANES
    cos, sin = tables
    pair = rope_dim // 4
    scale = (HEAD_DIM + rope_dim) ** -0.5
    w_dn_p = jnp.concatenate([w_dn, jnp.zeros((d, LANES - rope_dim), F32)], axis=1).astype(BF16)
    w_uq3 = w_uq.reshape(q_rank, n_heads, HEAD_DIM + rope_dim)
    w_qn = w_uq3[:, :, :HEAD_DIM].reshape(q_rank, n_heads * HEAD_DIM).astype(BF16)
    w_qr = jnp.concatenate([w_uq3[:, :, HEAD_DIM:], jnp.zeros((q_rank, n_heads, LANES - rope_dim), F32)],
                           axis=2).reshape(q_rank, n_heads * LANES).astype(BF16)
    w_kv3 = w_ukv.reshape(kv_rank, n_heads, 2 * HEAD_DIM)
    w_kv = jnp.concatenate([w_kv3[:, :, :HEAD_DIM].reshape(kv_rank, -1), w_kv3[:, :, HEAD_DIM:].reshape(kv_rank, -1)],
                           axis=1).astype(BF16)
    w_ob = w_o.astype(BF16)

    qlat_x, ckv_x, kr_x = _mla_down(hx, w_dn_p, q_g, kv_g, cos, sin, pair=pair, use_rope=True)
    qlat_y, ckv_y, kr_y = _mla_down(hy, w_dn_p, q_g, kv_g, cos[:lc], sin[:lc], pair=pair, use_rope=False)
    kv_all = _matmul([jnp.concatenate([ckv_x, ckv_y], axis=0)], [w_kv], out_dtype=BF16, bm=1280)
    kr_all = jnp.concatenate([kr_x, kr_y], axis=0)

    qn_x = _matmul([qlat_x], [w_qn], out_dtype=BF16, scale=scale)
    qr_x = _matmul([qlat_x], [w_qr], out_dtype=BF16, epilogue="rope", extras=(cos, sin), scale=scale, pair=pair)
    o_x = _mla_attn(qn_x, qr_x, kv_all, kr_all, n_heads=n_heads, key_rows=n + lc, key_block=0)
    x = _matmul([o_x], [w_ob], out_dtype=F32, epilogue="resid", extras=(x, gate_x), bn=512)
    if need_ctx:
        qn_y = _matmul([qlat_y], [w_qn], out_dtype=BF16, scale=scale)
        qr_y = _matmul([qlat_y], [w_qr], out_dtype=BF16, scale=scale)
        o_y = _mla_attn(qn_y, qr_y, kv_all, kr_all, n_heads=n_heads, key_rows=lc, key_block=n // lc)
        y = _matmul([o_y], [w_ob], out_dtype=F32, epilogue="resid", extras=(y, gate_y), bn=512)
    return x, y


def kernel(x, c, ctx, c_ctx, ada_w, ada_b, norm1_g, norm2_g, ev_w_in, ev_sink, ev_dw, ev_ln_g, ev_ln_b, ev_w_out, od_w_dn, od_q_norm_g, od_kv_norm_g, od_w_uq, od_w_ukv, od_w_o, moe_router, moe_w_gate, moe_w_up, moe_w_down, final_g):
    b, n, d = x.shape
    assert b == 1 and n % GRID_W == 0
    depth = ada_w.shape[0]
    xr, yr = x[0], ctx[0]
    mods = _adaln(c, c_ctx, ada_w, ada_b)
    tables_a = _rope_tables(n, HEAD_DIM)
    rope_dim = od_w_dn.shape[2] - od_q_norm_g.shape[1] - od_kv_norm_g.shape[1]
    tables_c = _rope_tables(n, rope_dim)
    for l in range(depth):
        need_ctx = l < depth - 1
        mx = [mods[l, 0:1, i * d:(i + 1) * d] for i in range(6)]
        my = [mods[l, 1:2, i * d:(i + 1) * d] for i in range(6)]
        hx = _normmod(xr, norm1_g[l], mx[0], mx[1], BF16)
        hy = _normmod(yr, norm1_g[l], my[0], my[1], BF16)
        i = l // 2
        if l % 2 == 0:
            xr, yr = _even_mixer(hx, hy, xr, yr, mx[2], my[2], ev_w_in[i], ev_sink[i], ev_dw[i], ev_ln_g[i],
                                 ev_ln_b[i], ev_w_out[i], tables_a, need_ctx)
        else:
            xr, yr = _mla_mixer(hx, hy, xr, yr, mx[2], my[2], od_w_dn[i], od_q_norm_g[i], od_kv_norm_g[i],
                                od_w_uq[i], od_w_ukv[i], od_w_o[i], tables_c, need_ctx)
        w_rt = moe_router[l].T
        wg, wu, wd = moe_w_gate[l].astype(BF16), moe_w_up[l].astype(BF16), moe_w_down[l].astype(BF16)
        xr = _ec_moe(xr, norm2_g[l], mx[3], mx[4], mx[5], w_rt, wg, wu, wd)
        if need_ctx:
            yr = _ec_moe(yr, norm2_g[l], my[3], my[4], my[5], w_rt, wg, wu, wd)
    zero = jnp.zeros((1, d), F32)
    return _normmod(xr, final_g, zero, zero, F32)[None]
```

```python
import functools

import jax
import jax.numpy as jnp
from jax import lax
from jax.experimental import pallas as pl
from jax.experimental.pallas import tpu as pltpu

F32 = jnp.float32
BF16 = jnp.bfloat16

GRID_W = 64
ROPE_BASE = 10000.0
NORM_EPS = 1e-6
NEG_INF = -1e30
HEAD_DIM = 128
A_WINDOW = 128
EC_CAPACITY = 2
LOG2_E = 1.4426950408889634

LANES = 128
SUBLANES = 8
BF16_ROWS = 16
MIB = 1024 * 1024


def _params(n_axes, vmem_mib):
    return pltpu.CompilerParams(dimension_semantics=("arbitrary",) * n_axes, vmem_limit_bytes=vmem_mib * MIB)


def _pick(total, target, mult):
    if total <= target:
        return total
    best = None
    for d in range(mult, target + 1, mult):
        if total % d == 0:
            best = d
    assert best is not None, (total, target, mult)
    return best


def _sigmoid(v):
    return 1.0 / (1.0 + jnp.exp(-v))


def _norm_mod(xv, g, shift, scale):
    yv = xv * lax.rsqrt(jnp.mean(xv * xv, axis=-1, keepdims=True) + NORM_EPS)
    return (yv * g) * (1.0 + scale) + shift


def _rope(xv, cos, sin, pair):
    lane = lax.broadcasted_iota(jnp.int32, xv.shape, 1)
    first = (lane % (2 * pair)) < pair
    partner = jnp.where(first, pltpu.roll(xv, LANES - pair, 1), pltpu.roll(xv, pair, 1))
    return xv * cos + partner * sin


def _adaln_kernel(cc_ref, w_ref, b_ref, o_ref):
    cc = cc_ref[...]
    s = cc * _sigmoid(cc)
    w = w_ref[...]
    b = b_ref[...]
    r0 = jnp.sum(s[:, 0:1] * w, axis=0, keepdims=True) + b
    r1 = jnp.sum(s[:, 1:2] * w, axis=0, keepdims=True) + b
    o_ref[...] = jnp.concatenate([r0, r1, jnp.zeros((SUBLANES - 2, w.shape[1]), F32)], axis=0)


def _adaln(c, c_ctx, ada_w, ada_b):
    depth, d, n6 = ada_w.shape
    cc = jnp.zeros((d, SUBLANES), F32).at[:, 0].set(c[0]).at[:, 1].set(c_ctx)
    tn = _pick(n6, 512, LANES)
    return pl.pallas_call(
        _adaln_kernel,
        out_shape=jax.ShapeDtypeStruct((depth, SUBLANES, n6), F32),
        grid=(depth, n6 // tn),
        in_specs=[pl.BlockSpec((d, SUBLANES), lambda l, j: (0, 0)),
                  pl.BlockSpec((None, d, tn), lambda l, j: (l, 0, j)),
                  pl.BlockSpec((None, 1, tn), lambda l, j: (l, 0, j))],
        out_specs=pl.BlockSpec((None, SUBLANES, tn), lambda l, j: (l, 0, j)),
        compiler_params=_params(2, 48),
        name="adaln",
    )(cc, ada_w, ada_b.reshape(depth, 1, n6))


def _normmod_kernel(x_ref, g_ref, sh_ref, sc_ref, o_ref):
    o_ref[...] = _norm_mod(x_ref[...], g_ref[...], sh_ref[...], sc_ref[...]).astype(o_ref.dtype)


def _normmod(xr, g, shift, scale, out_dtype):
    r, d = xr.shape
    tr = _pick(r, 512, BF16_ROWS)
    vec = pl.BlockSpec((1, d), lambda i: (0, 0))
    return pl.pallas_call(
        _normmod_kernel,
        out_shape=jax.ShapeDtypeStruct((r, d), out_dtype),
        grid=(r // tr,),
        in_specs=[pl.BlockSpec((tr, d), lambda i: (i, 0)), vec, vec, vec],
        out_specs=pl.BlockSpec((tr, d), lambda i: (i, 0)),
        compiler_params=_params(1, 48),
        name="normmod",
    )(xr, g.reshape(1, d), shift, scale)


def _mm_kernel(*refs, n_pairs, epilogue, scale, pair):
    a_refs, w_refs, rest = refs[:n_pairs], refs[n_pairs:2 * n_pairs], refs[2 * n_pairs:]
    acc = None
    for a_ref, w_ref in zip(a_refs, w_refs):
        part = jnp.dot(a_ref[...], w_ref[...], preferred_element_type=F32)
        acc = part if acc is None else acc + part
    if epilogue == "plain":
        (o_ref,) = rest
        o_ref[...] = (acc * scale if scale != 1.0 else acc).astype(o_ref.dtype)
    elif epilogue == "rope":
        cos_ref, sin_ref, o_ref = rest
        cos, sin = cos_ref[...], sin_ref[...]
        for h in range(acc.shape[1] // LANES):
            sl = slice(h * LANES, (h + 1) * LANES)
            r = _rope(acc[:, sl], cos, sin, pair)
            o_ref[:, sl] = (r * scale if scale != 1.0 else r).astype(o_ref.dtype)
    else:
        x_ref, gate_ref, o_ref = rest
        o_ref[...] = x_ref[...] + gate_ref[...] * acc


def _matmul(a_list, w_list, *, out_dtype, epilogue="plain", extras=(), scale=1.0, pair=0, bm=1024, bn=1024):
    m = a_list[0].shape[0]
    n = w_list[0].shape[1]
    bm = _pick(m, bm, BF16_ROWS)
    bn = _pick(n, bn, LANES)
    in_specs = [pl.BlockSpec((bm, a.shape[1]), lambda j, i: (i, 0)) for a in a_list]
    in_specs += [pl.BlockSpec((w.shape[0], bn), lambda j, i: (0, j)) for w in w_list]
    if epilogue == "rope":
        in_specs += [pl.BlockSpec((bm, LANES), lambda j, i: (i, 0))] * 2
    elif epilogue == "resid":
        in_specs += [pl.BlockSpec((bm, bn), lambda j, i: (i, j)), pl.BlockSpec((1, bn), lambda j, i: (0, j))]
    return pl.pallas_call(
        functools.partial(_mm_kernel, n_pairs=len(a_list), epilogue=epilogue, scale=scale, pair=pair),
        out_shape=jax.ShapeDtypeStruct((m, n), out_dtype),
        grid=(n // bn, m // bm),
        in_specs=in_specs,
        out_specs=pl.BlockSpec((bm, bn), lambda j, i: (i, j)),
        compiler_params=_params(2, 56),
        name="mm_" + epilogue,
    )(*a_list, *w_list, *extras)


def _gqa_kernel(sink_ref, *refs, groups, tq, local, n_tokens):
    if local:
        q_ref, k_ref, v_ref, ky_ref, vy_ref, o_ref = refs
    else:
        q_ref, ky_ref, vy_ref, o_ref = refs
    hk = pl.program_id(0)
    nt = (((1,), (1,)), ((), ()))
    ky = ky_ref[...]
    vy = vy_ref[...]
    if local:
        n = pl.program_id(1)
        win = tq + 2 * A_WINDOW
        start = pl.multiple_of(jnp.clip(n * tq - A_WINDOW, 0, n_tokens - win), BF16_ROWS)
        kw = k_ref[pl.ds(start, win), :]
        vw = v_ref[pl.ds(start, win), :]
        qpos = n * tq + lax.broadcasted_iota(jnp.int32, (tq, win), 0)
        kpos = start + lax.broadcasted_iota(jnp.int32, (tq, win), 1)
        band = jnp.abs(kpos - qpos) <= A_WINDOW
    for gi in range(groups):
        sl = slice(gi * HEAD_DIM, (gi + 1) * HEAD_DIM)
        qg = q_ref[:, sl]
        sink = sink_ref[hk * groups + gi]
        s_ctx = lax.dot_general(qg, ky, nt, preferred_element_type=F32)
        m = jnp.maximum(jnp.max(s_ctx, axis=1, keepdims=True), sink)
        if local:
            s_loc = jnp.where(band, lax.dot_general(qg, kw, nt, preferred_element_type=F32), NEG_INF)
            m = jnp.maximum(m, jnp.max(s_loc, axis=1, keepdims=True))
        p_ctx = jnp.exp(s_ctx - m)
        den = jnp.sum(p_ctx, axis=1, keepdims=True) + jnp.exp(sink - m)
        o = jnp.dot(p_ctx.astype(BF16), vy, preferred_element_type=F32)
        if local:
            p_loc = jnp.exp(s_loc - m)
            den = den + jnp.sum(p_loc, axis=1, keepdims=True)
            o = o + jnp.dot(p_loc.astype(BF16), vw, preferred_element_type=F32)
        o_ref[:, sl] = (o / den).astype(o_ref.dtype)


def _gqa(q, kx, vsrc_x, v_col0, ky, vsrc_y, sink, *, local):
    r, aq = q.shape
    hkv = ky.shape[1] // HEAD_DIM
    groups = aq // HEAD_DIM // hkv
    lc = ky.shape[0]
    gw = groups * HEAD_DIM
    tq = _pick(r, 256, BF16_ROWS)
    n_tokens = kx.shape[0] if local else 0
    if local:
        assert n_tokens >= tq + 2 * A_WINDOW
    q_spec = pl.BlockSpec((tq, gw), lambda h, i, s: (i, h))
    ctx_specs = [pl.BlockSpec((lc, HEAD_DIM), lambda h, i, s: (0, h)),
                 pl.BlockSpec((lc, HEAD_DIM), lambda h, i, s: (0, v_col0 + h))]
    if local:
        in_specs = [q_spec,
                    pl.BlockSpec((n_tokens, HEAD_DIM), lambda h, i, s: (0, h)),
                    pl.BlockSpec((n_tokens, HEAD_DIM), lambda h, i, s: (0, v_col0 + h))] + ctx_specs
        operands = (q, kx, vsrc_x, ky, vsrc_y)
    else:
        in_specs = [q_spec] + ctx_specs
        operands = (q, ky, vsrc_y)
    return pl.pallas_call(
        functools.partial(_gqa_kernel, groups=groups, tq=tq, local=local, n_tokens=n_tokens),
        out_shape=jax.ShapeDtypeStruct((r, aq), BF16),
        grid_spec=pltpu.PrefetchScalarGridSpec(
            num_scalar_prefetch=1, grid=(hkv, r // tq), in_specs=in_specs,
            out_specs=pl.BlockSpec((tq, gw), lambda h, i, s: (i, h))),
        compiler_params=_params(2, 48),
        name="gqa_local" if local else "gqa_ctx",
    )(sink, *operands)


CONV_HALO = 16
CONV_ROWS = 32
CONV_LANES = 256


def _conv_kernel(a_ref, g_ref, ap_ref, gp_ref, an_ref, gn_ref, dw_ref, lg_ref, lb_ref, o_ref, u_ref, z_ref,
                 *, taps, n_blocks):
    i = pl.program_id(0)
    tr, ch = z_ref.shape
    pad = (taps - 1) // 2

    def glu(a, g):
        return a.astype(F32) * _sigmoid(g.astype(F32))

    u_ref[0:CONV_HALO, :] = glu(ap_ref[...], gp_ref[...]) * (i > 0).astype(F32)
    u_ref[CONV_HALO:CONV_HALO + tr, :] = glu(a_ref[...], g_ref[...])
    u_ref[CONV_HALO + tr:, :] = glu(an_ref[...], gn_ref[...]) * (i < n_blocks - 1).astype(F32)

    def lane_chunk(cc, carry):
        c0 = pl.multiple_of(cc * CONV_LANES, CONV_LANES)
        for r0 in range(0, tr, CONV_ROWS):
            acc = jnp.zeros((CONV_ROWS, CONV_LANES), F32)
            for j in range(taps):
                acc = acc + (u_ref[pl.ds(r0 + j - pad + CONV_HALO, CONV_ROWS), pl.ds(c0, CONV_LANES)]
                             * dw_ref[pl.ds(j, 1), pl.ds(c0, CONV_LANES)])
            z_ref[pl.ds(r0, CONV_ROWS), pl.ds(c0, CONV_LANES)] = acc
        return carry

    lax.fori_loop(0, ch // CONV_LANES, lane_chunk, 0)
    z = z_ref[...]
    mu = jnp.mean(z, axis=-1, keepdims=True)
    zc = z - mu
    var = jnp.mean(zc * zc, axis=-1, keepdims=True)
    yv = zc * lax.rsqrt(var + NORM_EPS) * lg_ref[...] + lb_ref[...]
    o_ref[...] = (yv * _sigmoid(yv)).astype(o_ref.dtype)


def _conformer_conv(agv, dw, ln_g, ln_b):
    r = agv.shape[0]
    taps, ch = dw.shape
    assert (taps - 1) // 2 < CONV_HALO and ch % CONV_LANES == 0
    tr = _pick(r, 256, CONV_ROWS)
    nb = r // tr
    hb = tr // CONV_HALO
    last = r // CONV_HALO - 1
    dwp = jnp.zeros((2 * CONV_HALO, ch), F32).at[:taps].set(dw)
    cur = lambda col: pl.BlockSpec((tr, ch), lambda i: (i, col))
    prv = lambda col: pl.BlockSpec((CONV_HALO, ch), lambda i: (jnp.maximum(i * hb - 1, 0), col))
    nxt = lambda col: pl.BlockSpec((CONV_HALO, ch), lambda i: (jnp.minimum((i + 1) * hb, last), col))
    vec = pl.BlockSpec((1, ch), lambda i: (0, 0))
    return pl.pallas_call(
        functools.partial(_conv_kernel, taps=taps, n_blocks=nb),
        out_shape=jax.ShapeDtypeStruct((r, ch), BF16),
        grid=(nb,),
        in_specs=[cur(0), cur(1), prv(0), prv(1), nxt(0), nxt(1),
                  pl.BlockSpec((2 * CONV_HALO, ch), lambda i: (0, 0)), vec, vec],
        out_specs=pl.BlockSpec((tr, ch), lambda i: (i, 0)),
        scratch_shapes=[pltpu.VMEM((tr + 2 * CONV_HALO, ch), F32), pltpu.VMEM((tr, ch), F32)],
        compiler_params=_params(1, 48),
        name="conformer_conv",
    )(agv, agv, agv, agv, agv, agv, dwp, ln_g.reshape(1, ch), ln_b.reshape(1, ch))


def _mla_down_kernel(h_ref, w_ref, qg_ref, kg_ref, cos_ref, sin_ref, q_ref, kv_ref, kr_ref, *, q_rank, kv_rank,
                     pair, use_rope):
    acc = jnp.dot(h_ref[...], w_ref[...], preferred_element_type=F32)

    def rms(v, g):
        return v * lax.rsqrt(jnp.mean(v * v, axis=-1, keepdims=True) + NORM_EPS) * g

    q_ref[...] = rms(acc[:, :q_rank], qg_ref[...]).astype(q_ref.dtype)
    kv_ref[...] = rms(acc[:, q_rank:q_rank + kv_rank], kg_ref[...]).astype(kv_ref.dtype)
    kr = acc[:, q_rank + kv_rank:]
    if use_rope:
        kr = _rope(kr, cos_ref[...], sin_ref[...], pair)
    kr_ref[...] = kr.astype(kr_ref.dtype)


def _mla_down(h, w_dn_p, q_g, kv_g, cos, sin, *, pair, use_rope):
    m, d = h.shape
    q_rank, kv_rank = q_g.shape[0], kv_g.shape[0]
    wn = w_dn_p.shape[1]
    assert wn == q_rank + kv_rank + LANES
    bm = _pick(m, 512, BF16_ROWS)
    row = lambda w: pl.BlockSpec((bm, w), lambda i: (i, 0))
    return pl.pallas_call(
        functools.partial(_mla_down_kernel, q_rank=q_rank, kv_rank=kv_rank, pair=pair, use_rope=use_rope),
        out_shape=(jax.ShapeDtypeStruct((m, q_rank), BF16), jax.ShapeDtypeStruct((m, kv_rank), BF16),
                   jax.ShapeDtypeStruct((m, LANES), BF16)),
        grid=(m // bm,),
        in_specs=[row(d), pl.BlockSpec((d, wn), lambda i: (0, 0)),
                  pl.BlockSpec((1, q_rank), lambda i: (0, 0)), pl.BlockSpec((1, kv_rank), lambda i: (0, 0)),
                  row(LANES), row(LANES)],
        out_specs=(row(q_rank), row(kv_rank), row(LANES)),
        compiler_params=_params(1, 56),
        name="mla_down",
    )(h, w_dn_p, q_g.reshape(1, q_rank), kv_g.reshape(1, kv_rank), cos, sin)


MLA_SUB = 256


def _mla_attn_kernel(qn_ref, qr_ref, kn_ref, kr_ref, v_ref, o_ref, kcat_ref, vcat_ref, qcat_ref, *, tk, sub):
    nk = kcat_ref.shape[0]
    tq = qn_ref.shape[0]
    n_sub = tq // sub

    @pl.when(pl.program_id(1) == 0)
    def _():
        kcat_ref[:, :HEAD_DIM] = kn_ref[...]
        kcat_ref[:, HEAD_DIM:] = kr_ref[...]
        vcat_ref[:, :HEAD_DIM] = v_ref[...]
        vcat_ref[:, HEAD_DIM:] = jnp.ones((nk, HEAD_DIM), BF16)

    qcat_ref[:, :HEAD_DIM] = qn_ref[...]
    qcat_ref[:, HEAD_DIM:] = qr_ref[...]
    nt = (((1,), (1,)), ((), ()))

    def chunk(c, carry):
        off = pl.multiple_of(c * tk, tk)
        k = kcat_ref[pl.ds(off, tk), :]
        v = vcat_ref[pl.ds(off, tk), :]
        out = []
        for u in range(n_sub):
            m, acc = carry[u]
            s = lax.dot_general(qcat_ref[u * sub:(u + 1) * sub, :], k, nt, preferred_element_type=F32)
            m_new = jnp.maximum(m, jnp.max(s, axis=1, keepdims=True))
            p = jnp.exp2(s - m_new).astype(BF16)
            acc = jnp.exp2(m - m_new) * acc + jnp.dot(p, v, preferred_element_type=F32)
            out.append((m_new, acc))
        return tuple(out)

    init = tuple((jnp.full((sub, 1), NEG_INF, F32), jnp.zeros((sub, 2 * HEAD_DIM), F32)) for _ in range(n_sub))
    res = lax.fori_loop(0, nk // tk, chunk, init)
    for u in range(n_sub):
        acc = res[u][1]
        o_ref[u * sub:(u + 1) * sub, :] = (acc[:, :HEAD_DIM] / acc[:, HEAD_DIM:]).astype(o_ref.dtype)


def _mla_attn(qn, qr, kv, kr, *, n_heads, key_rows, key_block):
    r = qn.shape[0]
    tq = _pick(r, 1024, MLA_SUB) if r >= MLA_SUB else r
    sub = min(MLA_SUB, tq)
    tk = _pick(key_rows, 1280, 2 * LANES)
    kspec = lambda col0: pl.BlockSpec((key_rows, HEAD_DIM), lambda h, i: (key_block, col0 + h))
    qspec = pl.BlockSpec((tq, HEAD_DIM), lambda h, i: (i, h))
    return pl.pallas_call(
        functools.partial(_mla_attn_kernel, tk=tk, sub=sub),
        out_shape=jax.ShapeDtypeStruct((r, n_heads * HEAD_DIM), BF16),
        grid=(n_heads, r // tq),
        in_specs=[qspec, qspec, kspec(0), pl.BlockSpec((key_rows, LANES), lambda h, i: (key_block, 0)),
                  kspec(n_heads)],
        out_specs=pl.BlockSpec((tq, HEAD_DIM), lambda h, i: (i, h)),
        scratch_shapes=[pltpu.VMEM((key_rows, 2 * HEAD_DIM), BF16), pltpu.VMEM((key_rows, 2 * HEAD_DIM), BF16),
                        pltpu.VMEM((tq, 2 * HEAD_DIM), BF16)],
        compiler_params=_params(2, 60),
        name="mla_attn",
    )(qn, qr, kv, kr, kv)


def _router_kernel(x_ref, g_ref, sh_ref, sc_ref, wt_ref, aff_ref):
    h = _norm_mod(x_ref[...], g_ref[...], sh_ref[...], sc_ref[...])
    logits = lax.dot_general(wt_ref[...], h, (((1,), (1,)), ((), ())), precision=lax.Precision.HIGHEST,
                             preferred_element_type=F32)
    e = jnp.exp(logits - jnp.max(logits, axis=0, keepdims=True))
    aff_ref[...] = e / jnp.sum(e, axis=0, keepdims=True)


def _router(xr, g, shift, scale, w_router_t):
    r, d = xr.shape
    ne = w_router_t.shape[0]
    tr = _pick(r, 512, LANES)
    vec = pl.BlockSpec((1, d), lambda i: (0, 0))
    return pl.pallas_call(
        _router_kernel,
        out_shape=jax.ShapeDtypeStruct((ne, r), F32),
        grid=(r // tr,),
        in_specs=[pl.BlockSpec((tr, d), lambda i: (i, 0)), vec, vec, vec, pl.BlockSpec((ne, d), lambda i: (0, 0))],
        out_specs=pl.BlockSpec((ne, tr), lambda i: (0, i)),
        compiler_params=_params(1, 48),
        name="router",
    )(xr, g.reshape(1, d), shift, scale, w_router_t)


def _lane_cumsum(v):
    lane = lax.broadcasted_iota(jnp.int32, v.shape, 1)
    k = 1
    while k < LANES:
        v = v + jnp.where(lane >= k, pltpu.roll(v, k, 1), 0)
        k *= 2
    return v


def _select_kernel(a_ref, key_ref, *, cap, n_pad):
    a = a_ref[...]
    g = a.shape[0]
    bits = lax.bitcast_convert_type(a, jnp.int32)

    def count(mask):
        return jnp.sum(jnp.sum(mask.astype(F32), axis=0, keepdims=True), axis=1, keepdims=True)

    thr = jnp.zeros((1, 1), jnp.int32)
    for b in range(30, -1, -1):
        cand = thr | (1 << b)
        thr = jnp.where(count(bits >= cand) >= cap, cand, thr)
    above = bits > thr
    tie = bits == thr
    need = cap - count(above)
    tie_i = tie.astype(jnp.int32)
    incl = _lane_cumsum(tie_i)
    row_tot = jnp.broadcast_to(incl[:, LANES - 1:LANES].astype(F32), (g, LANES))
    lower = (lax.broadcasted_iota(jnp.int32, (g, g), 1) < lax.broadcasted_iota(jnp.int32, (g, g), 0)).astype(F32)
    row_off = jnp.dot(lower, row_tot, precision=lax.Precision.HIGHEST, preferred_element_type=F32)
    rank = (incl - tie_i).astype(F32) + row_off
    sel = above | (tie & (rank < need))
    tok = lax.broadcasted_iota(jnp.int32, a.shape, 0) * LANES + lax.broadcasted_iota(jnp.int32, a.shape, 1)
    key_ref[...] = jnp.where(sel, tok, n_pad)


def _select(aff_t, cap):
    ne, r = aff_t.shape
    assert r % LANES == 0
    n_pad = max(r, SUBLANES * LANES)
    g = n_pad // LANES
    a3 = jnp.pad(aff_t, ((0, 0), (0, n_pad - r))).reshape(ne, g, LANES)
    keys = pl.pallas_call(
        functools.partial(_select_kernel, cap=cap, n_pad=n_pad),
        out_shape=jax.ShapeDtypeStruct((ne, g, LANES), jnp.int32),
        grid=(ne,),
        in_specs=[pl.BlockSpec((None, g, LANES), lambda e: (e, 0, 0))],
        out_specs=pl.BlockSpec((None, g, LANES), lambda e: (e, 0, 0)),
        compiler_params=_params(1, 32),
        name="ec_select",
    )(a3)
    idx = jnp.sort(keys.reshape(ne, n_pad), axis=1)[:, :cap]
    return idx, jnp.take_along_axis(aff_t, idx, axis=1)


def _row_copy(src_hbm, buf, sem, slot, tok, r):
    return pltpu.make_async_copy(src_hbm.at[pl.ds(tok, 1), :], buf.at[slot, pl.ds(r, 1), :], sem.at[slot])


def _start_gather(idx_ref, src_hbm, buf, sem, step, slot, rows):
    def issue(r, carry):
        _row_copy(src_hbm, buf, sem, slot, idx_ref[step * rows + r], r).start()
        return carry
    lax.fori_loop(0, rows, issue, 0, unroll=8)


def _wait_rows(src_hbm, buf, sem, slot, rows):
    for _ in range(rows):
        _row_copy(src_hbm, buf, sem, slot, 0, 0).wait()


def _moe_ffn_kernel(idx_ref, gate_ref, g_ref, sh_ref, sc_ref, wg_ref, wu_ref, wd_ref, x_hbm, y_ref, xbuf, sem,
                    *, rows, n_steps):
    step = pl.program_id(0) * pl.num_programs(1) + pl.program_id(1)
    slot = step % 2

    @pl.when(step == 0)
    def _():
        _start_gather(idx_ref, x_hbm, xbuf, sem, 0, 0, rows)

    @pl.when(step + 1 < n_steps)
    def _():
        _start_gather(idx_ref, x_hbm, xbuf, sem, step + 1, 1 - slot, rows)

    _wait_rows(x_hbm, xbuf, sem, slot, rows)
    h = _norm_mod(xbuf[slot], g_ref[...], sh_ref[...], sc_ref[...]).astype(BF16)
    a = jnp.dot(h, wg_ref[...], preferred_element_type=F32)
    u = jnp.dot(h, wu_ref[...], preferred_element_type=F32)
    z = (a * _sigmoid(a) * u).astype(BF16)
    y_ref[...] = jnp.dot(z, wd_ref[...], preferred_element_type=F32) * gate_ref[...]


def _moe_ffn(idx_flat, gate_col, xr, g, shift, scale, wg, wu, wd, *, rows):
    r, d = xr.shape
    ne, _, ff = wg.shape
    slots = idx_flat.shape[0]
    nc = slots // ne // rows
    vec = pl.BlockSpec((1, d), lambda e, c, idx: (0, 0))
    return pl.pallas_call(
        functools.partial(_moe_ffn_kernel, rows=rows, n_steps=ne * nc),
        out_shape=jax.ShapeDtypeStruct((slots, d), F32),
        grid_spec=pltpu.PrefetchScalarGridSpec(
            num_scalar_prefetch=1, grid=(ne, nc),
            in_specs=[pl.BlockSpec((rows, 1), lambda e, c, idx: (e * nc + c, 0)), vec, vec, vec,
                      pl.BlockSpec((None, d, ff), lambda e, c, idx: (e, 0, 0)),
                      pl.BlockSpec((None, d, ff), lambda e, c, idx: (e, 0, 0)),
                      pl.BlockSpec((None, ff, d), lambda e, c, idx: (e, 0, 0)),
                      pl.BlockSpec(memory_space=pl.ANY)],
            out_specs=pl.BlockSpec((rows, d), lambda e, c, idx: (e * nc + c, 0)),
            scratch_shapes=[pltpu.VMEM((2, rows, d), F32), pltpu.SemaphoreType.DMA((2,))]),
        compiler_params=_params(2, 56),
        name="moe_ffn",
    )(idx_flat, gate_col, g.reshape(1, d), shift, scale, wg, wu, wd, xr)


def _moe_combine_kernel(idx_ref, y_ref, gate_ref, x_hbm, o_hbm, abuf, gsem, ssem, *, rows, n_steps):
    del x_hbm
    c = pl.program_id(1)
    nc = pl.num_programs(1)
    step = pl.program_id(0) * nc + c
    slot = step % 2

    def scatter_copy(s, sl, r):
        return pltpu.make_async_copy(abuf.at[sl, pl.ds(r, 1), :], o_hbm.at[pl.ds(idx_ref[s * rows + r], 1), :],
                                     ssem.at[sl])

    def wait_scatter(sl):
        for _ in range(rows):
            pltpu.make_async_copy(abuf.at[sl, pl.ds(0, 1), :], o_hbm.at[pl.ds(0, 1), :], ssem.at[sl]).wait()

    @pl.when(step == 0)
    def _():
        _start_gather(idx_ref, o_hbm, abuf, gsem, 0, 0, rows)

    _wait_rows(o_hbm, abuf, gsem, slot, rows)
    abuf[slot] = abuf[slot] + gate_ref[...] * y_ref[...]

    def issue(r, carry):
        scatter_copy(step, slot, r).start()
        return carry
    lax.fori_loop(0, rows, issue, 0, unroll=8)

    @pl.when(c > 0)
    def _():
        wait_scatter(1 - slot)

    @pl.when(c == nc - 1)
    def _():
        wait_scatter(slot)

    @pl.when(step + 1 < n_steps)
    def _():
        _start_gather(idx_ref, o_hbm, abuf, gsem, step + 1, 1 - slot, rows)


def _moe_combine(idx_flat, y, gate_vec, xr, *, n_experts, rows):
    r, d = xr.shape
    slots = idx_flat.shape[0]
    nc = slots // n_experts // rows
    return pl.pallas_call(
        functools.partial(_moe_combine_kernel, rows=rows, n_steps=n_experts * nc),
        out_shape=jax.ShapeDtypeStruct((r, d), F32),
        grid_spec=pltpu.PrefetchScalarGridSpec(
            num_scalar_prefetch=1, grid=(n_experts, nc),
            in_specs=[pl.BlockSpec((rows, d), lambda e, c, idx: (e * nc + c, 0)),
                      pl.BlockSpec((1, d), lambda e, c, idx: (0, 0)),
                      pl.BlockSpec(memory_space=pl.ANY)],
            out_specs=pl.BlockSpec(memory_space=pl.ANY),
            scratch_shapes=[pltpu.VMEM((2, rows, d), F32), pltpu.SemaphoreType.DMA((2,)),
                            pltpu.SemaphoreType.DMA((2,))]),
        input_output_aliases={3: 0},
        compiler_params=_params(2, 48),
        name="moe_combine",
    )(idx_flat, y, gate_vec, xr)


def _ec_moe(xr, g, shift, scale, out_gate, w_router_t, wg, wu, wd):
    r = xr.shape[0]
    ne = w_router_t.shape[0]
    cap = EC_CAPACITY * r // ne
    rows = _pick(cap, 256, SUBLANES)
    aff_t = _router(xr, g, shift, scale, w_router_t)
    idx, gate = _select(aff_t, cap)
    idx_flat = idx.reshape(-1)
    y = _moe_ffn(idx_flat, gate.reshape(-1, 1), xr, g, shift, scale, wg, wu, wd, rows=rows)
    return _moe_combine(idx_flat, y, out_gate, xr, n_experts=ne, rows=rows)


def _rope_tables(n, dim):
    half = dim // 2
    inv = ROPE_BASE ** (-jnp.arange(0, half, 2, dtype=F32) / half)
    rows = jnp.repeat(jnp.arange(n // GRID_W, dtype=F32), GRID_W)
    cols = (jnp.arange(n) % GRID_W).astype(F32)
    ar = rows[:, None] * inv[None, :]
    ac = cols[:, None] * inv[None, :]
    cos = jnp.concatenate([jnp.cos(ar), jnp.cos(ar), jnp.cos(ac), jnp.cos(ac)], axis=1)
    sin = jnp.concatenate([-jnp.sin(ar), jnp.sin(ar), -jnp.sin(ac), jnp.sin(ac)], axis=1)
    if dim < LANES:
        cos = jnp.concatenate([cos, jnp.ones((n, LANES - dim), F32)], axis=1)
        sin = jnp.concatenate([sin, jnp.zeros((n, LANES - dim), F32)], axis=1)
    return cos, sin


def _even_mixer(hx, hy, x, y, gate_x, gate_y, w_in, sink, dw, ln_g, ln_b, w_out, tables, need_ctx):
    ch = dw.shape[1]
    aq = w_out.shape[0] - ch
    akv = (w_in.shape[1] - aq - 2 * ch) // 2
    cos, sin = tables
    scale = HEAD_DIM ** -0.5
    pair = HEAD_DIM // 4
    wq = w_in[:, :aq].astype(BF16)
    wk = w_in[:, aq:aq + akv].astype(BF16)
    w_agv = jnp.concatenate([w_in[:, aq + 2 * akv:], w_in[:, aq + akv:aq + 2 * akv]], axis=1).astype(BF16)
    w_o1 = w_out[:aq].astype(BF16)
    w_o2 = w_out[aq:].astype(BF16)
    v_col0 = 2 * ch // HEAD_DIM

    qx = _matmul([hx], [wq], out_dtype=BF16, epilogue="rope", extras=(cos, sin), scale=scale, pair=pair)
    kx = _matmul([hx], [wk], out_dtype=BF16, epilogue="rope", extras=(cos, sin), pair=pair)
    agv_x = _matmul([hx], [w_agv], out_dtype=BF16)
    ky = _matmul([hy], [wk], out_dtype=BF16)
    agv_y = _matmul([hy], [w_agv], out_dtype=BF16)

    att_x = _gqa(qx, kx, agv_x, v_col0, ky, agv_y, sink, local=True)
    conv_x = _conformer_conv(agv_x, dw, ln_g, ln_b)
    x = _matmul([att_x, conv_x], [w_o1, w_o2], out_dtype=F32, epilogue="resid", extras=(x, gate_x), bn=512)
    if need_ctx:
        qy = _matmul([hy], [wq], out_dtype=BF16, scale=scale)
        att_y = _gqa(qy, None, None, v_col0, ky, agv_y, sink, local=False)
        conv_y = _conformer_conv(agv_y, dw, ln_g, ln_b)
        y = _matmul([att_y, conv_y], [w_o1, w_o2], out_dtype=F32, epilogue="resid", extras=(y, gate_y), bn=512)
    return x, y


def _mla_mixer(hx, hy, x, y, gate_x, gate_y, w_dn, q_g, kv_g, w_uq, w_ukv, w_o, tables, need_ctx):
    n, d = hx.shape
    lc = hy.shape[0]
    q_rank, kv_rank = q_g.shape[0], kv_g.shape[0]
    n_heads = w_o.shape[0] // HEAD_DIM
    rope_dim = w_dn.shape[1] - q_rank - kv_rank
    assert n % lc == 0 and rope_dim <= LANES
    cos, sin = tables
    pair = rope_dim // 4
    scale = (HEAD_DIM + rope_dim) ** -0.5 * LOG2_E
    w_dn_p = jnp.concatenate([w_dn, jnp.zeros((d, LANES - rope_dim), F32)], axis=1).astype(BF16)
    w_uq3 = w_uq.reshape(q_rank, n_heads, HEAD_DIM + rope_dim)
    w_qn = w_uq3[:, :, :HEAD_DIM].reshape(q_rank, n_heads * HEAD_DIM).astype(BF16)
    w_qr = jnp.concatenate([w_uq3[:, :, HEAD_DIM:], jnp.zeros((q_rank, n_heads, LANES - rope_dim), F32)],
                           axis=2).reshape(q_rank, n_heads * LANES).astype(BF16)
    w_kv3 = w_ukv.reshape(kv_rank, n_heads, 2 * HEAD_DIM)
    w_kv = jnp.concatenate([w_kv3[:, :, :HEAD_DIM].reshape(kv_rank, -1), w_kv3[:, :, HEAD_DIM:].reshape(kv_rank, -1)],
                           axis=1).astype(BF16)
    w_ob = w_o.astype(BF16)

    qlat_x, ckv_x, kr_x = _mla_down(hx, w_dn_p, q_g, kv_g, cos, sin, pair=pair, use_rope=True)
    qlat_y, ckv_y, kr_y = _mla_down(hy, w_dn_p, q_g, kv_g, cos[:lc], sin[:lc], pair=pair, use_rope=False)
    kv_all = _matmul([jnp.concatenate([ckv_x, ckv_y], axis=0)], [w_kv], out_dtype=BF16, bm=1280)
    kr_all = jnp.concatenate([kr_x, kr_y], axis=0)

    qn_x = _matmul([qlat_x], [w_qn], out_dtype=BF16, scale=scale)
    qr_x = _matmul([qlat_x], [w_qr], out_dtype=BF16, epilogue="rope", extras=(cos, sin), scale=scale, pair=pair)
    o_x = _mla_attn(qn_x, qr_x, kv_all, kr_all, n_heads=n_heads, key_rows=n + lc, key_block=0)
    x = _matmul([o_x], [w_ob], out_dtype=F32, epilogue="resid", extras=(x, gate_x), bn=512)
    if need_ctx:
        qn_y = _matmul([qlat_y], [w_qn], out_dtype=BF16, scale=scale)
        qr_y = _matmul([qlat_y], [w_qr], out_dtype=BF16, scale=scale)
        o_y = _mla_attn(qn_y, qr_y, kv_all, kr_all, n_heads=n_heads, key_rows=lc, key_block=n // lc)
        y = _matmul([o_y], [w_ob], out_dtype=F32, epilogue="resid", extras=(y, gate_y), bn=512)
    return x, y


def kernel(x, c, ctx, c_ctx, ada_w, ada_b, norm1_g, norm2_g, ev_w_in, ev_sink, ev_dw, ev_ln_g, ev_ln_b, ev_w_out, od_w_dn, od_q_norm_g, od_kv_norm_g, od_w_uq, od_w_ukv, od_w_o, moe_router, moe_w_gate, moe_w_up, moe_w_down, final_g):
    b, n, d = x.shape
    assert b == 1 and n % GRID_W == 0
    depth = ada_w.shape[0]
    xr, yr = x[0], ctx[0]
    mods = _adaln(c, c_ctx, ada_w, ada_b)
    tables_a = _rope_tables(n, HEAD_DIM)
    rope_dim = od_w_dn.shape[2] - od_q_norm_g.shape[1] - od_kv_norm_g.shape[1]
    tables_c = _rope_tables(n, rope_dim)
    for l in range(depth):
        need_ctx = l < depth - 1
        mx = [mods[l, 0:1, i * d:(i + 1) * d] for i in range(6)]
        my = [mods[l, 1:2, i * d:(i + 1) * d] for i in range(6)]
        hx = _normmod(xr, norm1_g[l], mx[0], mx[1], BF16)
        hy = _normmod(yr, norm1_g[l], my[0], my[1], BF16)
        i = l // 2
        if l % 2 == 0:
            xr, yr = _even_mixer(hx, hy, xr, yr, mx[2], my[2], ev_w_in[i], ev_sink[i], ev_dw[i], ev_ln_g[i],
                                 ev_ln_b[i], ev_w_out[i], tables_a, need_ctx)
        else:
            xr, yr = _mla_mixer(hx, hy, xr, yr, mx[2], my[2], od_w_dn[i], od_q_norm_g[i], od_kv_norm_g[i],
                                od_w_uq[i], od_w_ukv[i], od_w_o[i], tables_c, need_ctx)
        w_rt = moe_router[l].T
        wg, wu, wd = moe_w_gate[l].astype(BF16), moe_w_up[l].astype(BF16), moe_w_down[l].astype(BF16)
        xr = _ec_moe(xr, norm2_g[l], mx[3], mx[4], mx[5], w_rt, wg, wu, wd)
        if need_ctx:
            yr = _ec_moe(yr, norm2_g[l], my[3], my[4], my[5], w_rt, wg, wu, wd)
    zero = jnp.zeros((1, d), F32)
    return _normmod(xr, final_g, zero, zero, F32)[None]
```

```python
import functools

import jax
import jax.numpy as jnp
from jax import lax
from jax.experimental import pallas as pl
from jax.experimental.pallas import tpu as pltpu

F32 = jnp.float32
BF16 = jnp.bfloat16

GRID_W = 64
ROPE_BASE = 10000.0
NORM_EPS = 1e-6
NEG_INF = -1e30
HEAD_DIM = 128
A_WINDOW = 128
EC_CAPACITY = 2
LOG2_E = 1.4426950408889634

LANES = 128
SUBLANES = 8
BF16_ROWS = 16
MIB = 1024 * 1024


def _params(n_axes, vmem_mib):
    return pltpu.CompilerParams(dimension_semantics=("arbitrary",) * n_axes, vmem_limit_bytes=vmem_mib * MIB)


def _pick(total, target, mult):
    if total <= target:
        return total
    best = None
    for d in range(mult, target + 1, mult):
        if total % d == 0:
            best = d
    assert best is not None, (total, target, mult)
    return best


def _sigmoid(v):
    return 1.0 / (1.0 + jnp.exp(-v))


def _norm_mod(xv, g, shift, scale):
    yv = xv * lax.rsqrt(jnp.mean(xv * xv, axis=-1, keepdims=True) + NORM_EPS)
    return (yv * g) * (1.0 + scale) + shift


def _rope(xv, cos, sin, pair):
    lane = lax.broadcasted_iota(jnp.int32, xv.shape, 1)
    first = (lane % (2 * pair)) < pair
    partner = jnp.where(first, pltpu.roll(xv, LANES - pair, 1), pltpu.roll(xv, pair, 1))
    return xv * cos + partner * sin


def _adaln_kernel(cc_ref, w_ref, b_ref, o_ref):
    cc = cc_ref[...]
    s = cc * _sigmoid(cc)
    w = w_ref[...]
    b = b_ref[...]
    r0 = jnp.sum(s[:, 0:1] * w, axis=0, keepdims=True) + b
    r1 = jnp.sum(s[:, 1:2] * w, axis=0, keepdims=True) + b
    o_ref[...] = jnp.concatenate([r0, r1, jnp.zeros((SUBLANES - 2, w.shape[1]), F32)], axis=0)


def _adaln(c, c_ctx, ada_w, ada_b):
    depth, d, n6 = ada_w.shape
    cc = jnp.zeros((d, SUBLANES), F32).at[:, 0].set(c[0]).at[:, 1].set(c_ctx)
    tn = _pick(n6, 512, LANES)
    return pl.pallas_call(
        _adaln_kernel,
        out_shape=jax.ShapeDtypeStruct((depth, SUBLANES, n6), F32),
        grid=(depth, n6 // tn),
        in_specs=[pl.BlockSpec((d, SUBLANES), lambda l, j: (0, 0)),
                  pl.BlockSpec((None, d, tn), lambda l, j: (l, 0, j)),
                  pl.BlockSpec((None, 1, tn), lambda l, j: (l, 0, j))],
        out_specs=pl.BlockSpec((None, SUBLANES, tn), lambda l, j: (l, 0, j)),
        compiler_params=_params(2, 48),
        name="adaln",
    )(cc, ada_w, ada_b.reshape(depth, 1, n6))


def _normmod_kernel(x_ref, g_ref, sh_ref, sc_ref, o_ref):
    o_ref[...] = _norm_mod(x_ref[...], g_ref[...], sh_ref[...], sc_ref[...]).astype(o_ref.dtype)


def _normmod(xr, g, shift, scale, out_dtype):
    r, d = xr.shape
    tr = _pick(r, 512, BF16_ROWS)
    vec = pl.BlockSpec((1, d), lambda i: (0, 0))
    return pl.pallas_call(
        _normmod_kernel,
        out_shape=jax.ShapeDtypeStruct((r, d), out_dtype),
        grid=(r // tr,),
        in_specs=[pl.BlockSpec((tr, d), lambda i: (i, 0)), vec, vec, vec],
        out_specs=pl.BlockSpec((tr, d), lambda i: (i, 0)),
        compiler_params=_params(1, 48),
        name="normmod",
    )(xr, g.reshape(1, d), shift, scale)


def _mm_kernel(*refs, n_pairs, epilogue, scale, pair):
    a_refs, w_refs, rest = refs[:n_pairs], refs[n_pairs:2 * n_pairs], refs[2 * n_pairs:]
    acc = None
    for a_ref, w_ref in zip(a_refs, w_refs):
        part = jnp.dot(a_ref[...], w_ref[...], preferred_element_type=F32)
        acc = part if acc is None else acc + part
    if epilogue == "plain":
        (o_ref,) = rest
        o_ref[...] = (acc * scale if scale != 1.0 else acc).astype(o_ref.dtype)
    elif epilogue == "rope":
        cos_ref, sin_ref, o_ref = rest
        cos, sin = cos_ref[...], sin_ref[...]
        for h in range(acc.shape[1] // LANES):
            sl = slice(h * LANES, (h + 1) * LANES)
            r = _rope(acc[:, sl], cos, sin, pair)
            o_ref[:, sl] = (r * scale if scale != 1.0 else r).astype(o_ref.dtype)
    else:
        x_ref, gate_ref, o_ref = rest
        o_ref[...] = x_ref[...] + gate_ref[...] * acc


def _matmul(a_list, w_list, *, out_dtype, epilogue="plain", extras=(), scale=1.0, pair=0, bm=1024, bn=1024):
    m = a_list[0].shape[0]
    n = w_list[0].shape[1]
    bm = _pick(m, bm, BF16_ROWS)
    bn = _pick(n, bn, LANES)
    in_specs = [pl.BlockSpec((bm, a.shape[1]), lambda j, i: (i, 0)) for a in a_list]
    in_specs += [pl.BlockSpec((w.shape[0], bn), lambda j, i: (0, j)) for w in w_list]
    if epilogue == "rope":
        in_specs += [pl.BlockSpec((bm, LANES), lambda j, i: (i, 0))] * 2
    elif epilogue == "resid":
        in_specs += [pl.BlockSpec((bm, bn), lambda j, i: (i, j)), pl.BlockSpec((1, bn), lambda j, i: (0, j))]
    return pl.pallas_call(
        functools.partial(_mm_kernel, n_pairs=len(a_list), epilogue=epilogue, scale=scale, pair=pair),
        out_shape=jax.ShapeDtypeStruct((m, n), out_dtype),
        grid=(n // bn, m // bm),
        in_specs=in_specs,
        out_specs=pl.BlockSpec((bm, bn), lambda j, i: (i, j)),
        compiler_params=_params(2, 56),
        name="mm_" + epilogue,
    )(*a_list, *w_list, *extras)


def _gqa_kernel(sink_ref, *refs, groups, tq, local, n_tokens):
    if local:
        q_ref, k_ref, v_ref, ky_ref, vy_ref, o_ref = refs
    else:
        q_ref, ky_ref, vy_ref, o_ref = refs
    hk = pl.program_id(0)
    nt = (((1,), (1,)), ((), ()))
    ky = ky_ref[...]
    vy = vy_ref[...]
    if local:
        n = pl.program_id(1)
        win = tq + 2 * A_WINDOW
        start = pl.multiple_of(jnp.clip(n * tq - A_WINDOW, 0, n_tokens - win), BF16_ROWS)
        kw = k_ref[pl.ds(start, win), :]
        vw = v_ref[pl.ds(start, win), :]
        qpos = n * tq + lax.broadcasted_iota(jnp.int32, (tq, win), 0)
        kpos = start + lax.broadcasted_iota(jnp.int32, (tq, win), 1)
        band = jnp.abs(kpos - qpos) <= A_WINDOW
    for gi in range(groups):
        sl = slice(gi * HEAD_DIM, (gi + 1) * HEAD_DIM)
        qg = q_ref[:, sl]
        sink = sink_ref[hk * groups + gi]
        s_ctx = lax.dot_general(qg, ky, nt, preferred_element_type=F32)
        m = jnp.maximum(jnp.max(s_ctx, axis=1, keepdims=True), sink)
        if local:
            s_loc = jnp.where(band, lax.dot_general(qg, kw, nt, preferred_element_type=F32), NEG_INF)
            m = jnp.maximum(m, jnp.max(s_loc, axis=1, keepdims=True))
        p_ctx = jnp.exp(s_ctx - m)
        den = jnp.sum(p_ctx, axis=1, keepdims=True) + jnp.exp(sink - m)
        o = jnp.dot(p_ctx.astype(BF16), vy, preferred_element_type=F32)
        if local:
            p_loc = jnp.exp(s_loc - m)
            den = den + jnp.sum(p_loc, axis=1, keepdims=True)
            o = o + jnp.dot(p_loc.astype(BF16), vw, preferred_element_type=F32)
        o_ref[:, sl] = (o / den).astype(o_ref.dtype)


def _gqa(q, kx, vsrc_x, v_col0, ky, vsrc_y, sink, *, local):
    r, aq = q.shape
    hkv = ky.shape[1] // HEAD_DIM
    groups = aq // HEAD_DIM // hkv
    lc = ky.shape[0]
    gw = groups * HEAD_DIM
    tq = _pick(r, 256, BF16_ROWS)
    n_tokens = kx.shape[0] if local else 0
    if local:
        assert n_tokens >= tq + 2 * A_WINDOW
    q_spec = pl.BlockSpec((tq, gw), lambda h, i, s: (i, h))
    ctx_specs = [pl.BlockSpec((lc, HEAD_DIM), lambda h, i, s: (0, h)),
                 pl.BlockSpec((lc, HEAD_DIM), lambda h, i, s: (0, v_col0 + h))]
    if local:
        in_specs = [q_spec,
                    pl.BlockSpec((n_tokens, HEAD_DIM), lambda h, i, s: (0, h)),
                    pl.BlockSpec((n_tokens, HEAD_DIM), lambda h, i, s: (0, v_col0 + h))] + ctx_specs
        operands = (q, kx, vsrc_x, ky, vsrc_y)
    else:
        in_specs = [q_spec] + ctx_specs
        operands = (q, ky, vsrc_y)
    return pl.pallas_call(
        functools.partial(_gqa_kernel, groups=groups, tq=tq, local=local, n_tokens=n_tokens),
        out_shape=jax.ShapeDtypeStruct((r, aq), BF16),
        grid_spec=pltpu.PrefetchScalarGridSpec(
            num_scalar_prefetch=1, grid=(hkv, r // tq), in_specs=in_specs,
            out_specs=pl.BlockSpec((tq, gw), lambda h, i, s: (i, h))),
        compiler_params=_params(2, 48),
        name="gqa_local" if local else "gqa_ctx",
    )(sink, *operands)


CONV_HALO = 16
CONV_ROWS = 32
CONV_LANES = 256


def _conv_kernel(a_ref, g_ref, ap_ref, gp_ref, an_ref, gn_ref, dw_ref, lg_ref, lb_ref, o_ref, u_ref, z_ref,
                 *, taps, n_blocks):
    i = pl.program_id(0)
    tr, ch = z_ref.shape
    pad = (taps - 1) // 2

    def glu(a, g):
        return a.astype(F32) * _sigmoid(g.astype(F32))

    u_ref[0:CONV_HALO, :] = glu(ap_ref[...], gp_ref[...]) * (i > 0).astype(F32)
    u_ref[CONV_HALO:CONV_HALO + tr, :] = glu(a_ref[...], g_ref[...])
    u_ref[CONV_HALO + tr:, :] = glu(an_ref[...], gn_ref[...]) * (i < n_blocks - 1).astype(F32)

    def lane_chunk(cc, carry):
        c0 = pl.multiple_of(cc * CONV_LANES, CONV_LANES)
        for r0 in range(0, tr, CONV_ROWS):
            acc = jnp.zeros((CONV_ROWS, CONV_LANES), F32)
            for j in range(taps):
                acc = acc + (u_ref[pl.ds(r0 + j - pad + CONV_HALO, CONV_ROWS), pl.ds(c0, CONV_LANES)]
                             * dw_ref[pl.ds(j, 1), pl.ds(c0, CONV_LANES)])
            z_ref[pl.ds(r0, CONV_ROWS), pl.ds(c0, CONV_LANES)] = acc
        return carry

    lax.fori_loop(0, ch // CONV_LANES, lane_chunk, 0)
    z = z_ref[...]
    mu = jnp.mean(z, axis=-1, keepdims=True)
    zc = z - mu
    var = jnp.mean(zc * zc, axis=-1, keepdims=True)
    yv = zc * lax.rsqrt(var + NORM_EPS) * lg_ref[...] + lb_ref[...]
    o_ref[...] = (yv * _sigmoid(yv)).astype(o_ref.dtype)


def _conformer_conv(agv, dw, ln_g, ln_b):
    r = agv.shape[0]
    taps, ch = dw.shape
    assert (taps - 1) // 2 < CONV_HALO and ch % CONV_LANES == 0
    tr = _pick(r, 256, CONV_ROWS)
    nb = r // tr
    hb = tr // CONV_HALO
    last = r // CONV_HALO - 1
    dwp = jnp.zeros((2 * CONV_HALO, ch), F32).at[:taps].set(dw)
    cur = lambda col: pl.BlockSpec((tr, ch), lambda i: (i, col))
    prv = lambda col: pl.BlockSpec((CONV_HALO, ch), lambda i: (jnp.maximum(i * hb - 1, 0), col))
    nxt = lambda col: pl.BlockSpec((CONV_HALO, ch), lambda i: (jnp.minimum((i + 1) * hb, last), col))
    vec = pl.BlockSpec((1, ch), lambda i: (0, 0))
    return pl.pallas_call(
        functools.partial(_conv_kernel, taps=taps, n_blocks=nb),
        out_shape=jax.ShapeDtypeStruct((r, ch), BF16),
        grid=(nb,),
        in_specs=[cur(0), cur(1), prv(0), prv(1), nxt(0), nxt(1),
                  pl.BlockSpec((2 * CONV_HALO, ch), lambda i: (0, 0)), vec, vec],
        out_specs=pl.BlockSpec((tr, ch), lambda i: (i, 0)),
        scratch_shapes=[pltpu.VMEM((tr + 2 * CONV_HALO, ch), F32), pltpu.VMEM((tr, ch), F32)],
        compiler_params=_params(1, 48),
        name="conformer_conv",
    )(agv, agv, agv, agv, agv, agv, dwp, ln_g.reshape(1, ch), ln_b.reshape(1, ch))


def _mla_down_kernel(h_ref, w_ref, qg_ref, kg_ref, cos_ref, sin_ref, q_ref, kv_ref, kr_ref, *, q_rank, kv_rank,
                     pair, use_rope):
    acc = jnp.dot(h_ref[...], w_ref[...], preferred_element_type=F32)

    def rms(v, g):
        return v * lax.rsqrt(jnp.mean(v * v, axis=-1, keepdims=True) + NORM_EPS) * g

    q_ref[...] = rms(acc[:, :q_rank], qg_ref[...]).astype(q_ref.dtype)
    kv_ref[...] = rms(acc[:, q_rank:q_rank + kv_rank], kg_ref[...]).astype(kv_ref.dtype)
    kr = acc[:, q_rank + kv_rank:]
    if use_rope:
        kr = _rope(kr, cos_ref[...], sin_ref[...], pair)
    kr_ref[...] = kr.astype(kr_ref.dtype)


def _mla_down(h, w_dn_p, q_g, kv_g, cos, sin, *, pair, use_rope):
    m, d = h.shape
    q_rank, kv_rank = q_g.shape[0], kv_g.shape[0]
    wn = w_dn_p.shape[1]
    assert wn == q_rank + kv_rank + LANES
    bm = _pick(m, 512, BF16_ROWS)
    row = lambda w: pl.BlockSpec((bm, w), lambda i: (i, 0))
    return pl.pallas_call(
        functools.partial(_mla_down_kernel, q_rank=q_rank, kv_rank=kv_rank, pair=pair, use_rope=use_rope),
        out_shape=(jax.ShapeDtypeStruct((m, q_rank), BF16), jax.ShapeDtypeStruct((m, kv_rank), BF16),
                   jax.ShapeDtypeStruct((m, LANES), BF16)),
        grid=(m // bm,),
        in_specs=[row(d), pl.BlockSpec((d, wn), lambda i: (0, 0)),
                  pl.BlockSpec((1, q_rank), lambda i: (0, 0)), pl.BlockSpec((1, kv_rank), lambda i: (0, 0)),
                  row(LANES), row(LANES)],
        out_specs=(row(q_rank), row(kv_rank), row(LANES)),
        compiler_params=_params(1, 56),
        name="mla_down",
    )(h, w_dn_p, q_g.reshape(1, q_rank), kv_g.reshape(1, kv_rank), cos, sin)


MLA_SUB = 256
MLA_KEY_CHUNK = 1280
MLA_STAGES = 4


def _mla_attn_kernel(qn_ref, qr_ref, kn_ref, kr_ref, v_ref, o_ref, kcat_ref, vcat_ref, qcat_ref, s_ref, *, tk, sub):
    nk = kcat_ref.shape[0]
    tq = qn_ref.shape[0]
    n_sub = tq // sub

    @pl.when(pl.program_id(1) == 0)
    def _():
        kcat_ref[:, :HEAD_DIM] = kn_ref[...]
        kcat_ref[:, HEAD_DIM:] = kr_ref[...]
        vcat_ref[:, :HEAD_DIM] = v_ref[...]
        vcat_ref[:, HEAD_DIM:] = jnp.ones((nk, HEAD_DIM), BF16)

    qcat_ref[:, :HEAD_DIM] = qn_ref[...]
    qcat_ref[:, HEAD_DIM:] = qr_ref[...]
    nt = (((1,), (1,)), ((), ()))
    n_chunks = nk // tk

    def scores(buf, u, off):
        s_ref[buf, u] = lax.dot_general(qcat_ref[u * sub:(u + 1) * sub, :], kcat_ref[pl.ds(off, tk), :], nt,
                                        preferred_element_type=F32)

    def absorb(buf, u, off, m, acc):
        s = s_ref[buf, u]
        m_new = jnp.maximum(m, jnp.max(s, axis=1, keepdims=True))
        p = jnp.exp2(s - m_new).astype(BF16)
        acc = jnp.exp2(m - m_new) * acc + jnp.dot(p, vcat_ref[pl.ds(off, tk), :], preferred_element_type=F32)
        return m_new, acc

    def stage(buf, off, carry):
        out = []
        for u in range(n_sub):
            out.append(absorb(buf, u, off, *carry[u]))
            scores(1 - buf, u, off + tk)
        return tuple(out)

    def trip(j, carry):
        off = pl.multiple_of(j * (MLA_STAGES * tk), tk)
        for k in range(MLA_STAGES):
            carry = stage(k % 2, off + k * tk, carry)
        return carry

    for u in range(n_sub):
        scores(0, u, 0)
    carry = tuple((jnp.full((sub, 1), NEG_INF, F32), jnp.zeros((sub, 2 * HEAD_DIM), F32)) for _ in range(n_sub))
    n_stages = n_chunks - 1
    carry = lax.fori_loop(0, n_stages // MLA_STAGES, trip, carry)
    for k in range(n_stages - n_stages % MLA_STAGES, n_stages):
        carry = stage(k % 2, k * tk, carry)
    for u in range(n_sub):
        _, acc = absorb(n_stages % 2, u, n_stages * tk, *carry[u])
        o_ref[u * sub:(u + 1) * sub, :] = (acc[:, :HEAD_DIM] / acc[:, HEAD_DIM:]).astype(o_ref.dtype)


def _mla_attn(qn, qr, kv, kr, *, n_heads, key_rows, key_block):
    r = qn.shape[0]
    tq = _pick(r, 1024, MLA_SUB) if r >= MLA_SUB else r
    sub = min(MLA_SUB, tq)
    tk = _pick(key_rows, MLA_KEY_CHUNK, LANES)
    once = pl.Buffered(1)
    kspec = lambda col0: pl.BlockSpec((key_rows, HEAD_DIM), lambda h, i: (key_block, col0 + h), pipeline_mode=once)
    qspec = pl.BlockSpec((tq, HEAD_DIM), lambda h, i: (i, h))
    return pl.pallas_call(
        functools.partial(_mla_attn_kernel, tk=tk, sub=sub),
        out_shape=jax.ShapeDtypeStruct((r, n_heads * HEAD_DIM), BF16),
        grid=(n_heads, r // tq),
        in_specs=[qspec, qspec, kspec(0),
                  pl.BlockSpec((key_rows, LANES), lambda h, i: (key_block, 0), pipeline_mode=once), kspec(n_heads)],
        out_specs=pl.BlockSpec((tq, HEAD_DIM), lambda h, i: (i, h)),
        scratch_shapes=[pltpu.VMEM((key_rows, 2 * HEAD_DIM), BF16), pltpu.VMEM((key_rows, 2 * HEAD_DIM), BF16),
                        pltpu.VMEM((tq, 2 * HEAD_DIM), BF16), pltpu.VMEM((2, tq // sub, sub, tk), F32)],
        compiler_params=_params(2, 60),
        name="mla_attn",
    )(qn, qr, kv, kr, kv)


def _router_kernel(x_ref, g_ref, sh_ref, sc_ref, wt_ref, aff_ref):
    h = _norm_mod(x_ref[...], g_ref[...], sh_ref[...], sc_ref[...])
    logits = lax.dot_general(wt_ref[...], h, (((1,), (1,)), ((), ())), precision=lax.Precision.HIGHEST,
                             preferred_element_type=F32)
    e = jnp.exp(logits - jnp.max(logits, axis=0, keepdims=True))
    aff_ref[...] = e / jnp.sum(e, axis=0, keepdims=True)


def _router(xr, g, shift, scale, w_router_t):
    r, d = xr.shape
    ne = w_router_t.shape[0]
    tr = _pick(r, 512, LANES)
    vec = pl.BlockSpec((1, d), lambda i: (0, 0))
    return pl.pallas_call(
        _router_kernel,
        out_shape=jax.ShapeDtypeStruct((ne, r), F32),
        grid=(r // tr,),
        in_specs=[pl.BlockSpec((tr, d), lambda i: (i, 0)), vec, vec, vec, pl.BlockSpec((ne, d), lambda i: (0, 0))],
        out_specs=pl.BlockSpec((ne, tr), lambda i: (0, i)),
        compiler_params=_params(1, 48),
        name="router",
    )(xr, g.reshape(1, d), shift, scale, w_router_t)


def _lane_cumsum(v):
    lane = lax.broadcasted_iota(jnp.int32, v.shape, 1)
    k = 1
    while k < LANES:
        v = v + jnp.where(lane >= k, pltpu.roll(v, k, 1), 0)
        k *= 2
    return v


def _select_kernel(a_ref, key_ref, *, cap, n_pad):
    a = a_ref[...]
    g = a.shape[0]
    bits = lax.bitcast_convert_type(a, jnp.int32)

    def count(mask):
        return jnp.sum(jnp.sum(mask.astype(F32), axis=0, keepdims=True), axis=1, keepdims=True)

    thr = jnp.zeros((1, 1), jnp.int32)
    for b in range(30, -1, -1):
        cand = thr | (1 << b)
        thr = jnp.where(count(bits >= cand) >= cap, cand, thr)
    above = bits > thr
    tie = bits == thr
    need = cap - count(above)
    tie_i = tie.astype(jnp.int32)
    incl = _lane_cumsum(tie_i)
    row_tot = jnp.broadcast_to(incl[:, LANES - 1:LANES].astype(F32), (g, LANES))
    lower = (lax.broadcasted_iota(jnp.int32, (g, g), 1) < lax.broadcasted_iota(jnp.int32, (g, g), 0)).astype(F32)
    row_off = jnp.dot(lower, row_tot, precision=lax.Precision.HIGHEST, preferred_element_type=F32)
    rank = (incl - tie_i).astype(F32) + row_off
    sel = above | (tie & (rank < need))
    tok = lax.broadcasted_iota(jnp.int32, a.shape, 0) * LANES + lax.broadcasted_iota(jnp.int32, a.shape, 1)
    key_ref[...] = jnp.where(sel, tok, n_pad)


def _select(aff_t, cap):
    ne, r = aff_t.shape
    assert r % LANES == 0
    n_pad = max(r, SUBLANES * LANES)
    g = n_pad // LANES
    a3 = jnp.pad(aff_t, ((0, 0), (0, n_pad - r))).reshape(ne, g, LANES)
    keys = pl.pallas_call(
        functools.partial(_select_kernel, cap=cap, n_pad=n_pad),
        out_shape=jax.ShapeDtypeStruct((ne, g, LANES), jnp.int32),
        grid=(ne,),
        in_specs=[pl.BlockSpec((None, g, LANES), lambda e: (e, 0, 0))],
        out_specs=pl.BlockSpec((None, g, LANES), lambda e: (e, 0, 0)),
        compiler_params=_params(1, 32),
        name="ec_select",
    )(a3)
    idx = jnp.sort(keys.reshape(ne, n_pad), axis=1)[:, :cap]
    return idx, jnp.take_along_axis(aff_t, idx, axis=1)


def _row_copy(src_hbm, buf, sem, slot, tok, r):
    return pltpu.make_async_copy(src_hbm.at[pl.ds(tok, 1), :], buf.at[slot, pl.ds(r, 1), :], sem.at[slot])


def _start_gather(idx_ref, src_hbm, buf, sem, step, slot, rows):
    def issue(r, carry):
        _row_copy(src_hbm, buf, sem, slot, idx_ref[step * rows + r], r).start()
        return carry
    lax.fori_loop(0, rows, issue, 0, unroll=8)


def _wait_rows(src_hbm, buf, sem, slot, rows):
    for _ in range(rows):
        _row_copy(src_hbm, buf, sem, slot, 0, 0).wait()


def _moe_ffn_kernel(idx_ref, gate_ref, g_ref, sh_ref, sc_ref, wg_ref, wu_ref, wd_ref, x_hbm, y_ref, xbuf, sem,
                    *, rows, n_steps):
    step = pl.program_id(0) * pl.num_programs(1) + pl.program_id(1)
    slot = step % 2

    @pl.when(step == 0)
    def _():
        _start_gather(idx_ref, x_hbm, xbuf, sem, 0, 0, rows)

    @pl.when(step + 1 < n_steps)
    def _():
        _start_gather(idx_ref, x_hbm, xbuf, sem, step + 1, 1 - slot, rows)

    _wait_rows(x_hbm, xbuf, sem, slot, rows)
    h = _norm_mod(xbuf[slot], g_ref[...], sh_ref[...], sc_ref[...]).astype(BF16)
    a = jnp.dot(h, wg_ref[...], preferred_element_type=F32)
    u = jnp.dot(h, wu_ref[...], preferred_element_type=F32)
    z = (a * _sigmoid(a) * u).astype(BF16)
    y_ref[...] = jnp.dot(z, wd_ref[...], preferred_element_type=F32) * gate_ref[...]


def _moe_ffn(idx_flat, gate_col, xr, g, shift, scale, wg, wu, wd, *, rows):
    r, d = xr.shape
    ne, _, ff = wg.shape
    slots = idx_flat.shape[0]
    nc = slots // ne // rows
    vec = pl.BlockSpec((1, d), lambda e, c, idx: (0, 0))
    return pl.pallas_call(
        functools.partial(_moe_ffn_kernel, rows=rows, n_steps=ne * nc),
        out_shape=jax.ShapeDtypeStruct((slots, d), F32),
        grid_spec=pltpu.PrefetchScalarGridSpec(
            num_scalar_prefetch=1, grid=(ne, nc),
            in_specs=[pl.BlockSpec((rows, 1), lambda e, c, idx: (e * nc + c, 0)), vec, vec, vec,
                      pl.BlockSpec((None, d, ff), lambda e, c, idx: (e, 0, 0)),
                      pl.BlockSpec((None, d, ff), lambda e, c, idx: (e, 0, 0)),
                      pl.BlockSpec((None, ff, d), lambda e, c, idx: (e, 0, 0)),
                      pl.BlockSpec(memory_space=pl.ANY)],
            out_specs=pl.BlockSpec((rows, d), lambda e, c, idx: (e * nc + c, 0)),
            scratch_shapes=[pltpu.VMEM((2, rows, d), F32), pltpu.SemaphoreType.DMA((2,))]),
        compiler_params=_params(2, 56),
        name="moe_ffn",
    )(idx_flat, gate_col, g.reshape(1, d), shift, scale, wg, wu, wd, xr)


def _moe_combine_kernel(idx_ref, y_ref, gate_ref, x_hbm, o_hbm, abuf, gsem, ssem, *, rows, n_steps):
    del x_hbm
    c = pl.program_id(1)
    nc = pl.num_programs(1)
    step = pl.program_id(0) * nc + c
    slot = step % 2

    def scatter_copy(s, sl, r):
        return pltpu.make_async_copy(abuf.at[sl, pl.ds(r, 1), :], o_hbm.at[pl.ds(idx_ref[s * rows + r], 1), :],
                                     ssem.at[sl])

    def wait_scatter(sl):
        for _ in range(rows):
            pltpu.make_async_copy(abuf.at[sl, pl.ds(0, 1), :], o_hbm.at[pl.ds(0, 1), :], ssem.at[sl]).wait()

    @pl.when(step == 0)
    def _():
        _start_gather(idx_ref, o_hbm, abuf, gsem, 0, 0, rows)

    _wait_rows(o_hbm, abuf, gsem, slot, rows)
    abuf[slot] = abuf[slot] + gate_ref[...] * y_ref[...]

    def issue(r, carry):
        scatter_copy(step, slot, r).start()
        return carry
    lax.fori_loop(0, rows, issue, 0, unroll=8)

    @pl.when(c > 0)
    def _():
        wait_scatter(1 - slot)

    @pl.when(c == nc - 1)
    def _():
        wait_scatter(slot)

    @pl.when(step + 1 < n_steps)
    def _():
        _start_gather(idx_ref, o_hbm, abuf, gsem, step + 1, 1 - slot, rows)


def _moe_combine(idx_flat, y, gate_vec, xr, *, n_experts, rows):
    r, d = xr.shape
    slots = idx_flat.shape[0]
    nc = slots // n_experts // rows
    return pl.pallas_call(
        functools.partial(_moe_combine_kernel, rows=rows, n_steps=n_experts * nc),
        out_shape=jax.ShapeDtypeStruct((r, d), F32),
        grid_spec=pltpu.PrefetchScalarGridSpec(
            num_scalar_prefetch=1, grid=(n_experts, nc),
            in_specs=[pl.BlockSpec((rows, d), lambda e, c, idx: (e * nc + c, 0)),
                      pl.BlockSpec((1, d), lambda e, c, idx: (0, 0)),
                      pl.BlockSpec(memory_space=pl.ANY)],
            out_specs=pl.BlockSpec(memory_space=pl.ANY),
            scratch_shapes=[pltpu.VMEM((2, rows, d), F32), pltpu.SemaphoreType.DMA((2,)),
                            pltpu.SemaphoreType.DMA((2,))]),
        input_output_aliases={3: 0},
        compiler_params=_params(2, 48),
        name="moe_combine",
    )(idx_flat, y, gate_vec, xr)


def _ec_moe(xr, g, shift, scale, out_gate, w_router_t, wg, wu, wd):
    r = xr.shape[0]
    ne = w_router_t.shape[0]
    cap = EC_CAPACITY * r // ne
    rows = _pick(cap, 256, SUBLANES)
    aff_t = _router(xr, g, shift, scale, w_router_t)
    idx, gate = _select(aff_t, cap)
    idx_flat = idx.reshape(-1)
    y = _moe_ffn(idx_flat, gate.reshape(-1, 1), xr, g, shift, scale, wg, wu, wd, rows=rows)
    return _moe_combine(idx_flat, y, out_gate, xr, n_experts=ne, rows=rows)


def _rope_tables(n, dim):
    half = dim // 2
    inv = ROPE_BASE ** (-jnp.arange(0, half, 2, dtype=F32) / half)
    rows = jnp.repeat(jnp.arange(n // GRID_W, dtype=F32), GRID_W)
    cols = (jnp.arange(n) % GRID_W).astype(F32)
    ar = rows[:, None] * inv[None, :]
    ac = cols[:, None] * inv[None, :]
    cos = jnp.concatenate([jnp.cos(ar), jnp.cos(ar), jnp.cos(ac), jnp.cos(ac)], axis=1)
    sin = jnp.concatenate([-jnp.sin(ar), jnp.sin(ar), -jnp.sin(ac), jnp.sin(ac)], axis=1)
    if dim < LANES:
        cos = jnp.concatenate([cos, jnp.ones((n, LANES - dim), F32)], axis=1)
        sin = jnp.concatenate([sin, jnp.zeros((n, LANES - dim), F32)], axis=1)
    return cos, sin


def _even_mixer(hx, hy, x, y, gate_x, gate_y, w_in, sink, dw, ln_g, ln_b, w_out, tables, need_ctx):
    ch = dw.shape[1]
    aq = w_out.shape[0] - ch
    akv = (w_in.shape[1] - aq - 2 * ch) // 2
    cos, sin = tables
    scale = HEAD_DIM ** -0.5
    pair = HEAD_DIM // 4
    wq = w_in[:, :aq].astype(BF16)
    wk = w_in[:, aq:aq + akv].astype(BF16)
    w_agv = jnp.concatenate([w_in[:, aq + 2 * akv:], w_in[:, aq + akv:aq + 2 * akv]], axis=1).astype(BF16)
    w_o1 = w_out[:aq].astype(BF16)
    w_o2 = w_out[aq:].astype(BF16)
    v_col0 = 2 * ch // HEAD_DIM

    qx = _matmul([hx], [wq], out_dtype=BF16, epilogue="rope", extras=(cos, sin), scale=scale, pair=pair)
    kx = _matmul([hx], [wk], out_dtype=BF16, epilogue="rope", extras=(cos, sin), pair=pair)
    agv_x = _matmul([hx], [w_agv], out_dtype=BF16)
    ky = _matmul([hy], [wk], out_dtype=BF16)
    agv_y = _matmul([hy], [w_agv], out_dtype=BF16)

    att_x = _gqa(qx, kx, agv_x, v_col0, ky, agv_y, sink, local=True)
    conv_x = _conformer_conv(agv_x, dw, ln_g, ln_b)
    x = _matmul([att_x, conv_x], [w_o1, w_o2], out_dtype=F32, epilogue="resid", extras=(x, gate_x), bn=512)
    if need_ctx:
        qy = _matmul([hy], [wq], out_dtype=BF16, scale=scale)
        att_y = _gqa(qy, None, None, v_col0, ky, agv_y, sink, local=False)
        conv_y = _conformer_conv(agv_y, dw, ln_g, ln_b)
        y = _matmul([att_y, conv_y], [w_o1, w_o2], out_dtype=F32, epilogue="resid", extras=(y, gate_y), bn=512)
    return x, y


def _mla_mixer(hx, hy, x, y, gate_x, gate_y, w_dn, q_g, kv_g, w_uq, w_ukv, w_o, tables, need_ctx):
    n, d = hx.shape
    lc = hy.shape[0]
    q_rank, kv_rank = q_g.shape[0], kv_g.shape[0]
    n_heads = w_o.shape[0] // HEAD_DIM
    rope_dim = w_dn.shape[1] - q_rank - kv_rank
    assert n % lc == 0 and rope_dim <= LANES
    cos, sin = tables
    pair = rope_dim // 4
    scale = (HEAD_DIM + rope_dim) ** -0.5 * LOG2_E
    w_dn_p = jnp.concatenate([w_dn, jnp.zeros((d, LANES - rope_dim), F32)], axis=1).astype(BF16)
    w_uq3 = w_uq.reshape(q_rank, n_heads, HEAD_DIM + rope_dim)
    w_qn = w_uq3[:, :, :HEAD_DIM].reshape(q_rank, n_heads * HEAD_DIM).astype(BF16)
    w_qr = jnp.concatenate([w_uq3[:, :, HEAD_DIM:], jnp.zeros((q_rank, n_heads, LANES - rope_dim), F32)],
                           axis=2).reshape(q_rank, n_heads * LANES).astype(BF16)
    w_kv3 = w_ukv.reshape(kv_rank, n_heads, 2 * HEAD_DIM)
    w_kv = jnp.concatenate([w_kv3[:, :, :HEAD_DIM].reshape(kv_rank, -1), w_kv3[:, :, HEAD_DIM:].reshape(kv_rank, -1)],
                           axis=1).astype(BF16)
    w_ob = w_o.astype(BF16)

    qlat_x, ckv_x, kr_x = _mla_down(hx, w_dn_p, q_g, kv_g, cos, sin, pair=pair, use_rope=True)
    qlat_y, ckv_y, kr_y = _mla_down(hy, w_dn_p, q_g, kv_g, cos[:lc], sin[:lc], pair=pair, use_rope=False)
    kv_all = _matmul([jnp.concatenate([ckv_x, ckv_y], axis=0)], [w_kv], out_dtype=BF16, bm=1280)
    kr_all = jnp.concatenate([kr_x, kr_y], axis=0)

    qn_x = _matmul([qlat_x], [w_qn], out_dtype=BF16, scale=scale)
    qr_x = _matmul([qlat_x], [w_qr], out_dtype=BF16, epilogue="rope", extras=(cos, sin), scale=scale, pair=pair)
    o_x = _mla_attn(qn_x, qr_x, kv_all, kr_all, n_heads=n_heads, key_rows=n + lc, key_block=0)
    x = _matmul([o_x], [w_ob], out_dtype=F32, epilogue="resid", extras=(x, gate_x), bn=512)
    if need_ctx:
        qn_y = _matmul([qlat_y], [w_qn], out_dtype=BF16, scale=scale)
        qr_y = _matmul([qlat_y], [w_qr], out_dtype=BF16, scale=scale)
        o_y = _mla_attn(qn_y, qr_y, kv_all, kr_all, n_heads=n_heads, key_rows=lc, key_block=n // lc)
        y = _matmul([o_y], [w_ob], out_dtype=F32, epilogue="resid", extras=(y, gate_y), bn=512)
    return x, y


def kernel(x, c, ctx, c_ctx, ada_w, ada_b, norm1_g, norm2_g, ev_w_in, ev_sink, ev_dw, ev_ln_g, ev_ln_b, ev_w_out, od_w_dn, od_q_norm_g, od_kv_norm_g, od_w_uq, od_w_ukv, od_w_o, moe_router, moe_w_gate, moe_w_up, moe_w_down, final_g):
    b, n, d = x.shape
    assert b == 1 and n % GRID_W == 0
    depth = ada_w.shape[0]
    xr, yr = x[0], ctx[0]
    mods = _adaln(c, c_ctx, ada_w, ada_b)
    tables_a = _rope_tables(n, HEAD_DIM)
    rope_dim = od_w_dn.shape[2] - od_q_norm_g.shape[1] - od_kv_norm_g.shape[1]
    tables_c = _rope_tables(n, rope_dim)
    for l in range(depth):
        need_ctx = l < depth - 1
        mx = [mods[l, 0:1, i * d:(i + 1) * d] for i in range(6)]
        my = [mods[l, 1:2, i * d:(i + 1) * d] for i in range(6)]
        hx = _normmod(xr, norm1_g[l], mx[0], mx[1], BF16)
        hy = _normmod(yr, norm1_g[l], my[0], my[1], BF16)
        i = l // 2
        if l % 2 == 0:
            xr, yr = _even_mixer(hx, hy, xr, yr, mx[2], my[2], ev_w_in[i], ev_sink[i], ev_dw[i], ev_ln_g[i],
                                 ev_ln_b[i], ev_w_out[i], tables_a, need_ctx)
        else:
            xr, yr = _mla_mixer(hx, hy, xr, yr, mx[2], my[2], od_w_dn[i], od_q_norm_g[i], od_kv_norm_g[i],
                                od_w_uq[i], od_w_ukv[i], od_w_o[i], tables_c, need_ctx)
        w_rt = moe_router[l].T
        wg, wu, wd = moe_w_gate[l].astype(BF16), moe_w_up[l].astype(BF16), moe_w_down[l].astype(BF16)
        xr = _ec_moe(xr, norm2_g[l], mx[3], mx[4], mx[5], w_rt, wg, wu, wd)
        if need_ctx:
            yr = _ec_moe(yr, norm2_g[l], my[3], my[4], my[5], w_rt, wg, wu, wd)
    zero = jnp.zeros((1, d), F32)
    return _normmod(xr, final_g, zero, zero, F32)[None]
```

```python
import functools

import jax
import jax.numpy as jnp
from jax import lax
from jax.experimental import pallas as pl
from jax.experimental.pallas import tpu as pltpu

F32 = jnp.float32
BF16 = jnp.bfloat16

GRID_W = 64
ROPE_BASE = 10000.0
NORM_EPS = 1e-6
NEG_INF = -1e30
HEAD_DIM = 128
A_WINDOW = 128
EC_CAPACITY = 2
LOG2_E = 1.4426950408889634

LANES = 128
SUBLANES = 8
BF16_ROWS = 16
MIB = 1024 * 1024


def _params(n_axes, vmem_mib):
    return pltpu.CompilerParams(dimension_semantics=("arbitrary",) * n_axes, vmem_limit_bytes=vmem_mib * MIB)


def _pick(total, target, mult):
    if total <= target:
        return total
    best = None
    for d in range(mult, target + 1, mult):
        if total % d == 0:
            best = d
    assert best is not None, (total, target, mult)
    return best


def _sigmoid(v):
    return 1.0 / (1.0 + jnp.exp(-v))


def _norm_mod(xv, g, shift, scale):
    yv = xv * lax.rsqrt(jnp.mean(xv * xv, axis=-1, keepdims=True) + NORM_EPS)
    return (yv * g) * (1.0 + scale) + shift


def _rope(xv, cos, sin, pair):
    lane = lax.broadcasted_iota(jnp.int32, xv.shape, 1)
    first = (lane % (2 * pair)) < pair
    partner = jnp.where(first, pltpu.roll(xv, LANES - pair, 1), pltpu.roll(xv, pair, 1))
    return xv * cos + partner * sin


def _adaln_kernel(cc_ref, w_ref, b_ref, o_ref):
    cc = cc_ref[...]
    s = cc * _sigmoid(cc)
    w = w_ref[...]
    b = b_ref[...]
    r0 = jnp.sum(s[:, 0:1] * w, axis=0, keepdims=True) + b
    r1 = jnp.sum(s[:, 1:2] * w, axis=0, keepdims=True) + b
    o_ref[...] = jnp.concatenate([r0, r1, jnp.zeros((SUBLANES - 2, w.shape[1]), F32)], axis=0)


def _adaln(c, c_ctx, ada_w, ada_b):
    depth, d, n6 = ada_w.shape
    cc = jnp.zeros((d, SUBLANES), F32).at[:, 0].set(c[0]).at[:, 1].set(c_ctx)
    tn = _pick(n6, 512, LANES)
    return pl.pallas_call(
        _adaln_kernel,
        out_shape=jax.ShapeDtypeStruct((depth, SUBLANES, n6), F32),
        grid=(depth, n6 // tn),
        in_specs=[pl.BlockSpec((d, SUBLANES), lambda l, j: (0, 0)),
                  pl.BlockSpec((None, d, tn), lambda l, j: (l, 0, j)),
                  pl.BlockSpec((None, 1, tn), lambda l, j: (l, 0, j))],
        out_specs=pl.BlockSpec((None, SUBLANES, tn), lambda l, j: (l, 0, j)),
        compiler_params=_params(2, 48),
        name="adaln",
    )(cc, ada_w, ada_b.reshape(depth, 1, n6))


def _normmod_kernel(x_ref, g_ref, sh_ref, sc_ref, o_ref):
    o_ref[...] = _norm_mod(x_ref[...], g_ref[...], sh_ref[...], sc_ref[...]).astype(o_ref.dtype)


def _normmod(xr, g, shift, scale, out_dtype):
    r, d = xr.shape
    tr = _pick(r, 512, BF16_ROWS)
    vec = pl.BlockSpec((1, d), lambda i: (0, 0))
    return pl.pallas_call(
        _normmod_kernel,
        out_shape=jax.ShapeDtypeStruct((r, d), out_dtype),
        grid=(r // tr,),
        in_specs=[pl.BlockSpec((tr, d), lambda i: (i, 0)), vec, vec, vec],
        out_specs=pl.BlockSpec((tr, d), lambda i: (i, 0)),
        compiler_params=_params(1, 48),
        name="normmod",
    )(xr, g.reshape(1, d), shift, scale)


def _mm_kernel(*refs, n_pairs, epilogue, scale, pair):
    a_refs, w_refs, rest = refs[:n_pairs], refs[n_pairs:2 * n_pairs], refs[2 * n_pairs:]
    acc = None
    for a_ref, w_ref in zip(a_refs, w_refs):
        part = jnp.dot(a_ref[...], w_ref[...], preferred_element_type=F32)
        acc = part if acc is None else acc + part
    if epilogue == "plain":
        (o_ref,) = rest
        o_ref[...] = (acc * scale if scale != 1.0 else acc).astype(o_ref.dtype)
    elif epilogue == "rope":
        cos_ref, sin_ref, o_ref = rest
        cos, sin = cos_ref[...], sin_ref[...]
        for h in range(acc.shape[1] // LANES):
            sl = slice(h * LANES, (h + 1) * LANES)
            r = _rope(acc[:, sl], cos, sin, pair)
            o_ref[:, sl] = (r * scale if scale != 1.0 else r).astype(o_ref.dtype)
    else:
        x_ref, gate_ref, o_ref = rest
        o_ref[...] = x_ref[...] + gate_ref[...] * acc


def _matmul(a_list, w_list, *, out_dtype, epilogue="plain", extras=(), scale=1.0, pair=0, bm=1024, bn=1024):
    m = a_list[0].shape[0]
    n = w_list[0].shape[1]
    bm = _pick(m, bm, BF16_ROWS)
    bn = _pick(n, bn, LANES)
    in_specs = [pl.BlockSpec((bm, a.shape[1]), lambda j, i: (i, 0)) for a in a_list]
    in_specs += [pl.BlockSpec((w.shape[0], bn), lambda j, i: (0, j)) for w in w_list]
    if epilogue == "rope":
        in_specs += [pl.BlockSpec((bm, LANES), lambda j, i: (i, 0))] * 2
    elif epilogue == "resid":
        in_specs += [pl.BlockSpec((bm, bn), lambda j, i: (i, j)), pl.BlockSpec((1, bn), lambda j, i: (0, j))]
    return pl.pallas_call(
        functools.partial(_mm_kernel, n_pairs=len(a_list), epilogue=epilogue, scale=scale, pair=pair),
        out_shape=jax.ShapeDtypeStruct((m, n), out_dtype),
        grid=(n // bn, m // bm),
        in_specs=in_specs,
        out_specs=pl.BlockSpec((bm, bn), lambda j, i: (i, j)),
        compiler_params=_params(2, 56),
        name="mm_" + epilogue,
    )(*a_list, *w_list, *extras)


def _gqa_kernel(sink_ref, *refs, groups, tq, local, n_tokens):
    if local:
        q_ref, k_ref, v_ref, ky_ref, vy_ref, o_ref = refs
    else:
        q_ref, ky_ref, vy_ref, o_ref = refs
    hk = pl.program_id(0)
    nt = (((1,), (1,)), ((), ()))
    ky = ky_ref[...]
    vy = vy_ref[...]
    if local:
        n = pl.program_id(1)
        win = tq + 2 * A_WINDOW
        start = pl.multiple_of(jnp.clip(n * tq - A_WINDOW, 0, n_tokens - win), BF16_ROWS)
        kw = k_ref[pl.ds(start, win), :]
        vw = v_ref[pl.ds(start, win), :]
        qpos = n * tq + lax.broadcasted_iota(jnp.int32, (tq, win), 0)
        kpos = start + lax.broadcasted_iota(jnp.int32, (tq, win), 1)
        band = jnp.abs(kpos - qpos) <= A_WINDOW
    for gi in range(groups):
        sl = slice(gi * HEAD_DIM, (gi + 1) * HEAD_DIM)
        qg = q_ref[:, sl]
        sink = sink_ref[hk * groups + gi]
        s_ctx = lax.dot_general(qg, ky, nt, preferred_element_type=F32)
        m = jnp.maximum(jnp.max(s_ctx, axis=1, keepdims=True), sink)
        if local:
            s_loc = jnp.where(band, lax.dot_general(qg, kw, nt, preferred_element_type=F32), NEG_INF)
            m = jnp.maximum(m, jnp.max(s_loc, axis=1, keepdims=True))
        p_ctx = jnp.exp(s_ctx - m)
        den = jnp.sum(p_ctx, axis=1, keepdims=True) + jnp.exp(sink - m)
        o = jnp.dot(p_ctx.astype(BF16), vy, preferred_element_type=F32)
        if local:
            p_loc = jnp.exp(s_loc - m)
            den = den + jnp.sum(p_loc, axis=1, keepdims=True)
            o = o + jnp.dot(p_loc.astype(BF16), vw, preferred_element_type=F32)
        o_ref[:, sl] = (o / den).astype(o_ref.dtype)


def _gqa(q, kx, vsrc_x, v_col0, ky, vsrc_y, sink, *, local):
    r, aq = q.shape
    hkv = ky.shape[1] // HEAD_DIM
    groups = aq // HEAD_DIM // hkv
    lc = ky.shape[0]
    gw = groups * HEAD_DIM
    tq = _pick(r, 256, BF16_ROWS)
    n_tokens = kx.shape[0] if local else 0
    if local:
        assert n_tokens >= tq + 2 * A_WINDOW
    q_spec = pl.BlockSpec((tq, gw), lambda h, i, s: (i, h))
    ctx_specs = [pl.BlockSpec((lc, HEAD_DIM), lambda h, i, s: (0, h)),
                 pl.BlockSpec((lc, HEAD_DIM), lambda h, i, s: (0, v_col0 + h))]
    if local:
        in_specs = [q_spec,
                    pl.BlockSpec((n_tokens, HEAD_DIM), lambda h, i, s: (0, h)),
                    pl.BlockSpec((n_tokens, HEAD_DIM), lambda h, i, s: (0, v_col0 + h))] + ctx_specs
        operands = (q, kx, vsrc_x, ky, vsrc_y)
    else:
        in_specs = [q_spec] + ctx_specs
        operands = (q, ky, vsrc_y)
    return pl.pallas_call(
        functools.partial(_gqa_kernel, groups=groups, tq=tq, local=local, n_tokens=n_tokens),
        out_shape=jax.ShapeDtypeStruct((r, aq), BF16),
        grid_spec=pltpu.PrefetchScalarGridSpec(
            num_scalar_prefetch=1, grid=(hkv, r // tq), in_specs=in_specs,
            out_specs=pl.BlockSpec((tq, gw), lambda h, i, s: (i, h))),
        compiler_params=_params(2, 48),
        name="gqa_local" if local else "gqa_ctx",
    )(sink, *operands)


CONV_HALO = 16
CONV_ROWS = 32
CONV_LANES = 256


def _conv_kernel(a_ref, g_ref, ap_ref, gp_ref, an_ref, gn_ref, dw_ref, lg_ref, lb_ref, o_ref, u_ref, z_ref,
                 *, taps, n_blocks):
    i = pl.program_id(0)
    tr, ch = z_ref.shape
    pad = (taps - 1) // 2

    def glu(a, g):
        return a.astype(F32) * _sigmoid(g.astype(F32))

    u_ref[0, 0:CONV_HALO, :] = glu(ap_ref[...], gp_ref[...]) * (i > 0).astype(F32)
    u_ref[0, CONV_HALO:CONV_HALO + tr, :] = glu(a_ref[...], g_ref[...])
    u_ref[0, CONV_HALO + tr:, :] = glu(an_ref[...], gn_ref[...]) * (i < n_blocks - 1).astype(F32)
    span = u_ref.shape[1] - SUBLANES
    for s in range(1, SUBLANES):
        u_ref[s, 0:span, :] = u_ref[0, s:s + span, :]

    def lane_chunk(cc, carry):
        c0 = pl.multiple_of(cc * CONV_LANES, CONV_LANES)
        for r0 in range(0, tr, CONV_ROWS):
            acc = jnp.zeros((CONV_ROWS, CONV_LANES), F32)
            for j in range(taps):
                shift = j - pad + CONV_HALO
                acc = acc + (u_ref[shift % SUBLANES, pl.ds(r0 + shift - shift % SUBLANES, CONV_ROWS),
                                   pl.ds(c0, CONV_LANES)]
                             * dw_ref[pl.ds(j, 1), pl.ds(c0, CONV_LANES)])
            z_ref[pl.ds(r0, CONV_ROWS), pl.ds(c0, CONV_LANES)] = acc
        return carry

    lax.fori_loop(0, ch // CONV_LANES, lane_chunk, 0)
    z = z_ref[...]
    mu = jnp.mean(z, axis=-1, keepdims=True)
    zc = z - mu
    var = jnp.mean(zc * zc, axis=-1, keepdims=True)
    yv = zc * lax.rsqrt(var + NORM_EPS) * lg_ref[...] + lb_ref[...]
    o_ref[...] = (yv * _sigmoid(yv)).astype(o_ref.dtype)


def _conformer_conv(agv, dw, ln_g, ln_b):
    r = agv.shape[0]
    taps, ch = dw.shape
    assert (taps - 1) // 2 < CONV_HALO and ch % CONV_LANES == 0
    tr = _pick(r, 256, CONV_ROWS)
    nb = r // tr
    hb = tr // CONV_HALO
    last = r // CONV_HALO - 1
    dwp = jnp.zeros((2 * CONV_HALO, ch), F32).at[:taps].set(dw)
    cur = lambda col: pl.BlockSpec((tr, ch), lambda i: (i, col))
    prv = lambda col: pl.BlockSpec((CONV_HALO, ch), lambda i: (jnp.maximum(i * hb - 1, 0), col))
    nxt = lambda col: pl.BlockSpec((CONV_HALO, ch), lambda i: (jnp.minimum((i + 1) * hb, last), col))
    vec = pl.BlockSpec((1, ch), lambda i: (0, 0))
    return pl.pallas_call(
        functools.partial(_conv_kernel, taps=taps, n_blocks=nb),
        out_shape=jax.ShapeDtypeStruct((r, ch), BF16),
        grid=(nb,),
        in_specs=[cur(0), cur(1), prv(0), prv(1), nxt(0), nxt(1),
                  pl.BlockSpec((2 * CONV_HALO, ch), lambda i: (0, 0)), vec, vec],
        out_specs=pl.BlockSpec((tr, ch), lambda i: (i, 0)),
        scratch_shapes=[pltpu.VMEM((SUBLANES, tr + 2 * CONV_HALO, ch), F32), pltpu.VMEM((tr, ch), F32)],
        compiler_params=_params(1, 48),
        name="conformer_conv",
    )(agv, agv, agv, agv, agv, agv, dwp, ln_g.reshape(1, ch), ln_b.reshape(1, ch))


def _mla_down_kernel(h_ref, w_ref, qg_ref, kg_ref, cos_ref, sin_ref, q_ref, kv_ref, kr_ref, *, q_rank, kv_rank,
                     pair, use_rope):
    acc = jnp.dot(h_ref[...], w_ref[...], preferred_element_type=F32)

    def rms(v, g):
        return v * lax.rsqrt(jnp.mean(v * v, axis=-1, keepdims=True) + NORM_EPS) * g

    q_ref[...] = rms(acc[:, :q_rank], qg_ref[...]).astype(q_ref.dtype)
    kv_ref[...] = rms(acc[:, q_rank:q_rank + kv_rank], kg_ref[...]).astype(kv_ref.dtype)
    kr = acc[:, q_rank + kv_rank:]
    if use_rope:
        kr = _rope(kr, cos_ref[...], sin_ref[...], pair)
    kr_ref[...] = kr.astype(kr_ref.dtype)


def _mla_down(h, w_dn_p, q_g, kv_g, cos, sin, *, pair, use_rope):
    m, d = h.shape
    q_rank, kv_rank = q_g.shape[0], kv_g.shape[0]
    wn = w_dn_p.shape[1]
    assert wn == q_rank + kv_rank + LANES
    bm = _pick(m, 512, BF16_ROWS)
    row = lambda w: pl.BlockSpec((bm, w), lambda i: (i, 0))
    return pl.pallas_call(
        functools.partial(_mla_down_kernel, q_rank=q_rank, kv_rank=kv_rank, pair=pair, use_rope=use_rope),
        out_shape=(jax.ShapeDtypeStruct((m, q_rank), BF16), jax.ShapeDtypeStruct((m, kv_rank), BF16),
                   jax.ShapeDtypeStruct((m, LANES), BF16)),
        grid=(m // bm,),
        in_specs=[row(d), pl.BlockSpec((d, wn), lambda i: (0, 0)),
                  pl.BlockSpec((1, q_rank), lambda i: (0, 0)), pl.BlockSpec((1, kv_rank), lambda i: (0, 0)),
                  row(LANES), row(LANES)],
        out_specs=(row(q_rank), row(kv_rank), row(LANES)),
        compiler_params=_params(1, 56),
        name="mla_down",
    )(h, w_dn_p, q_g.reshape(1, q_rank), kv_g.reshape(1, kv_rank), cos, sin)


MLA_TQ = 1024
MLA_SUB = 256
MLA_KEY_CHUNK = 1280
MLA_STAGES = 4


def _mla_attn_kernel(qn_ref, qr_ref, kn_ref, kr_ref, v_ref, o_ref, kcat_ref, vcat_ref, qcat_ref, s_ref, *, tk, sub):
    nk = kcat_ref.shape[0]
    tq = qn_ref.shape[0]
    n_sub = tq // sub

    @pl.when(pl.program_id(1) == 0)
    def _():
        kcat_ref[:, :HEAD_DIM] = kn_ref[...]
        kcat_ref[:, HEAD_DIM:] = kr_ref[...]
        vcat_ref[:, :HEAD_DIM] = v_ref[...]
        vcat_ref[:, HEAD_DIM:] = jnp.ones((nk, HEAD_DIM), BF16)

    qcat_ref[:, :HEAD_DIM] = qn_ref[...]
    qcat_ref[:, HEAD_DIM:] = qr_ref[...]
    nt = (((1,), (1,)), ((), ()))
    n_chunks = nk // tk

    def scores(buf, u, off):
        s_ref[buf, u] = lax.dot_general(qcat_ref[u * sub:(u + 1) * sub, :], kcat_ref[pl.ds(off, tk), :], nt,
                                        preferred_element_type=F32)

    def absorb(buf, u, off, m, acc):
        s = s_ref[buf, u]
        m_new = jnp.maximum(m, jnp.max(s, axis=1, keepdims=True))
        p = jnp.exp2(s - m_new).astype(BF16)
        acc = jnp.exp2(m - m_new) * acc + jnp.dot(p, vcat_ref[pl.ds(off, tk), :], preferred_element_type=F32)
        return m_new, acc

    def stage(buf, off, carry):
        out = []
        for u in range(n_sub):
            out.append(absorb(buf, u, off, *carry[u]))
            scores(1 - buf, u, off + tk)
        return tuple(out)

    def trip(j, carry):
        off = pl.multiple_of(j * (MLA_STAGES * tk), tk)
        for k in range(MLA_STAGES):
            carry = stage(k % 2, off + k * tk, carry)
        return carry

    for u in range(n_sub):
        scores(0, u, 0)
    carry = tuple((jnp.full((sub, 1), NEG_INF, F32), jnp.zeros((sub, 2 * HEAD_DIM), F32)) for _ in range(n_sub))
    n_stages = n_chunks - 1
    carry = lax.fori_loop(0, n_stages // MLA_STAGES, trip, carry)
    for k in range(n_stages - n_stages % MLA_STAGES, n_stages):
        carry = stage(k % 2, k * tk, carry)
    for u in range(n_sub):
        _, acc = absorb(n_stages % 2, u, n_stages * tk, *carry[u])
        o_ref[u * sub:(u + 1) * sub, :] = (acc[:, :HEAD_DIM] / acc[:, HEAD_DIM:]).astype(o_ref.dtype)


def _mla_attn(qn, qr, kv, kr, *, n_heads, key_rows, key_block):
    r = qn.shape[0]
    tq = _pick(r, MLA_TQ, MLA_SUB) if r >= MLA_SUB else r
    sub = min(MLA_SUB, tq)
    tk = _pick(key_rows, MLA_KEY_CHUNK, LANES)
    once = pl.Buffered(1)
    kspec = lambda col0: pl.BlockSpec((key_rows, HEAD_DIM), lambda h, i: (key_block, col0 + h), pipeline_mode=once)
    qspec = pl.BlockSpec((tq, HEAD_DIM), lambda h, i: (i, h))
    return pl.pallas_call(
        functools.partial(_mla_attn_kernel, tk=tk, sub=sub),
        out_shape=jax.ShapeDtypeStruct((r, n_heads * HEAD_DIM), BF16),
        grid=(n_heads, r // tq),
        in_specs=[qspec, qspec, kspec(0),
                  pl.BlockSpec((key_rows, LANES), lambda h, i: (key_block, 0), pipeline_mode=once), kspec(n_heads)],
        out_specs=pl.BlockSpec((tq, HEAD_DIM), lambda h, i: (i, h)),
        scratch_shapes=[pltpu.VMEM((key_rows, 2 * HEAD_DIM), BF16), pltpu.VMEM((key_rows, 2 * HEAD_DIM), BF16),
                        pltpu.VMEM((tq, 2 * HEAD_DIM), BF16), pltpu.VMEM((2, tq // sub, sub, tk), F32)],
        compiler_params=_params(2, 60),
        name="mla_attn",
    )(qn, qr, kv, kr, kv)


def _router_kernel(x_ref, g_ref, sh_ref, sc_ref, wt_ref, aff_ref):
    h = _norm_mod(x_ref[...], g_ref[...], sh_ref[...], sc_ref[...])
    w = wt_ref[...]
    ne = w.shape[0]
    h_hi = h.astype(BF16)
    h_lo = (h - h_hi.astype(F32)).astype(BF16)
    w_hi = w.astype(BF16)
    w_lo = (w - w_hi.astype(F32)).astype(BF16)
    nt = (((1,), (1,)), ((), ()))
    both = lax.dot_general(jnp.concatenate([w_hi, w_lo], axis=0), h_hi, nt, preferred_element_type=F32)
    logits = both[:ne] + both[ne:] + lax.dot_general(w_hi, h_lo, nt, preferred_element_type=F32)
    e = jnp.exp(logits - jnp.max(logits, axis=0, keepdims=True))
    aff_ref[...] = e / jnp.sum(e, axis=0, keepdims=True)


def _router(xr, g, shift, scale, w_router_t):
    r, d = xr.shape
    ne = w_router_t.shape[0]
    tr = _pick(r, 512, LANES)
    vec = pl.BlockSpec((1, d), lambda i: (0, 0))
    return pl.pallas_call(
        _router_kernel,
        out_shape=jax.ShapeDtypeStruct((ne, r), F32),
        grid=(r // tr,),
        in_specs=[pl.BlockSpec((tr, d), lambda i: (i, 0)), vec, vec, vec, pl.BlockSpec((ne, d), lambda i: (0, 0))],
        out_specs=pl.BlockSpec((ne, tr), lambda i: (0, i)),
        compiler_params=_params(1, 48),
        name="router",
    )(xr, g.reshape(1, d), shift, scale, w_router_t)


def _lane_cumsum(v):
    lane = lax.broadcasted_iota(jnp.int32, v.shape, 1)
    k = 1
    while k < LANES:
        v = v + jnp.where(lane >= k, pltpu.roll(v, k, 1), 0)
        k *= 2
    return v


def _select_kernel(a_ref, key_ref, *, cap, n_pad):
    a = a_ref[...]
    ne, g, _ = a.shape
    bits = lax.bitcast_convert_type(a, jnp.int32)

    def count(mask):
        return jnp.sum(jnp.sum(mask.astype(F32), axis=2, keepdims=True), axis=1, keepdims=True)

    thr = jnp.zeros((ne, 1, 1), jnp.int32)
    for b in range(30, -1, -1):
        cand = thr | (1 << b)
        thr = jnp.where(count(bits >= cand) >= cap, cand, thr)
    above = bits > thr
    tie = bits == thr
    need = cap - count(above)
    tie_i = tie.astype(jnp.int32)
    incl = _lane_cumsum(tie_i.reshape(ne * g, LANES)).reshape(ne, g, LANES)
    lower = (lax.broadcasted_iota(jnp.int32, (g, g), 1) < lax.broadcasted_iota(jnp.int32, (g, g), 0)).astype(F32)
    tok = lax.broadcasted_iota(jnp.int32, (g, LANES), 0) * LANES + lax.broadcasted_iota(jnp.int32, (g, LANES), 1)
    for e in range(ne):
        row_tot = jnp.broadcast_to(incl[e, :, LANES - 1:LANES].astype(F32), (g, LANES))
        row_off = jnp.dot(lower, row_tot, precision=lax.Precision.HIGHEST, preferred_element_type=F32)
        rank = (incl[e] - tie_i[e]).astype(F32) + row_off
        sel = above[e] | (tie[e] & (rank < need[e]))
        key_ref[e] = jnp.where(sel, tok, n_pad)


def _select(aff_t, cap):
    ne, r = aff_t.shape
    assert r % LANES == 0
    n_pad = max(r, SUBLANES * LANES)
    g = n_pad // LANES
    a3 = jnp.pad(aff_t, ((0, 0), (0, n_pad - r))).reshape(ne, g, LANES)
    keys = pl.pallas_call(
        functools.partial(_select_kernel, cap=cap, n_pad=n_pad),
        out_shape=jax.ShapeDtypeStruct((ne, g, LANES), jnp.int32),
        grid=(1,),
        in_specs=[pl.BlockSpec((ne, g, LANES), lambda i: (0, 0, 0))],
        out_specs=pl.BlockSpec((ne, g, LANES), lambda i: (0, 0, 0)),
        compiler_params=_params(1, 32),
        name="ec_select",
    )(a3)
    idx = jnp.sort(keys.reshape(ne, n_pad), axis=1)[:, :cap]
    return idx, jnp.take_along_axis(aff_t, idx, axis=1)


def _row_copy(src_hbm, buf, sem, slot, tok, r):
    return pltpu.make_async_copy(src_hbm.at[pl.ds(tok, 1), :], buf.at[slot, pl.ds(r, 1), :], sem.at[slot])


def _start_gather(idx_ref, src_hbm, buf, sem, step, slot, rows):
    def issue(r, carry):
        _row_copy(src_hbm, buf, sem, slot, idx_ref[step * rows + r], r).start()
        return carry
    lax.fori_loop(0, rows, issue, 0, unroll=8)


def _wait_rows(src_hbm, buf, sem, slot, rows):
    for _ in range(rows):
        _row_copy(src_hbm, buf, sem, slot, 0, 0).wait()


def _moe_ffn_kernel(idx_ref, gate_ref, g_ref, sh_ref, sc_ref, wgu_ref, wd_ref, x_hbm, y_ref, xbuf, sem,
                    *, rows, n_steps):
    step = pl.program_id(0) * pl.num_programs(1) + pl.program_id(1)
    slot = step % 2

    @pl.when(step == 0)
    def _():
        _start_gather(idx_ref, x_hbm, xbuf, sem, 0, 0, rows)

    @pl.when(step + 1 < n_steps)
    def _():
        _start_gather(idx_ref, x_hbm, xbuf, sem, step + 1, 1 - slot, rows)

    _wait_rows(x_hbm, xbuf, sem, slot, rows)
    h = _norm_mod(xbuf[slot], g_ref[...], sh_ref[...], sc_ref[...]).astype(BF16)
    ff = wd_ref.shape[0]
    halves = []
    for hh in (h[:rows // 2], h[rows // 2:]):
        au = jnp.dot(hh, wgu_ref[...], preferred_element_type=F32)
        a, u = au[:, :ff], au[:, ff:]
        halves.append((a * _sigmoid(a) * u).astype(BF16))
    z = jnp.concatenate(halves, axis=0)
    y_ref[...] = jnp.dot(z, wd_ref[...], preferred_element_type=F32) * gate_ref[...]


def _moe_ffn(idx_flat, gate_col, xr, g, shift, scale, wgu, wd, *, rows):
    r, d = xr.shape
    ne, ff, _ = wd.shape
    assert ff % LANES == 0
    slots = idx_flat.shape[0]
    nc = slots // ne // rows
    vec = pl.BlockSpec((1, d), lambda e, c, idx: (0, 0))
    return pl.pallas_call(
        functools.partial(_moe_ffn_kernel, rows=rows, n_steps=ne * nc),
        out_shape=jax.ShapeDtypeStruct((slots, d), F32),
        grid_spec=pltpu.PrefetchScalarGridSpec(
            num_scalar_prefetch=1, grid=(ne, nc),
            in_specs=[pl.BlockSpec((rows, 1), lambda e, c, idx: (e * nc + c, 0)), vec, vec, vec,
                      pl.BlockSpec((None, d, 2 * ff), lambda e, c, idx: (e, 0, 0)),
                      pl.BlockSpec((None, ff, d), lambda e, c, idx: (e, 0, 0)),
                      pl.BlockSpec(memory_space=pl.ANY)],
            out_specs=pl.BlockSpec((rows, d), lambda e, c, idx: (e * nc + c, 0)),
            scratch_shapes=[pltpu.VMEM((2, rows, d), F32), pltpu.SemaphoreType.DMA((2,))]),
        compiler_params=_params(2, 56),
        name="moe_ffn",
    )(idx_flat, gate_col, g.reshape(1, d), shift, scale, wgu, wd, xr)


def _moe_combine_kernel(idx_ref, y_ref, gate_ref, x_hbm, o_hbm, abuf, gsem, ssem, *, rows, n_steps):
    del x_hbm
    c = pl.program_id(1)
    nc = pl.num_programs(1)
    step = pl.program_id(0) * nc + c
    slot = step % 2

    def scatter_copy(s, sl, r):
        return pltpu.make_async_copy(abuf.at[sl, pl.ds(r, 1), :], o_hbm.at[pl.ds(idx_ref[s * rows + r], 1), :],
                                     ssem.at[sl])

    def wait_scatter(sl):
        for _ in range(rows):
            pltpu.make_async_copy(abuf.at[sl, pl.ds(0, 1), :], o_hbm.at[pl.ds(0, 1), :], ssem.at[sl]).wait()

    @pl.when(step == 0)
    def _():
        _start_gather(idx_ref, o_hbm, abuf, gsem, 0, 0, rows)

    _wait_rows(o_hbm, abuf, gsem, slot, rows)
    abuf[slot] = abuf[slot] + gate_ref[...] * y_ref[...]

    def issue(r, carry):
        scatter_copy(step, slot, r).start()
        return carry
    lax.fori_loop(0, rows, issue, 0, unroll=8)

    @pl.when(c > 0)
    def _():
        wait_scatter(1 - slot)

    @pl.when(c == nc - 1)
    def _():
        wait_scatter(slot)

    @pl.when(step + 1 < n_steps)
    def _():
        _start_gather(idx_ref, o_hbm, abuf, gsem, step + 1, 1 - slot, rows)


def _moe_combine(idx_flat, y, gate_vec, xr, *, n_experts, rows):
    r, d = xr.shape
    slots = idx_flat.shape[0]
    nc = slots // n_experts // rows
    return pl.pallas_call(
        functools.partial(_moe_combine_kernel, rows=rows, n_steps=n_experts * nc),
        out_shape=jax.ShapeDtypeStruct((r, d), F32),
        grid_spec=pltpu.PrefetchScalarGridSpec(
            num_scalar_prefetch=1, grid=(n_experts, nc),
            in_specs=[pl.BlockSpec((rows, d), lambda e, c, idx: (e * nc + c, 0)),
                      pl.BlockSpec((1, d), lambda e, c, idx: (0, 0)),
                      pl.BlockSpec(memory_space=pl.ANY)],
            out_specs=pl.BlockSpec(memory_space=pl.ANY),
            scratch_shapes=[pltpu.VMEM((2, rows, d), F32), pltpu.SemaphoreType.DMA((2,)),
                            pltpu.SemaphoreType.DMA((2,))]),
        input_output_aliases={3: 0},
        compiler_params=_params(2, 48),
        name="moe_combine",
    )(idx_flat, y, gate_vec, xr)


def _ec_moe(xr, g, shift, scale, out_gate, w_router_t, wgu, wd):
    r = xr.shape[0]
    ne = w_router_t.shape[0]
    cap = EC_CAPACITY * r // ne
    rows = _pick(cap, 256, SUBLANES)
    aff_t = _router(xr, g, shift, scale, w_router_t)
    idx, gate = _select(aff_t, cap)
    idx_flat = idx.reshape(-1)
    y = _moe_ffn(idx_flat, gate.reshape(-1, 1), xr, g, shift, scale, wgu, wd, rows=rows)
    return _moe_combine(idx_flat, y, out_gate, xr, n_experts=ne, rows=rows)


def _rope_tables(n, dim):
    half = dim // 2
    inv = ROPE_BASE ** (-jnp.arange(0, half, 2, dtype=F32) / half)
    rows = jnp.repeat(jnp.arange(n // GRID_W, dtype=F32), GRID_W)
    cols = (jnp.arange(n) % GRID_W).astype(F32)
    ar = rows[:, None] * inv[None, :]
    ac = cols[:, None] * inv[None, :]
    cos = jnp.concatenate([jnp.cos(ar), jnp.cos(ar), jnp.cos(ac), jnp.cos(ac)], axis=1)
    sin = jnp.concatenate([-jnp.sin(ar), jnp.sin(ar), -jnp.sin(ac), jnp.sin(ac)], axis=1)
    if dim < LANES:
        cos = jnp.concatenate([cos, jnp.ones((n, LANES - dim), F32)], axis=1)
        sin = jnp.concatenate([sin, jnp.zeros((n, LANES - dim), F32)], axis=1)
    return cos, sin


def _even_mixer(hx, hy, x, y, gate_x, gate_y, w_in, sink, dw, ln_g, ln_b, w_out, tables, need_ctx):
    ch = dw.shape[1]
    aq = w_out.shape[0] - ch
    akv = (w_in.shape[1] - aq - 2 * ch) // 2
    cos, sin = tables
    scale = HEAD_DIM ** -0.5
    pair = HEAD_DIM // 4
    wq = w_in[:, :aq].astype(BF16)
    wk = w_in[:, aq:aq + akv].astype(BF16)
    w_agv = jnp.concatenate([w_in[:, aq + 2 * akv:], w_in[:, aq + akv:aq + 2 * akv]], axis=1).astype(BF16)
    w_o1 = w_out[:aq].astype(BF16)
    w_o2 = w_out[aq:].astype(BF16)
    v_col0 = 2 * ch // HEAD_DIM

    qx = _matmul([hx], [wq], out_dtype=BF16, epilogue="rope", extras=(cos, sin), scale=scale, pair=pair)
    kx = _matmul([hx], [wk], out_dtype=BF16, epilogue="rope", extras=(cos, sin), pair=pair)
    agv_x = _matmul([hx], [w_agv], out_dtype=BF16)
    ky = _matmul([hy], [wk], out_dtype=BF16)
    agv_y = _matmul([hy], [w_agv], out_dtype=BF16)

    att_x = _gqa(qx, kx, agv_x, v_col0, ky, agv_y, sink, local=True)
    conv_x = _conformer_conv(agv_x, dw, ln_g, ln_b)
    x = _matmul([att_x, conv_x], [w_o1, w_o2], out_dtype=F32, epilogue="resid", extras=(x, gate_x), bn=512)
    if need_ctx:
        qy = _matmul([hy], [wq], out_dtype=BF16, scale=scale)
        att_y = _gqa(qy, None, None, v_col0, ky, agv_y, sink, local=False)
        conv_y = _conformer_conv(agv_y, dw, ln_g, ln_b)
        y = _matmul([att_y, conv_y], [w_o1, w_o2], out_dtype=F32, epilogue="resid", extras=(y, gate_y), bn=512)
    return x, y


def _mla_mixer(hx, hy, x, y, gate_x, gate_y, w_dn, q_g, kv_g, w_uq, w_ukv, w_o, tables, need_ctx):
    n, d = hx.shape
    lc = hy.shape[0]
    q_rank, kv_rank = q_g.shape[0], kv_g.shape[0]
    n_heads = w_o.shape[0] // HEAD_DIM
    rope_dim = w_dn.shape[1] - q_rank - kv_rank
    assert n % lc == 0 and rope_dim <= LANES
    cos, sin = tables
    pair = rope_dim // 4
    scale = (HEAD_DIM + rope_dim) ** -0.5 * LOG2_E
    w_dn_p = jnp.concatenate([w_dn, jnp.zeros((d, LANES - rope_dim), F32)], axis=1).astype(BF16)
    w_uq3 = w_uq.reshape(q_rank, n_heads, HEAD_DIM + rope_dim)
    w_qn = w_uq3[:, :, :HEAD_DIM].reshape(q_rank, n_heads * HEAD_DIM).astype(BF16)
    w_qr = jnp.concatenate([w_uq3[:, :, HEAD_DIM:], jnp.zeros((q_rank, n_heads, LANES - rope_dim), F32)],
                           axis=2).reshape(q_rank, n_heads * LANES).astype(BF16)
    w_kv3 = w_ukv.reshape(kv_rank, n_heads, 2 * HEAD_DIM)
    w_kv = jnp.concatenate([w_kv3[:, :, :HEAD_DIM].reshape(kv_rank, -1), w_kv3[:, :, HEAD_DIM:].reshape(kv_rank, -1)],
                           axis=1).astype(BF16)
    w_ob = w_o.astype(BF16)

    qlat_x, ckv_x, kr_x = _mla_down(hx, w_dn_p, q_g, kv_g, cos, sin, pair=pair, use_rope=True)
    qlat_y, ckv_y, kr_y = _mla_down(hy, w_dn_p, q_g, kv_g, cos[:lc], sin[:lc], pair=pair, use_rope=False)
    kv_all = _matmul([jnp.concatenate([ckv_x, ckv_y], axis=0)], [w_kv], out_dtype=BF16, bm=1280)
    kr_all = jnp.concatenate([kr_x, kr_y], axis=0)

    qn_x = _matmul([qlat_x], [w_qn], out_dtype=BF16, scale=scale)
    qr_x = _matmul([qlat_x], [w_qr], out_dtype=BF16, epilogue="rope", extras=(cos, sin), scale=scale, pair=pair)
    o_x = _mla_attn(qn_x, qr_x, kv_all, kr_all, n_heads=n_heads, key_rows=n + lc, key_block=0)
    x = _matmul([o_x], [w_ob], out_dtype=F32, epilogue="resid", extras=(x, gate_x), bn=512)
    if need_ctx:
        qn_y = _matmul([qlat_y], [w_qn], out_dtype=BF16, scale=scale)
        qr_y = _matmul([qlat_y], [w_qr], out_dtype=BF16, scale=scale)
        o_y = _mla_attn(qn_y, qr_y, kv_all, kr_all, n_heads=n_heads, key_rows=lc, key_block=n // lc)
        y = _matmul([o_y], [w_ob], out_dtype=F32, epilogue="resid", extras=(y, gate_y), bn=512)
    return x, y


def kernel(x, c, ctx, c_ctx, ada_w, ada_b, norm1_g, norm2_g, ev_w_in, ev_sink, ev_dw, ev_ln_g, ev_ln_b, ev_w_out, od_w_dn, od_q_norm_g, od_kv_norm_g, od_w_uq, od_w_ukv, od_w_o, moe_router, moe_w_gate, moe_w_up, moe_w_down, final_g):
    b, n, d = x.shape
    assert b == 1 and n % GRID_W == 0
    depth = ada_w.shape[0]
    xr, yr = x[0], ctx[0]
    mods = _adaln(c, c_ctx, ada_w, ada_b)
    tables_a = _rope_tables(n, HEAD_DIM)
    rope_dim = od_w_dn.shape[2] - od_q_norm_g.shape[1] - od_kv_norm_g.shape[1]
    tables_c = _rope_tables(n, rope_dim)
    for l in range(depth):
        need_ctx = l < depth - 1
        mx = [mods[l, 0:1, i * d:(i + 1) * d] for i in range(6)]
        my = [mods[l, 1:2, i * d:(i + 1) * d] for i in range(6)]
        hx = _normmod(xr, norm1_g[l], mx[0], mx[1], BF16)
        hy = _normmod(yr, norm1_g[l], my[0], my[1], BF16)
        i = l // 2
        if l % 2 == 0:
            xr, yr = _even_mixer(hx, hy, xr, yr, mx[2], my[2], ev_w_in[i], ev_sink[i], ev_dw[i], ev_ln_g[i],
                                 ev_ln_b[i], ev_w_out[i], tables_a, need_ctx)
        else:
            xr, yr = _mla_mixer(hx, hy, xr, yr, mx[2], my[2], od_w_dn[i], od_q_norm_g[i], od_kv_norm_g[i],
                                od_w_uq[i], od_w_ukv[i], od_w_o[i], tables_c, need_ctx)
        w_rt = moe_router[l].T
        wgu = jnp.concatenate([moe_w_gate[l], moe_w_up[l]], axis=2).astype(BF16)
        wd = moe_w_down[l].astype(BF16)
        xr = _ec_moe(xr, norm2_g[l], mx[3], mx[4], mx[5], w_rt, wgu, wd)
        if need_ctx:
            yr = _ec_moe(yr, norm2_g[l], my[3], my[4], my[5], w_rt, wgu, wd)
    zero = jnp.zeros((1, d), F32)
    return _normmod(xr, final_g, zero, zero, F32)[None]
```

```python
import functools

import jax
import jax.numpy as jnp
from jax import lax
from jax.experimental import pallas as pl
from jax.experimental.pallas import tpu as pltpu

F32 = jnp.float32
BF16 = jnp.bfloat16

GRID_W = 64
ROPE_BASE = 10000.0
NORM_EPS = 1e-6
NEG_INF = -1e30
HEAD_DIM = 128
A_WINDOW = 128
EC_CAPACITY = 2
LOG2_E = 1.4426950408889634

LANES = 128
SUBLANES = 8
BF16_ROWS = 16
MIB = 1024 * 1024


def _params(n_axes, vmem_mib):
    return pltpu.CompilerParams(dimension_semantics=("arbitrary",) * n_axes, vmem_limit_bytes=vmem_mib * MIB)


def _pick(total, target, mult):
    if total <= target:
        return total
    best = None
    for d in range(mult, target + 1, mult):
        if total % d == 0:
            best = d
    assert best is not None, (total, target, mult)
    return best


def _sigmoid(v):
    return 1.0 / (1.0 + jnp.exp(-v))


def _norm_mod(xv, g, shift, scale):
    yv = xv * lax.rsqrt(jnp.mean(xv * xv, axis=-1, keepdims=True) + NORM_EPS)
    return (yv * g) * (1.0 + scale) + shift


def _rope(xv, cos, sin, pair):
    lane = lax.broadcasted_iota(jnp.int32, xv.shape, 1)
    first = (lane % (2 * pair)) < pair
    partner = jnp.where(first, pltpu.roll(xv, LANES - pair, 1), pltpu.roll(xv, pair, 1))
    return xv * cos + partner * sin


def _adaln_kernel(cc_ref, w_ref, b_ref, o_ref):
    cc = cc_ref[...]
    s = cc * _sigmoid(cc)
    w = w_ref[...]
    b = b_ref[...]
    r0 = jnp.sum(s[:, 0:1] * w, axis=0, keepdims=True) + b
    r1 = jnp.sum(s[:, 1:2] * w, axis=0, keepdims=True) + b
    o_ref[...] = jnp.concatenate([r0, r1, jnp.zeros((SUBLANES - 2, w.shape[1]), F32)], axis=0)


def _adaln(c, c_ctx, ada_w, ada_b):
    depth, d, n6 = ada_w.shape
    cc = jnp.zeros((d, SUBLANES), F32).at[:, 0].set(c[0]).at[:, 1].set(c_ctx)
    tn = _pick(n6, 512, LANES)
    return pl.pallas_call(
        _adaln_kernel,
        out_shape=jax.ShapeDtypeStruct((depth, SUBLANES, n6), F32),
        grid=(depth, n6 // tn),
        in_specs=[pl.BlockSpec((d, SUBLANES), lambda l, j: (0, 0)),
                  pl.BlockSpec((None, d, tn), lambda l, j: (l, 0, j)),
                  pl.BlockSpec((None, 1, tn), lambda l, j: (l, 0, j))],
        out_specs=pl.BlockSpec((None, SUBLANES, tn), lambda l, j: (l, 0, j)),
        compiler_params=_params(2, 48),
        name="adaln",
    )(cc, ada_w, ada_b.reshape(depth, 1, n6))


def _normmod_kernel(x_ref, g_ref, sh_ref, sc_ref, o_ref):
    o_ref[...] = _norm_mod(x_ref[...], g_ref[...], sh_ref[...], sc_ref[...]).astype(o_ref.dtype)


def _normmod(xr, g, shift, scale, out_dtype):
    r, d = xr.shape
    tr = _pick(r, 512, BF16_ROWS)
    vec = pl.BlockSpec((1, d), lambda i: (0, 0))
    return pl.pallas_call(
        _normmod_kernel,
        out_shape=jax.ShapeDtypeStruct((r, d), out_dtype),
        grid=(r // tr,),
        in_specs=[pl.BlockSpec((tr, d), lambda i: (i, 0)), vec, vec, vec],
        out_specs=pl.BlockSpec((tr, d), lambda i: (i, 0)),
        compiler_params=_params(1, 48),
        name="normmod",
    )(xr, g.reshape(1, d), shift, scale)


def _mm_kernel(*refs, n_pairs, epilogue, scale, pair):
    a_refs, w_refs, rest = refs[:n_pairs], refs[n_pairs:2 * n_pairs], refs[2 * n_pairs:]
    acc = None
    for a_ref, w_ref in zip(a_refs, w_refs):
        part = jnp.dot(a_ref[...], w_ref[...], preferred_element_type=F32)
        acc = part if acc is None else acc + part
    if epilogue == "plain":
        (o_ref,) = rest
        o_ref[...] = (acc * scale if scale != 1.0 else acc).astype(o_ref.dtype)
    elif epilogue == "rope":
        cos_ref, sin_ref, o_ref = rest
        cos, sin = cos_ref[...], sin_ref[...]
        for h in range(acc.shape[1] // LANES):
            sl = slice(h * LANES, (h + 1) * LANES)
            r = _rope(acc[:, sl], cos, sin, pair)
            o_ref[:, sl] = (r * scale if scale != 1.0 else r).astype(o_ref.dtype)
    else:
        x_ref, gate_ref, o_ref = rest
        o_ref[...] = x_ref[...] + gate_ref[...] * acc


def _matmul(a_list, w_list, *, out_dtype, epilogue="plain", extras=(), scale=1.0, pair=0, bm=1024, bn=1024):
    m = a_list[0].shape[0]
    n = w_list[0].shape[1]
    bm = _pick(m, bm, BF16_ROWS)
    bn = _pick(n, bn, LANES)
    in_specs = [pl.BlockSpec((bm, a.shape[1]), lambda j, i: (i, 0)) for a in a_list]
    in_specs += [pl.BlockSpec((w.shape[0], bn), lambda j, i: (0, j)) for w in w_list]
    if epilogue == "rope":
        in_specs += [pl.BlockSpec((bm, LANES), lambda j, i: (i, 0))] * 2
    elif epilogue == "resid":
        in_specs += [pl.BlockSpec((bm, bn), lambda j, i: (i, j)), pl.BlockSpec((1, bn), lambda j, i: (0, j))]
    return pl.pallas_call(
        functools.partial(_mm_kernel, n_pairs=len(a_list), epilogue=epilogue, scale=scale, pair=pair),
        out_shape=jax.ShapeDtypeStruct((m, n), out_dtype),
        grid=(n // bn, m // bm),
        in_specs=in_specs,
        out_specs=pl.BlockSpec((bm, bn), lambda j, i: (i, j)),
        compiler_params=_params(2, 56),
        name="mm_" + epilogue,
    )(*a_list, *w_list, *extras)


def _gqa_kernel(sink_ref, *refs, groups, tq, local, n_tokens):
    if local:
        q_ref, k_ref, v_ref, ky_ref, vy_ref, o_ref = refs
    else:
        q_ref, ky_ref, vy_ref, o_ref = refs
    hk = pl.program_id(0)
    nt = (((1,), (1,)), ((), ()))
    def with_ones(v):
        return jnp.concatenate([v, jnp.ones(v.shape, BF16)], axis=1)

    ky = ky_ref[...]
    vy = with_ones(vy_ref[...])
    if local:
        n = pl.program_id(1)
        win = tq + 2 * A_WINDOW
        start = pl.multiple_of(jnp.clip(n * tq - A_WINDOW, 0, n_tokens - win), BF16_ROWS)
        kw = k_ref[pl.ds(start, win), :]
        vw = with_ones(v_ref[pl.ds(start, win), :])
        qpos = n * tq + lax.broadcasted_iota(jnp.int32, (tq, win), 0)
        kpos = start + lax.broadcasted_iota(jnp.int32, (tq, win), 1)
        band = jnp.abs(kpos - qpos) <= A_WINDOW
    for gi in range(groups):
        sl = slice(gi * HEAD_DIM, (gi + 1) * HEAD_DIM)
        qg = q_ref[:, sl]
        sink = sink_ref[hk * groups + gi] * LOG2_E
        s_ctx = lax.dot_general(qg, ky, nt, preferred_element_type=F32)
        m = jnp.maximum(jnp.max(s_ctx, axis=1, keepdims=True), sink)
        if local:
            s_loc = jnp.where(band, lax.dot_general(qg, kw, nt, preferred_element_type=F32), NEG_INF)
            m = jnp.maximum(m, jnp.max(s_loc, axis=1, keepdims=True))
        acc = jnp.dot(jnp.exp2(s_ctx - m).astype(BF16), vy, preferred_element_type=F32)
        if local:
            acc = acc + jnp.dot(jnp.exp2(s_loc - m).astype(BF16), vw, preferred_element_type=F32)
        den = acc[:, HEAD_DIM:] + jnp.exp2(sink - m)
        o_ref[:, sl] = (acc[:, :HEAD_DIM] / den).astype(o_ref.dtype)


def _gqa(q, kx, vsrc_x, v_col0, ky, vsrc_y, sink, *, local):
    r, aq = q.shape
    hkv = ky.shape[1] // HEAD_DIM
    groups = aq // HEAD_DIM // hkv
    lc = ky.shape[0]
    gw = groups * HEAD_DIM
    tq = _pick(r, 256, BF16_ROWS)
    n_tokens = kx.shape[0] if local else 0
    if local:
        assert n_tokens >= tq + 2 * A_WINDOW
    q_spec = pl.BlockSpec((tq, gw), lambda h, i, s: (i, h))
    ctx_specs = [pl.BlockSpec((lc, HEAD_DIM), lambda h, i, s: (0, h)),
                 pl.BlockSpec((lc, HEAD_DIM), lambda h, i, s: (0, v_col0 + h))]
    if local:
        in_specs = [q_spec,
                    pl.BlockSpec((n_tokens, HEAD_DIM), lambda h, i, s: (0, h)),
                    pl.BlockSpec((n_tokens, HEAD_DIM), lambda h, i, s: (0, v_col0 + h))] + ctx_specs
        operands = (q, kx, vsrc_x, ky, vsrc_y)
    else:
        in_specs = [q_spec] + ctx_specs
        operands = (q, ky, vsrc_y)
    return pl.pallas_call(
        functools.partial(_gqa_kernel, groups=groups, tq=tq, local=local, n_tokens=n_tokens),
        out_shape=jax.ShapeDtypeStruct((r, aq), BF16),
        grid_spec=pltpu.PrefetchScalarGridSpec(
            num_scalar_prefetch=1, grid=(hkv, r // tq), in_specs=in_specs,
            out_specs=pl.BlockSpec((tq, gw), lambda h, i, s: (i, h))),
        compiler_params=_params(2, 48),
        name="gqa_local" if local else "gqa_ctx",
    )(sink, *operands)


CONV_HALO = 16
CONV_ROWS = 32
CONV_LANES = 256


def _conv_kernel(a_ref, g_ref, ap_ref, gp_ref, an_ref, gn_ref, dw_ref, lg_ref, lb_ref, o_ref, u_ref, z_ref,
                 *, taps, n_blocks):
    i = pl.program_id(0)
    tr, ch = z_ref.shape
    pad = (taps - 1) // 2

    def glu(a, g):
        return a.astype(F32) * _sigmoid(g.astype(F32))

    u_ref[0, 0:CONV_HALO, :] = glu(ap_ref[...], gp_ref[...]) * (i > 0).astype(F32)
    u_ref[0, CONV_HALO:CONV_HALO + tr, :] = glu(a_ref[...], g_ref[...])
    u_ref[0, CONV_HALO + tr:, :] = glu(an_ref[...], gn_ref[...]) * (i < n_blocks - 1).astype(F32)
    span = u_ref.shape[1] - SUBLANES
    for s in range(1, SUBLANES):
        u_ref[s, 0:span, :] = u_ref[0, s:s + span, :]

    def lane_chunk(cc, carry):
        c0 = pl.multiple_of(cc * CONV_LANES, CONV_LANES)
        for r0 in range(0, tr, CONV_ROWS):
            acc = jnp.zeros((CONV_ROWS, CONV_LANES), F32)
            for j in range(taps):
                shift = j - pad + CONV_HALO
                acc = acc + (u_ref[shift % SUBLANES, pl.ds(r0 + shift - shift % SUBLANES, CONV_ROWS),
                                   pl.ds(c0, CONV_LANES)]
                             * dw_ref[pl.ds(j, 1), pl.ds(c0, CONV_LANES)])
            z_ref[pl.ds(r0, CONV_ROWS), pl.ds(c0, CONV_LANES)] = acc
        return carry

    lax.fori_loop(0, ch // CONV_LANES, lane_chunk, 0)
    z = z_ref[...]
    mu = jnp.mean(z, axis=-1, keepdims=True)
    zc = z - mu
    var = jnp.mean(zc * zc, axis=-1, keepdims=True)
    yv = zc * lax.rsqrt(var + NORM_EPS) * lg_ref[...] + lb_ref[...]
    o_ref[...] = (yv * _sigmoid(yv)).astype(o_ref.dtype)


def _conformer_conv(agv, dw, ln_g, ln_b):
    r = agv.shape[0]
    taps, ch = dw.shape
    assert (taps - 1) // 2 < CONV_HALO and ch % CONV_LANES == 0
    tr = _pick(r, 256, CONV_ROWS)
    nb = r // tr
    hb = tr // CONV_HALO
    last = r // CONV_HALO - 1
    dwp = jnp.zeros((2 * CONV_HALO, ch), F32).at[:taps].set(dw)
    cur = lambda col: pl.BlockSpec((tr, ch), lambda i: (i, col))
    prv = lambda col: pl.BlockSpec((CONV_HALO, ch), lambda i: (jnp.maximum(i * hb - 1, 0), col))
    nxt = lambda col: pl.BlockSpec((CONV_HALO, ch), lambda i: (jnp.minimum((i + 1) * hb, last), col))
    vec = pl.BlockSpec((1, ch), lambda i: (0, 0))
    return pl.pallas_call(
        functools.partial(_conv_kernel, taps=taps, n_blocks=nb),
        out_shape=jax.ShapeDtypeStruct((r, ch), BF16),
        grid=(nb,),
        in_specs=[cur(0), cur(1), prv(0), prv(1), nxt(0), nxt(1),
                  pl.BlockSpec((2 * CONV_HALO, ch), lambda i: (0, 0)), vec, vec],
        out_specs=pl.BlockSpec((tr, ch), lambda i: (i, 0)),
        scratch_shapes=[pltpu.VMEM((SUBLANES, tr + 2 * CONV_HALO, ch), F32), pltpu.VMEM((tr, ch), F32)],
        compiler_params=_params(1, 48),
        name="conformer_conv",
    )(agv, agv, agv, agv, agv, agv, dwp, ln_g.reshape(1, ch), ln_b.reshape(1, ch))


def _mla_down_kernel(h_ref, w_ref, qg_ref, kg_ref, cos_ref, sin_ref, q_ref, kv_ref, kr_ref, *, q_rank, kv_rank,
                     pair, use_rope):
    acc = jnp.dot(h_ref[...], w_ref[...], preferred_element_type=F32)

    def rms(v, g):
        return v * lax.rsqrt(jnp.mean(v * v, axis=-1, keepdims=True) + NORM_EPS) * g

    q_ref[...] = rms(acc[:, :q_rank], qg_ref[...]).astype(q_ref.dtype)
    kv_ref[...] = rms(acc[:, q_rank:q_rank + kv_rank], kg_ref[...]).astype(kv_ref.dtype)
    kr = acc[:, q_rank + kv_rank:]
    if use_rope:
        kr = _rope(kr, cos_ref[...], sin_ref[...], pair)
    kr_ref[...] = kr.astype(kr_ref.dtype)


def _mla_down(h, w_dn_p, q_g, kv_g, cos, sin, *, pair, use_rope):
    m, d = h.shape
    q_rank, kv_rank = q_g.shape[0], kv_g.shape[0]
    wn = w_dn_p.shape[1]
    assert wn == q_rank + kv_rank + LANES
    bm = _pick(m, 512, BF16_ROWS)
    row = lambda w: pl.BlockSpec((bm, w), lambda i: (i, 0))
    return pl.pallas_call(
        functools.partial(_mla_down_kernel, q_rank=q_rank, kv_rank=kv_rank, pair=pair, use_rope=use_rope),
        out_shape=(jax.ShapeDtypeStruct((m, q_rank), BF16), jax.ShapeDtypeStruct((m, kv_rank), BF16),
                   jax.ShapeDtypeStruct((m, LANES), BF16)),
        grid=(m // bm,),
        in_specs=[row(d), pl.BlockSpec((d, wn), lambda i: (0, 0)),
                  pl.BlockSpec((1, q_rank), lambda i: (0, 0)), pl.BlockSpec((1, kv_rank), lambda i: (0, 0)),
                  row(LANES), row(LANES)],
        out_specs=(row(q_rank), row(kv_rank), row(LANES)),
        compiler_params=_params(1, 56),
        name="mla_down",
    )(h, w_dn_p, q_g.reshape(1, q_rank), kv_g.reshape(1, kv_rank), cos, sin)


MLA_TQ = 1024
MLA_SUB = 256
MLA_KEY_CHUNK = 1280
MLA_STAGES = 12


def _mla_attn_kernel(qn_ref, qr_ref, kn_ref, kr_ref, v_ref, o_ref, kcat_ref, vcat_ref, qcat_ref, s_ref, *, tk, sub):
    nk = kcat_ref.shape[0]
    tq = qn_ref.shape[0]
    n_sub = tq // sub

    @pl.when(pl.program_id(1) == 0)
    def _():
        kcat_ref[:, :HEAD_DIM] = kn_ref[...]
        kcat_ref[:, HEAD_DIM:] = kr_ref[...]
        vcat_ref[:, :HEAD_DIM] = v_ref[...]
        vcat_ref[:, HEAD_DIM:] = jnp.ones((nk, HEAD_DIM), BF16)

    qcat_ref[:, :HEAD_DIM] = qn_ref[...]
    qcat_ref[:, HEAD_DIM:] = qr_ref[...]
    nt = (((1,), (1,)), ((), ()))
    n_chunks = nk // tk

    def scores(buf, u, off):
        s_ref[buf, u] = lax.dot_general(qcat_ref[u * sub:(u + 1) * sub, :], kcat_ref[pl.ds(off, tk), :], nt,
                                        preferred_element_type=F32)

    def absorb(buf, u, off, m, acc):
        s = s_ref[buf, u]
        m_new = jnp.maximum(m, jnp.max(s, axis=1, keepdims=True))
        p = jnp.exp2(s - m_new).astype(BF16)
        acc = jnp.exp2(m - m_new) * acc + jnp.dot(p, vcat_ref[pl.ds(off, tk), :], preferred_element_type=F32)
        return m_new, acc

    def stage(buf, off, carry):
        out = []
        for u in range(n_sub):
            out.append(absorb(buf, u, off, *carry[u]))
            scores(1 - buf, u, off + tk)
        return tuple(out)

    def trip(j, carry):
        off = pl.multiple_of(j * (MLA_STAGES * tk), tk)
        for k in range(MLA_STAGES):
            carry = stage(k % 2, off + k * tk, carry)
        return carry

    for u in range(n_sub):
        scores(0, u, 0)
    carry = tuple((jnp.full((sub, 1), NEG_INF, F32), jnp.zeros((sub, 2 * HEAD_DIM), F32)) for _ in range(n_sub))
    n_stages = n_chunks - 1
    carry = lax.fori_loop(0, n_stages // MLA_STAGES, trip, carry)
    for k in range(n_stages - n_stages % MLA_STAGES, n_stages):
        carry = stage(k % 2, k * tk, carry)
    for u in range(n_sub):
        _, acc = absorb(n_stages % 2, u, n_stages * tk, *carry[u])
        o_ref[u * sub:(u + 1) * sub, :] = (acc[:, :HEAD_DIM] / acc[:, HEAD_DIM:]).astype(o_ref.dtype)


def _mla_attn(qn, qr, kv, kr, *, n_heads, key_rows, key_block):
    r = qn.shape[0]
    tq = _pick(r, MLA_TQ, MLA_SUB) if r >= MLA_SUB else r
    sub = min(MLA_SUB, tq)
    tk = _pick(key_rows, MLA_KEY_CHUNK, LANES)
    once = pl.Buffered(1)
    kspec = lambda col0: pl.BlockSpec((key_rows, HEAD_DIM), lambda h, i: (key_block, col0 + h), pipeline_mode=once)
    qspec = pl.BlockSpec((tq, HEAD_DIM), lambda h, i: (i, h))
    return pl.pallas_call(
        functools.partial(_mla_attn_kernel, tk=tk, sub=sub),
        out_shape=jax.ShapeDtypeStruct((r, n_heads * HEAD_DIM), BF16),
        grid=(n_heads, r // tq),
        in_specs=[qspec, qspec, kspec(0),
                  pl.BlockSpec((key_rows, LANES), lambda h, i: (key_block, 0), pipeline_mode=once), kspec(n_heads)],
        out_specs=pl.BlockSpec((tq, HEAD_DIM), lambda h, i: (i, h)),
        scratch_shapes=[pltpu.VMEM((key_rows, 2 * HEAD_DIM), BF16), pltpu.VMEM((key_rows, 2 * HEAD_DIM), BF16),
                        pltpu.VMEM((tq, 2 * HEAD_DIM), BF16), pltpu.VMEM((2, tq // sub, sub, tk), F32)],
        compiler_params=_params(2, 60),
        name="mla_attn",
    )(qn, qr, kv, kr, kv)


def _router_kernel(x_ref, g_ref, sh_ref, sc_ref, wt_ref, aff_ref):
    h = _norm_mod(x_ref[...], g_ref[...], sh_ref[...], sc_ref[...])
    w = wt_ref[...]
    ne = w.shape[0]
    h_hi = h.astype(BF16)
    h_lo = (h - h_hi.astype(F32)).astype(BF16)
    w_hi = w.astype(BF16)
    w_lo = (w - w_hi.astype(F32)).astype(BF16)
    nt = (((1,), (1,)), ((), ()))
    both = lax.dot_general(jnp.concatenate([w_hi, w_lo], axis=0), h_hi, nt, preferred_element_type=F32)
    logits = both[:ne] + both[ne:] + lax.dot_general(w_hi, h_lo, nt, preferred_element_type=F32)
    e = jnp.exp(logits - jnp.max(logits, axis=0, keepdims=True))
    aff_ref[...] = e / jnp.sum(e, axis=0, keepdims=True)


def _router(xr, g, shift, scale, w_router_t):
    r, d = xr.shape
    ne = w_router_t.shape[0]
    tr = _pick(r, 512, LANES)
    vec = pl.BlockSpec((1, d), lambda i: (0, 0))
    return pl.pallas_call(
        _router_kernel,
        out_shape=jax.ShapeDtypeStruct((ne, r), F32),
        grid=(r // tr,),
        in_specs=[pl.BlockSpec((tr, d), lambda i: (i, 0)), vec, vec, vec, pl.BlockSpec((ne, d), lambda i: (0, 0))],
        out_specs=pl.BlockSpec((ne, tr), lambda i: (0, i)),
        compiler_params=_params(1, 48),
        name="router",
    )(xr, g.reshape(1, d), shift, scale, w_router_t)


def _lane_cumsum(v):
    lane = lax.broadcasted_iota(jnp.int32, v.shape, 1)
    k = 1
    while k < LANES:
        v = v + jnp.where(lane >= k, pltpu.roll(v, k, 1), 0)
        k *= 2
    return v


def _select_kernel(a_ref, key_ref, *, cap, n_pad):
    a = a_ref[...]
    ne, g, _ = a.shape
    bits = lax.bitcast_convert_type(a, jnp.int32)

    def count(mask):
        return jnp.sum(jnp.sum(mask.astype(F32), axis=2, keepdims=True), axis=1, keepdims=True)

    thr = jnp.zeros((ne, 1, 1), jnp.int32)
    for b in range(30, -1, -1):
        cand = thr | (1 << b)
        thr = jnp.where(count(bits >= cand) >= cap, cand, thr)
    above = bits > thr
    tie = bits == thr
    need = cap - count(above)
    tie_i = tie.astype(jnp.int32)
    incl = _lane_cumsum(tie_i.reshape(ne * g, LANES)).reshape(ne, g, LANES)
    lower = (lax.broadcasted_iota(jnp.int32, (g, g), 1) < lax.broadcasted_iota(jnp.int32, (g, g), 0)).astype(F32)
    tok = lax.broadcasted_iota(jnp.int32, (g, LANES), 0) * LANES + lax.broadcasted_iota(jnp.int32, (g, LANES), 1)
    for e in range(ne):
        row_tot = jnp.broadcast_to(incl[e, :, LANES - 1:LANES].astype(F32), (g, LANES))
        row_off = jnp.dot(lower, row_tot, precision=lax.Precision.HIGHEST, preferred_element_type=F32)
        rank = (incl[e] - tie_i[e]).astype(F32) + row_off
        sel = above[e] | (tie[e] & (rank < need[e]))
        key_ref[e] = jnp.where(sel, tok, n_pad)


def _select(aff_t, cap):
    ne, r = aff_t.shape
    assert r % LANES == 0
    n_pad = max(r, SUBLANES * LANES)
    g = n_pad // LANES
    a3 = jnp.pad(aff_t, ((0, 0), (0, n_pad - r))).reshape(ne, g, LANES)
    keys = pl.pallas_call(
        functools.partial(_select_kernel, cap=cap, n_pad=n_pad),
        out_shape=jax.ShapeDtypeStruct((ne, g, LANES), jnp.int32),
        grid=(1,),
        in_specs=[pl.BlockSpec((ne, g, LANES), lambda i: (0, 0, 0))],
        out_specs=pl.BlockSpec((ne, g, LANES), lambda i: (0, 0, 0)),
        compiler_params=_params(1, 32),
        name="ec_select",
    )(a3)
    idx = jnp.sort(keys.reshape(ne, n_pad), axis=1)[:, :cap]
    return idx, jnp.take_along_axis(aff_t, idx, axis=1)


def _row_copy(src_hbm, buf, sem, slot, tok, r):
    return pltpu.make_async_copy(src_hbm.at[pl.ds(tok, 1), :], buf.at[slot, pl.ds(r, 1), :], sem.at[slot])


def _start_gather(idx_ref, src_hbm, buf, sem, step, slot, rows):
    def issue(r, carry):
        _row_copy(src_hbm, buf, sem, slot, idx_ref[step * rows + r], r).start()
        return carry
    lax.fori_loop(0, rows, issue, 0, unroll=8)


def _wait_rows(src_hbm, buf, sem, slot, rows):
    for _ in range(rows):
        _row_copy(src_hbm, buf, sem, slot, 0, 0).wait()


def _moe_ffn_kernel(idx_ref, gate_ref, g_ref, sh_ref, sc_ref, wgu_ref, wd_ref, x_hbm, y_ref, xbuf, sem,
                    *, rows, n_steps):
    step = pl.program_id(0) * pl.num_programs(1) + pl.program_id(1)
    slot = step % 2

    @pl.when(step == 0)
    def _():
        _start_gather(idx_ref, x_hbm, xbuf, sem, 0, 0, rows)

    @pl.when(step + 1 < n_steps)
    def _():
        _start_gather(idx_ref, x_hbm, xbuf, sem, step + 1, 1 - slot, rows)

    _wait_rows(x_hbm, xbuf, sem, slot, rows)
    h = _norm_mod(xbuf[slot], g_ref[...], sh_ref[...], sc_ref[...]).astype(BF16)
    ff = wd_ref.shape[0]
    halves = []
    for hh in (h[:rows // 2], h[rows // 2:]):
        au = jnp.dot(hh, wgu_ref[...], preferred_element_type=F32)
        a, u = au[:, :ff], au[:, ff:]
        halves.append((a * _sigmoid(a) * u).astype(BF16))
    z = jnp.concatenate(halves, axis=0)
    y_ref[...] = jnp.dot(z, wd_ref[...], preferred_element_type=F32) * gate_ref[...]


def _moe_ffn(idx_flat, gate_col, xr, g, shift, scale, wgu, wd, layer, *, rows):
    r, d = xr.shape
    _, ne, ff, _ = wd.shape
    assert ff % LANES == 0
    slots = idx_flat.shape[0]
    nc = slots // ne // rows
    vec = pl.BlockSpec((1, d), lambda e, c, idx: (0, 0))
    return pl.pallas_call(
        functools.partial(_moe_ffn_kernel, rows=rows, n_steps=ne * nc),
        out_shape=jax.ShapeDtypeStruct((slots, d), F32),
        grid_spec=pltpu.PrefetchScalarGridSpec(
            num_scalar_prefetch=1, grid=(ne, nc),
            in_specs=[pl.BlockSpec((rows, 1), lambda e, c, idx: (e * nc + c, 0)), vec, vec, vec,
                      pl.BlockSpec((None, None, d, 2 * ff), lambda e, c, idx: (layer, e, 0, 0)),
                      pl.BlockSpec((None, None, ff, d), lambda e, c, idx: (layer, e, 0, 0)),
                      pl.BlockSpec(memory_space=pl.ANY)],
            out_specs=pl.BlockSpec((rows, d), lambda e, c, idx: (e * nc + c, 0)),
            scratch_shapes=[pltpu.VMEM((2, rows, d), F32), pltpu.SemaphoreType.DMA((2,))]),
        compiler_params=_params(2, 56),
        name="moe_ffn",
    )(idx_flat, gate_col, g.reshape(1, d), shift, scale, wgu, wd, xr)


def _moe_combine_kernel(idx_ref, y_ref, gate_ref, x_hbm, o_hbm, abuf, gsem, ssem, *, rows, n_steps):
    del x_hbm
    c = pl.program_id(1)
    nc = pl.num_programs(1)
    step = pl.program_id(0) * nc + c
    slot = step % 2

    def scatter_copy(s, sl, r):
        return pltpu.make_async_copy(abuf.at[sl, pl.ds(r, 1), :], o_hbm.at[pl.ds(idx_ref[s * rows + r], 1), :],
                                     ssem.at[sl])

    def wait_scatter(sl):
        for _ in range(rows):
            pltpu.make_async_copy(abuf.at[sl, pl.ds(0, 1), :], o_hbm.at[pl.ds(0, 1), :], ssem.at[sl]).wait()

    @pl.when(step == 0)
    def _():
        _start_gather(idx_ref, o_hbm, abuf, gsem, 0, 0, rows)

    _wait_rows(o_hbm, abuf, gsem, slot, rows)
    abuf[slot] = abuf[slot] + gate_ref[...] * y_ref[...]

    def issue(r, carry):
        scatter_copy(step, slot, r).start()
        return carry
    lax.fori_loop(0, rows, issue, 0, unroll=8)

    @pl.when(c > 0)
    def _():
        wait_scatter(1 - slot)

    @pl.when(c == nc - 1)
    def _():
        wait_scatter(slot)

    @pl.when(step + 1 < n_steps)
    def _():
        _start_gather(idx_ref, o_hbm, abuf, gsem, step + 1, 1 - slot, rows)


def _moe_combine(idx_flat, y, gate_vec, xr, *, n_experts, rows):
    r, d = xr.shape
    slots = idx_flat.shape[0]
    nc = slots // n_experts // rows
    return pl.pallas_call(
        functools.partial(_moe_combine_kernel, rows=rows, n_steps=n_experts * nc),
        out_shape=jax.ShapeDtypeStruct((r, d), F32),
        grid_spec=pltpu.PrefetchScalarGridSpec(
            num_scalar_prefetch=1, grid=(n_experts, nc),
            in_specs=[pl.BlockSpec((rows, d), lambda e, c, idx: (e * nc + c, 0)),
                      pl.BlockSpec((1, d), lambda e, c, idx: (0, 0)),
                      pl.BlockSpec(memory_space=pl.ANY)],
            out_specs=pl.BlockSpec(memory_space=pl.ANY),
            scratch_shapes=[pltpu.VMEM((2, rows, d), F32), pltpu.SemaphoreType.DMA((2,)),
                            pltpu.SemaphoreType.DMA((2,))]),
        input_output_aliases={3: 0},
        compiler_params=_params(2, 48),
        name="moe_combine",
    )(idx_flat, y, gate_vec, xr)


def _ec_moe(xr, g, shift, scale, out_gate, w_router_t, wgu, wd, layer):
    r = xr.shape[0]
    ne = w_router_t.shape[0]
    cap = EC_CAPACITY * r // ne
    rows = _pick(cap, 256, SUBLANES)
    aff_t = _router(xr, g, shift, scale, w_router_t)
    idx, gate = _select(aff_t, cap)
    idx_flat = idx.reshape(-1)
    y = _moe_ffn(idx_flat, gate.reshape(-1, 1), xr, g, shift, scale, wgu, wd, layer, rows=rows)
    return _moe_combine(idx_flat, y, out_gate, xr, n_experts=ne, rows=rows)


def _rope_tables(n, dim):
    half = dim // 2
    inv = ROPE_BASE ** (-jnp.arange(0, half, 2, dtype=F32) / half)
    rows = jnp.repeat(jnp.arange(n // GRID_W, dtype=F32), GRID_W)
    cols = (jnp.arange(n) % GRID_W).astype(F32)
    ar = rows[:, None] * inv[None, :]
    ac = cols[:, None] * inv[None, :]
    cos = jnp.concatenate([jnp.cos(ar), jnp.cos(ar), jnp.cos(ac), jnp.cos(ac)], axis=1)
    sin = jnp.concatenate([-jnp.sin(ar), jnp.sin(ar), -jnp.sin(ac), jnp.sin(ac)], axis=1)
    if dim < LANES:
        cos = jnp.concatenate([cos, jnp.ones((n, LANES - dim), F32)], axis=1)
        sin = jnp.concatenate([sin, jnp.zeros((n, LANES - dim), F32)], axis=1)
    return cos, sin


def _even_mixer(hx, hy, x, y, gate_x, gate_y, w_in, sink, dw, ln_g, ln_b, w_out, tables, need_ctx):
    ch = dw.shape[1]
    aq = w_out.shape[0] - ch
    akv = (w_in.shape[1] - aq - 2 * ch) // 2
    cos, sin = tables
    scale = HEAD_DIM ** -0.5 * LOG2_E
    pair = HEAD_DIM // 4
    wq = w_in[:, :aq].astype(BF16)
    wk = w_in[:, aq:aq + akv].astype(BF16)
    w_agv = jnp.concatenate([w_in[:, aq + 2 * akv:], w_in[:, aq + akv:aq + 2 * akv]], axis=1).astype(BF16)
    w_o1 = w_out[:aq].astype(BF16)
    w_o2 = w_out[aq:].astype(BF16)
    v_col0 = 2 * ch // HEAD_DIM

    qx = _matmul([hx], [wq], out_dtype=BF16, epilogue="rope", extras=(cos, sin), scale=scale, pair=pair)
    kx = _matmul([hx], [wk], out_dtype=BF16, epilogue="rope", extras=(cos, sin), pair=pair)
    agv_x = _matmul([hx], [w_agv], out_dtype=BF16)
    ky = _matmul([hy], [wk], out_dtype=BF16)
    agv_y = _matmul([hy], [w_agv], out_dtype=BF16)

    att_x = _gqa(qx, kx, agv_x, v_col0, ky, agv_y, sink, local=True)
    conv_x = _conformer_conv(agv_x, dw, ln_g, ln_b)
    x = _matmul([att_x, conv_x], [w_o1, w_o2], out_dtype=F32, epilogue="resid", extras=(x, gate_x), bn=512)
    if need_ctx:
        qy = _matmul([hy], [wq], out_dtype=BF16, scale=scale)
        att_y = _gqa(qy, None, None, v_col0, ky, agv_y, sink, local=False)
        conv_y = _conformer_conv(agv_y, dw, ln_g, ln_b)
        y = _matmul([att_y, conv_y], [w_o1, w_o2], out_dtype=F32, epilogue="resid", extras=(y, gate_y), bn=512)
    return x, y


def _mla_mixer(hx, hy, x, y, gate_x, gate_y, w_dn, q_g, kv_g, w_uq, w_ukv, w_o, tables, need_ctx):
    n, d = hx.shape
    lc = hy.shape[0]
    q_rank, kv_rank = q_g.shape[0], kv_g.shape[0]
    n_heads = w_o.shape[0] // HEAD_DIM
    rope_dim = w_dn.shape[1] - q_rank - kv_rank
    assert n % lc == 0 and rope_dim <= LANES
    cos, sin = tables
    pair = rope_dim // 4
    scale = (HEAD_DIM + rope_dim) ** -0.5 * LOG2_E
    w_dn_p = jnp.concatenate([w_dn, jnp.zeros((d, LANES - rope_dim), F32)], axis=1).astype(BF16)
    w_uq3 = w_uq.reshape(q_rank, n_heads, HEAD_DIM + rope_dim)
    w_qn = w_uq3[:, :, :HEAD_DIM].reshape(q_rank, n_heads * HEAD_DIM).astype(BF16)
    w_qr = jnp.concatenate([w_uq3[:, :, HEAD_DIM:], jnp.zeros((q_rank, n_heads, LANES - rope_dim), F32)],
                           axis=2).reshape(q_rank, n_heads * LANES).astype(BF16)
    w_kv3 = w_ukv.reshape(kv_rank, n_heads, 2 * HEAD_DIM)
    w_kv = jnp.concatenate([w_kv3[:, :, :HEAD_DIM].reshape(kv_rank, -1), w_kv3[:, :, HEAD_DIM:].reshape(kv_rank, -1)],
                           axis=1).astype(BF16)
    w_ob = w_o.astype(BF16)

    qlat_x, ckv_x, kr_x = _mla_down(hx, w_dn_p, q_g, kv_g, cos, sin, pair=pair, use_rope=True)
    qlat_y, ckv_y, kr_y = _mla_down(hy, w_dn_p, q_g, kv_g, cos[:lc], sin[:lc], pair=pair, use_rope=False)
    kv_all = _matmul([jnp.concatenate([ckv_x, ckv_y], axis=0)], [w_kv], out_dtype=BF16, bm=1280)
    kr_all = jnp.concatenate([kr_x, kr_y], axis=0)

    qn_x = _matmul([qlat_x], [w_qn], out_dtype=BF16, scale=scale)
    qr_x = _matmul([qlat_x], [w_qr], out_dtype=BF16, epilogue="rope", extras=(cos, sin), scale=scale, pair=pair)
    o_x = _mla_attn(qn_x, qr_x, kv_all, kr_all, n_heads=n_heads, key_rows=n + lc, key_block=0)
    x = _matmul([o_x], [w_ob], out_dtype=F32, epilogue="resid", extras=(x, gate_x), bn=512)
    if need_ctx:
        qn_y = _matmul([qlat_y], [w_qn], out_dtype=BF16, scale=scale)
        qr_y = _matmul([qlat_y], [w_qr], out_dtype=BF16, scale=scale)
        o_y = _mla_attn(qn_y, qr_y, kv_all, kr_all, n_heads=n_heads, key_rows=lc, key_block=n // lc)
        y = _matmul([o_y], [w_ob], out_dtype=F32, epilogue="resid", extras=(y, gate_y), bn=512)
    return x, y


def kernel(x, c, ctx, c_ctx, ada_w, ada_b, norm1_g, norm2_g, ev_w_in, ev_sink, ev_dw, ev_ln_g, ev_ln_b, ev_w_out, od_w_dn, od_q_norm_g, od_kv_norm_g, od_w_uq, od_w_ukv, od_w_o, moe_router, moe_w_gate, moe_w_up, moe_w_down, final_g):
    b, n, d = x.shape
    assert b == 1 and n % GRID_W == 0
    depth = ada_w.shape[0]
    xr, yr = x[0], ctx[0]
    mods = _adaln(c, c_ctx, ada_w, ada_b)
    tables_a = _rope_tables(n, HEAD_DIM)
    rope_dim = od_w_dn.shape[2] - od_q_norm_g.shape[1] - od_kv_norm_g.shape[1]
    tables_c = _rope_tables(n, rope_dim)
    wgu = jnp.concatenate([moe_w_gate, moe_w_up], axis=3).astype(BF16)
    wd = moe_w_down.astype(BF16)
    for l in range(depth):
        need_ctx = l < depth - 1
        mx = [mods[l, 0:1, i * d:(i + 1) * d] for i in range(6)]
        my = [mods[l, 1:2, i * d:(i + 1) * d] for i in range(6)]
        hx = _normmod(xr, norm1_g[l], mx[0], mx[1], BF16)
        hy = _normmod(yr, norm1_g[l], my[0], my[1], BF16)
        i = l // 2
        if l % 2 == 0:
            xr, yr = _even_mixer(hx, hy, xr, yr, mx[2], my[2], ev_w_in[i], ev_sink[i], ev_dw[i], ev_ln_g[i],
                                 ev_ln_b[i], ev_w_out[i], tables_a, need_ctx)
        else:
            xr, yr = _mla_mixer(hx, hy, xr, yr, mx[2], my[2], od_w_dn[i], od_q_norm_g[i], od_kv_norm_g[i],
                                od_w_uq[i], od_w_ukv[i], od_w_o[i], tables_c, need_ctx)
        w_rt = moe_router[l].T
        xr = _ec_moe(xr, norm2_g[l], mx[3], mx[4], mx[5], w_rt, wgu, wd, l)
        if need_ctx:
            yr = _ec_moe(yr, norm2_g[l], my[3], my[4], my[5], w_rt, wgu, wd, l)
    zero = jnp.zeros((1, d), F32)
    return _normmod(xr, final_g, zero, zero, F32)[None]
```

```python
import functools

import jax
import jax.numpy as jnp
from jax import lax
from jax.experimental import pallas as pl
from jax.experimental.pallas import tpu as pltpu

F32 = jnp.float32
BF16 = jnp.bfloat16

GRID_W = 64
ROPE_BASE = 10000.0
NORM_EPS = 1e-6
NEG_INF = -1e30
HEAD_DIM = 128
A_WINDOW = 128
EC_CAPACITY = 2
LOG2_E = 1.4426950408889634

LANES = 128
SUBLANES = 8
BF16_ROWS = 16
MIB = 1024 * 1024


def _params(n_axes, vmem_mib):
    return pltpu.CompilerParams(dimension_semantics=("arbitrary",) * n_axes, vmem_limit_bytes=vmem_mib * MIB)


def _pick(total, target, mult):
    if total <= target:
        return total
    best = None
    for d in range(mult, target + 1, mult):
        if total % d == 0:
            best = d
    assert best is not None, (total, target, mult)
    return best


def _sigmoid(v):
    return 1.0 / (1.0 + jnp.exp(-v))


def _norm_mod(xv, g, shift, scale):
    yv = xv * lax.rsqrt(jnp.mean(xv * xv, axis=-1, keepdims=True) + NORM_EPS)
    return (yv * g) * (1.0 + scale) + shift


def _rope(xv, cos, sin, pair):
    lane = lax.broadcasted_iota(jnp.int32, xv.shape, 1)
    first = (lane % (2 * pair)) < pair
    partner = jnp.where(first, pltpu.roll(xv, LANES - pair, 1), pltpu.roll(xv, pair, 1))
    return xv * cos + partner * sin


def _adaln_kernel(cc_ref, w_ref, b_ref, o_ref):
    cc = cc_ref[...]
    s = cc * _sigmoid(cc)
    w = w_ref[...]
    b = b_ref[...]
    r0 = jnp.sum(s[:, 0:1] * w, axis=0, keepdims=True) + b
    r1 = jnp.sum(s[:, 1:2] * w, axis=0, keepdims=True) + b
    o_ref[...] = jnp.concatenate([r0, r1, jnp.zeros((SUBLANES - 2, w.shape[1]), F32)], axis=0)


def _adaln(c, c_ctx, ada_w, ada_b):
    depth, d, n6 = ada_w.shape
    cc = jnp.zeros((d, SUBLANES), F32).at[:, 0].set(c[0]).at[:, 1].set(c_ctx)
    tn = _pick(n6, 512, LANES)
    return pl.pallas_call(
        _adaln_kernel,
        out_shape=jax.ShapeDtypeStruct((depth, SUBLANES, n6), F32),
        grid=(depth, n6 // tn),
        in_specs=[pl.BlockSpec((d, SUBLANES), lambda l, j: (0, 0)),
                  pl.BlockSpec((None, d, tn), lambda l, j: (l, 0, j)),
                  pl.BlockSpec((None, 1, tn), lambda l, j: (l, 0, j))],
        out_specs=pl.BlockSpec((None, SUBLANES, tn), lambda l, j: (l, 0, j)),
        compiler_params=_params(2, 48),
        name="adaln",
    )(cc, ada_w, ada_b.reshape(depth, 1, n6))


def _normmod_kernel(x_ref, g_ref, sh_ref, sc_ref, o_ref):
    o_ref[...] = _norm_mod(x_ref[...], g_ref[...], sh_ref[...], sc_ref[...]).astype(o_ref.dtype)


def _normmod(xr, g, shift, scale, out_dtype):
    r, d = xr.shape
    tr = _pick(r, 512, BF16_ROWS)
    vec = pl.BlockSpec((1, d), lambda i: (0, 0))
    return pl.pallas_call(
        _normmod_kernel,
        out_shape=jax.ShapeDtypeStruct((r, d), out_dtype),
        grid=(r // tr,),
        in_specs=[pl.BlockSpec((tr, d), lambda i: (i, 0)), vec, vec, vec],
        out_specs=pl.BlockSpec((tr, d), lambda i: (i, 0)),
        compiler_params=_params(1, 48),
        name="normmod",
    )(xr, g.reshape(1, d), shift, scale)


def _mm_kernel(*refs, n_pairs, epilogue, scale, pair):
    a_refs, w_refs, rest = refs[:n_pairs], refs[n_pairs:2 * n_pairs], refs[2 * n_pairs:]
    acc = None
    for a_ref, w_ref in zip(a_refs, w_refs):
        part = jnp.dot(a_ref[...], w_ref[...], preferred_element_type=F32)
        acc = part if acc is None else acc + part
    if epilogue == "plain":
        (o_ref,) = rest
        o_ref[...] = (acc * scale if scale != 1.0 else acc).astype(o_ref.dtype)
    elif epilogue == "rope":
        cos_ref, sin_ref, o_ref = rest
        cos, sin = cos_ref[...], sin_ref[...]
        for h in range(acc.shape[1] // LANES):
            sl = slice(h * LANES, (h + 1) * LANES)
            r = _rope(acc[:, sl], cos, sin, pair)
            o_ref[:, sl] = (r * scale if scale != 1.0 else r).astype(o_ref.dtype)
    else:
        x_ref, gate_ref, o_ref = rest
        o_ref[...] = x_ref[...] + gate_ref[...] * acc


def _matmul(a_list, w_list, *, out_dtype, epilogue="plain", extras=(), scale=1.0, pair=0, bm=1024, bn=1024):
    m = a_list[0].shape[0]
    n = w_list[0].shape[1]
    bm = _pick(m, bm, BF16_ROWS)
    bn = _pick(n, bn, LANES)
    in_specs = [pl.BlockSpec((bm, a.shape[1]), lambda j, i: (i, 0)) for a in a_list]
    in_specs += [pl.BlockSpec((w.shape[0], bn), lambda j, i: (0, j)) for w in w_list]
    if epilogue == "rope":
        in_specs += [pl.BlockSpec((bm, LANES), lambda j, i: (i, 0))] * 2
    elif epilogue == "resid":
        in_specs += [pl.BlockSpec((bm, bn), lambda j, i: (i, j)), pl.BlockSpec((1, bn), lambda j, i: (0, j))]
    return pl.pallas_call(
        functools.partial(_mm_kernel, n_pairs=len(a_list), epilogue=epilogue, scale=scale, pair=pair),
        out_shape=jax.ShapeDtypeStruct((m, n), out_dtype),
        grid=(n // bn, m // bm),
        in_specs=in_specs,
        out_specs=pl.BlockSpec((bm, bn), lambda j, i: (i, j)),
        compiler_params=_params(2, 56),
        name="mm_" + epilogue,
    )(*a_list, *w_list, *extras)


def _gqa_kernel(sink_ref, *refs, groups, tq, local, n_tokens):
    if local:
        q_ref, k_ref, v_ref, ky_ref, vy_ref, o_ref = refs
    else:
        q_ref, ky_ref, vy_ref, o_ref = refs
    hk = pl.program_id(0)
    nt = (((1,), (1,)), ((), ()))
    def with_ones(v):
        return jnp.concatenate([v, jnp.ones(v.shape, BF16)], axis=1)

    ky = ky_ref[...]
    vy = with_ones(vy_ref[...])
    if local:
        n = pl.program_id(1)
        win = tq + 2 * A_WINDOW
        start = pl.multiple_of(jnp.clip(n * tq - A_WINDOW, 0, n_tokens - win), BF16_ROWS)
        kw = k_ref[pl.ds(start, win), :]
        vw = with_ones(v_ref[pl.ds(start, win), :])
        qpos = n * tq + lax.broadcasted_iota(jnp.int32, (tq, win), 0)
        kpos = start + lax.broadcasted_iota(jnp.int32, (tq, win), 1)
        band = jnp.abs(kpos - qpos) <= A_WINDOW
    for gi in range(groups):
        sl = slice(gi * HEAD_DIM, (gi + 1) * HEAD_DIM)
        qg = q_ref[:, sl]
        sink = sink_ref[hk * groups + gi] * LOG2_E
        s_ctx = lax.dot_general(qg, ky, nt, preferred_element_type=F32)
        m = jnp.maximum(jnp.max(s_ctx, axis=1, keepdims=True), sink)
        if local:
            s_loc = jnp.where(band, lax.dot_general(qg, kw, nt, preferred_element_type=F32), NEG_INF)
            m = jnp.maximum(m, jnp.max(s_loc, axis=1, keepdims=True))
        acc = jnp.dot(jnp.exp2(s_ctx - m).astype(BF16), vy, preferred_element_type=F32)
        if local:
            acc = acc + jnp.dot(jnp.exp2(s_loc - m).astype(BF16), vw, preferred_element_type=F32)
        den = acc[:, HEAD_DIM:] + jnp.exp2(sink - m)
        o_ref[:, sl] = (acc[:, :HEAD_DIM] / den).astype(o_ref.dtype)


def _gqa(q, kx, vsrc_x, v_col0, ky, vsrc_y, sink, *, local):
    r, aq = q.shape
    hkv = ky.shape[1] // HEAD_DIM
    groups = aq // HEAD_DIM // hkv
    lc = ky.shape[0]
    gw = groups * HEAD_DIM
    tq = _pick(r, 256, BF16_ROWS)
    n_tokens = kx.shape[0] if local else 0
    if local:
        assert n_tokens >= tq + 2 * A_WINDOW
    q_spec = pl.BlockSpec((tq, gw), lambda h, i, s: (i, h))
    ctx_specs = [pl.BlockSpec((lc, HEAD_DIM), lambda h, i, s: (0, h)),
                 pl.BlockSpec((lc, HEAD_DIM), lambda h, i, s: (0, v_col0 + h))]
    if local:
        in_specs = [q_spec,
                    pl.BlockSpec((n_tokens, HEAD_DIM), lambda h, i, s: (0, h)),
                    pl.BlockSpec((n_tokens, HEAD_DIM), lambda h, i, s: (0, v_col0 + h))] + ctx_specs
        operands = (q, kx, vsrc_x, ky, vsrc_y)
    else:
        in_specs = [q_spec] + ctx_specs
        operands = (q, ky, vsrc_y)
    return pl.pallas_call(
        functools.partial(_gqa_kernel, groups=groups, tq=tq, local=local, n_tokens=n_tokens),
        out_shape=jax.ShapeDtypeStruct((r, aq), BF16),
        grid_spec=pltpu.PrefetchScalarGridSpec(
            num_scalar_prefetch=1, grid=(hkv, r // tq), in_specs=in_specs,
            out_specs=pl.BlockSpec((tq, gw), lambda h, i, s: (i, h))),
        compiler_params=_params(2, 48),
        name="gqa_local" if local else "gqa_ctx",
    )(sink, *operands)


CONV_HALO = 16
CONV_ROWS = 32
CONV_LANES = 256


def _conv_kernel(a_ref, g_ref, ap_ref, gp_ref, an_ref, gn_ref, dw_ref, lg_ref, lb_ref, o_ref, u_ref, z_ref,
                 *, taps, n_blocks):
    i = pl.program_id(0)
    tr, ch = z_ref.shape
    pad = (taps - 1) // 2

    def glu(a, g):
        return a.astype(F32) * _sigmoid(g.astype(F32))

    u_ref[0, 0:CONV_HALO, :] = glu(ap_ref[...], gp_ref[...]) * (i > 0).astype(F32)
    u_ref[0, CONV_HALO:CONV_HALO + tr, :] = glu(a_ref[...], g_ref[...])
    u_ref[0, CONV_HALO + tr:, :] = glu(an_ref[...], gn_ref[...]) * (i < n_blocks - 1).astype(F32)
    span = u_ref.shape[1] - SUBLANES
    for s in range(1, SUBLANES):
        u_ref[s, 0:span, :] = u_ref[0, s:s + span, :]

    def lane_chunk(cc, carry):
        c0 = pl.multiple_of(cc * CONV_LANES, CONV_LANES)
        for r0 in range(0, tr, CONV_ROWS):
            acc = jnp.zeros((CONV_ROWS, CONV_LANES), F32)
            for j in range(taps):
                shift = j - pad + CONV_HALO
                acc = acc + (u_ref[shift % SUBLANES, pl.ds(r0 + shift - shift % SUBLANES, CONV_ROWS),
                                   pl.ds(c0, CONV_LANES)]
                             * dw_ref[pl.ds(j, 1), pl.ds(c0, CONV_LANES)])
            z_ref[pl.ds(r0, CONV_ROWS), pl.ds(c0, CONV_LANES)] = acc
        return carry

    lax.fori_loop(0, ch // CONV_LANES, lane_chunk, 0)
    z = z_ref[...]
    mu = jnp.mean(z, axis=-1, keepdims=True)
    zc = z - mu
    var = jnp.mean(zc * zc, axis=-1, keepdims=True)
    yv = zc * lax.rsqrt(var + NORM_EPS) * lg_ref[...] + lb_ref[...]
    o_ref[...] = (yv * _sigmoid(yv)).astype(o_ref.dtype)


def _conformer_conv(agv, dw, ln_g, ln_b):
    r = agv.shape[0]
    taps, ch = dw.shape
    assert (taps - 1) // 2 < CONV_HALO and ch % CONV_LANES == 0
    tr = _pick(r, 256, CONV_ROWS)
    nb = r // tr
    hb = tr // CONV_HALO
    last = r // CONV_HALO - 1
    dwp = jnp.zeros((2 * CONV_HALO, ch), F32).at[:taps].set(dw)
    cur = lambda col: pl.BlockSpec((tr, ch), lambda i: (i, col))
    prv = lambda col: pl.BlockSpec((CONV_HALO, ch), lambda i: (jnp.maximum(i * hb - 1, 0), col))
    nxt = lambda col: pl.BlockSpec((CONV_HALO, ch), lambda i: (jnp.minimum((i + 1) * hb, last), col))
    vec = pl.BlockSpec((1, ch), lambda i: (0, 0))
    return pl.pallas_call(
        functools.partial(_conv_kernel, taps=taps, n_blocks=nb),
        out_shape=jax.ShapeDtypeStruct((r, ch), BF16),
        grid=(nb,),
        in_specs=[cur(0), cur(1), prv(0), prv(1), nxt(0), nxt(1),
                  pl.BlockSpec((2 * CONV_HALO, ch), lambda i: (0, 0)), vec, vec],
        out_specs=pl.BlockSpec((tr, ch), lambda i: (i, 0)),
        scratch_shapes=[pltpu.VMEM((SUBLANES, tr + 2 * CONV_HALO, ch), F32), pltpu.VMEM((tr, ch), F32)],
        compiler_params=_params(1, 48),
        name="conformer_conv",
    )(agv, agv, agv, agv, agv, agv, dwp, ln_g.reshape(1, ch), ln_b.reshape(1, ch))


def _mla_down_kernel(h_ref, w_ref, qg_ref, kg_ref, cos_ref, sin_ref, q_ref, kv_ref, kr_ref, *, q_rank, kv_rank,
                     pair, use_rope):
    acc = jnp.dot(h_ref[...], w_ref[...], preferred_element_type=F32)

    def rms(v, g):
        return v * lax.rsqrt(jnp.mean(v * v, axis=-1, keepdims=True) + NORM_EPS) * g

    q_ref[...] = rms(acc[:, :q_rank], qg_ref[...]).astype(q_ref.dtype)
    kv_ref[...] = rms(acc[:, q_rank:q_rank + kv_rank], kg_ref[...]).astype(kv_ref.dtype)
    kr = acc[:, q_rank + kv_rank:]
    if use_rope:
        kr = _rope(kr, cos_ref[...], sin_ref[...], pair)
    kr_ref[...] = kr.astype(kr_ref.dtype)


def _mla_down(h, w_dn_p, q_g, kv_g, cos, sin, *, pair, use_rope):
    m, d = h.shape
    q_rank, kv_rank = q_g.shape[0], kv_g.shape[0]
    wn = w_dn_p.shape[1]
    assert wn == q_rank + kv_rank + LANES
    bm = _pick(m, 512, BF16_ROWS)
    row = lambda w: pl.BlockSpec((bm, w), lambda i: (i, 0))
    return pl.pallas_call(
        functools.partial(_mla_down_kernel, q_rank=q_rank, kv_rank=kv_rank, pair=pair, use_rope=use_rope),
        out_shape=(jax.ShapeDtypeStruct((m, q_rank), BF16), jax.ShapeDtypeStruct((m, kv_rank), BF16),
                   jax.ShapeDtypeStruct((m, LANES), BF16)),
        grid=(m // bm,),
        in_specs=[row(d), pl.BlockSpec((d, wn), lambda i: (0, 0)),
                  pl.BlockSpec((1, q_rank), lambda i: (0, 0)), pl.BlockSpec((1, kv_rank), lambda i: (0, 0)),
                  row(LANES), row(LANES)],
        out_specs=(row(q_rank), row(kv_rank), row(LANES)),
        compiler_params=_params(1, 56),
        name="mla_down",
    )(h, w_dn_p, q_g.reshape(1, q_rank), kv_g.reshape(1, kv_rank), cos, sin)


MLA_TQ = 1024
MLA_SUB = 256
MLA_KEY_CHUNK = 1280
MLA_STAGES = 12


def _mla_attn_kernel(qn_ref, qr_ref, kn_ref, kr_ref, v_ref, o_ref, kcat_ref, vcat_ref, qcat_ref, s_ref, *, tk, sub):
    nk = kcat_ref.shape[0]
    tq = qn_ref.shape[0]
    n_sub = tq // sub

    @pl.when(pl.program_id(1) == 0)
    def _():
        kcat_ref[:, :HEAD_DIM] = kn_ref[...]
        kcat_ref[:, HEAD_DIM:] = kr_ref[...]
        vcat_ref[:, :HEAD_DIM] = v_ref[...]
        vcat_ref[:, HEAD_DIM:] = jnp.ones((nk, HEAD_DIM), BF16)

    qcat_ref[:, :HEAD_DIM] = qn_ref[...]
    qcat_ref[:, HEAD_DIM:] = qr_ref[...]
    nt = (((1,), (1,)), ((), ()))
    n_chunks = nk // tk

    def scores(buf, u, off):
        s_ref[buf, u] = lax.dot_general(qcat_ref[u * sub:(u + 1) * sub, :], kcat_ref[pl.ds(off, tk), :], nt,
                                        preferred_element_type=F32)

    def absorb(buf, u, off, m, acc):
        s = s_ref[buf, u]
        m_new = jnp.maximum(m, jnp.max(s, axis=1, keepdims=True))
        p = jnp.exp2(s - m_new).astype(BF16)
        acc = jnp.exp2(m - m_new) * acc + jnp.dot(p, vcat_ref[pl.ds(off, tk), :], preferred_element_type=F32)
        return m_new, acc

    def stage(buf, off, carry):
        out = []
        for u in range(n_sub):
            out.append(absorb(buf, u, off, *carry[u]))
            scores(1 - buf, u, off + tk)
        return tuple(out)

    def trip(j, carry):
        off = pl.multiple_of(j * (MLA_STAGES * tk), tk)
        for k in range(MLA_STAGES):
            carry = stage(k % 2, off + k * tk, carry)
        return carry

    for u in range(n_sub):
        scores(0, u, 0)
    carry = tuple((jnp.full((sub, 1), NEG_INF, F32), jnp.zeros((sub, 2 * HEAD_DIM), F32)) for _ in range(n_sub))
    n_stages = n_chunks - 1
    carry = lax.fori_loop(0, n_stages // MLA_STAGES, trip, carry)
    for k in range(n_stages - n_stages % MLA_STAGES, n_stages):
        carry = stage(k % 2, k * tk, carry)
    for u in range(n_sub):
        _, acc = absorb(n_stages % 2, u, n_stages * tk, *carry[u])
        o_ref[u * sub:(u + 1) * sub, :] = (acc[:, :HEAD_DIM] / acc[:, HEAD_DIM:]).astype(o_ref.dtype)


def _mla_attn(qn, qr, kv, kr, *, n_heads, key_rows, key_block):
    r = qn.shape[0]
    tq = _pick(r, MLA_TQ, MLA_SUB) if r >= MLA_SUB else r
    sub = min(MLA_SUB, tq)
    tk = _pick(key_rows, MLA_KEY_CHUNK, LANES)
    once = pl.Buffered(1)
    kspec = lambda col0: pl.BlockSpec((key_rows, HEAD_DIM), lambda h, i: (key_block, col0 + h), pipeline_mode=once)
    qspec = pl.BlockSpec((tq, HEAD_DIM), lambda h, i: (i, h))
    return pl.pallas_call(
        functools.partial(_mla_attn_kernel, tk=tk, sub=sub),
        out_shape=jax.ShapeDtypeStruct((r, n_heads * HEAD_DIM), BF16),
        grid=(n_heads, r // tq),
        in_specs=[qspec, qspec, kspec(0),
                  pl.BlockSpec((key_rows, LANES), lambda h, i: (key_block, 0), pipeline_mode=once), kspec(n_heads)],
        out_specs=pl.BlockSpec((tq, HEAD_DIM), lambda h, i: (i, h)),
        scratch_shapes=[pltpu.VMEM((key_rows, 2 * HEAD_DIM), BF16), pltpu.VMEM((key_rows, 2 * HEAD_DIM), BF16),
                        pltpu.VMEM((tq, 2 * HEAD_DIM), BF16), pltpu.VMEM((2, tq // sub, sub, tk), F32)],
        compiler_params=_params(2, 60),
        name="mla_attn",
    )(qn, qr, kv, kr, kv)


def _router_kernel(x_ref, g_ref, sh_ref, sc_ref, wt_ref, aff_ref):
    h = _norm_mod(x_ref[...], g_ref[...], sh_ref[...], sc_ref[...])
    w = wt_ref[...]
    ne = w.shape[0]
    h_hi = h.astype(BF16)
    h_lo = (h - h_hi.astype(F32)).astype(BF16)
    w_hi = w.astype(BF16)
    w_lo = (w - w_hi.astype(F32)).astype(BF16)
    nt = (((1,), (1,)), ((), ()))
    both = lax.dot_general(jnp.concatenate([w_hi, w_lo], axis=0), h_hi, nt, preferred_element_type=F32)
    logits = both[:ne] + both[ne:] + lax.dot_general(w_hi, h_lo, nt, preferred_element_type=F32)
    e = jnp.exp(logits - jnp.max(logits, axis=0, keepdims=True))
    aff_ref[...] = e / jnp.sum(e, axis=0, keepdims=True)


def _router(xr, g, shift, scale, w_router_t):
    r, d = xr.shape
    ne = w_router_t.shape[0]
    tr = _pick(r, 512, LANES)
    vec = pl.BlockSpec((1, d), lambda i: (0, 0))
    return pl.pallas_call(
        _router_kernel,
        out_shape=jax.ShapeDtypeStruct((ne, r), F32),
        grid=(r // tr,),
        in_specs=[pl.BlockSpec((tr, d), lambda i: (i, 0)), vec, vec, vec, pl.BlockSpec((ne, d), lambda i: (0, 0))],
        out_specs=pl.BlockSpec((ne, tr), lambda i: (0, i)),
        compiler_params=_params(1, 48),
        name="router",
    )(xr, g.reshape(1, d), shift, scale, w_router_t)


def _lane_cumsum(v):
    lane = lax.broadcasted_iota(jnp.int32, v.shape, 1)
    k = 1
    while k < LANES:
        v = v + jnp.where(lane >= k, pltpu.roll(v, k, 1), 0)
        k *= 2
    return v


def _select_kernel(a_ref, key_ref, *, cap, n_pad):
    a = a_ref[...]
    ne, g, _ = a.shape
    bits = lax.bitcast_convert_type(a, jnp.int32)

    def count(mask):
        return jnp.sum(jnp.sum(mask.astype(F32), axis=2, keepdims=True), axis=1, keepdims=True)

    thr = jnp.zeros((ne, 1, 1), jnp.int32)
    for b in range(30, -1, -1):
        cand = thr | (1 << b)
        thr = jnp.where(count(bits >= cand) >= cap, cand, thr)
    above = bits > thr
    tie = bits == thr
    need = cap - count(above)
    tie_i = tie.astype(jnp.int32)
    incl = _lane_cumsum(tie_i.reshape(ne * g, LANES)).reshape(ne, g, LANES)
    lower = (lax.broadcasted_iota(jnp.int32, (g, g), 1) < lax.broadcasted_iota(jnp.int32, (g, g), 0)).astype(F32)
    tok = lax.broadcasted_iota(jnp.int32, (g, LANES), 0) * LANES + lax.broadcasted_iota(jnp.int32, (g, LANES), 1)
    for e in range(ne):
        row_tot = jnp.broadcast_to(incl[e, :, LANES - 1:LANES].astype(F32), (g, LANES))
        row_off = jnp.dot(lower, row_tot, precision=lax.Precision.HIGHEST, preferred_element_type=F32)
        rank = (incl[e] - tie_i[e]).astype(F32) + row_off
        sel = above[e] | (tie[e] & (rank < need[e]))
        key_ref[e] = jnp.where(sel, tok, n_pad)


def _select(aff_t, cap):
    ne, r = aff_t.shape
    assert r % LANES == 0
    n_pad = max(r, SUBLANES * LANES)
    g = n_pad // LANES
    a3 = jnp.pad(aff_t, ((0, 0), (0, n_pad - r))).reshape(ne, g, LANES)
    keys = pl.pallas_call(
        functools.partial(_select_kernel, cap=cap, n_pad=n_pad),
        out_shape=jax.ShapeDtypeStruct((ne, g, LANES), jnp.int32),
        grid=(1,),
        in_specs=[pl.BlockSpec((ne, g, LANES), lambda i: (0, 0, 0))],
        out_specs=pl.BlockSpec((ne, g, LANES), lambda i: (0, 0, 0)),
        compiler_params=_params(1, 32),
        name="ec_select",
    )(a3)
    idx = jnp.sort(keys.reshape(ne, n_pad), axis=1)[:, :cap]
    return idx, jnp.take_along_axis(aff_t, idx, axis=1)


def _row_copy(src_hbm, buf, sem, slot, tok, r):
    return pltpu.make_async_copy(src_hbm.at[pl.ds(tok, 1), :], buf.at[slot, pl.ds(r, 1), :], sem.at[slot])


def _start_gather(idx_ref, src_hbm, buf, sem, step, slot, rows):
    for r in range(rows):
        _row_copy(src_hbm, buf, sem, slot, idx_ref[step * rows + r], r).start()


def _for_slot(slot, fn):
    for s in (0, 1):
        pl.when(slot == s)(functools.partial(fn, s))


def _wait_rows(src_hbm, buf, sem, slot, rows):
    for _ in range(rows):
        _row_copy(src_hbm, buf, sem, slot, 0, 0).wait()


def _moe_ffn_kernel(idx_ref, gate_ref, g_ref, sh_ref, sc_ref, wgu_ref, wd_ref, x_hbm, y_ref, xbuf, sem,
                    *, rows, n_steps):
    step = pl.program_id(0) * pl.num_programs(1) + pl.program_id(1)
    slot = step % 2

    @pl.when(step == 0)
    def _():
        _start_gather(idx_ref, x_hbm, xbuf, sem, 0, 0, rows)

    @pl.when(step + 1 < n_steps)
    def _():
        _for_slot(1 - slot, lambda s: _start_gather(idx_ref, x_hbm, xbuf, sem, step + 1, s, rows))

    _wait_rows(x_hbm, xbuf, sem, slot, rows)
    h = _norm_mod(xbuf[slot], g_ref[...], sh_ref[...], sc_ref[...]).astype(BF16)
    ff = wd_ref.shape[0]
    halves = []
    for hh in (h[:rows // 2], h[rows // 2:]):
        au = jnp.dot(hh, wgu_ref[...], preferred_element_type=F32)
        a, u = au[:, :ff], au[:, ff:]
        halves.append((a * _sigmoid(a) * u).astype(BF16))
    z = jnp.concatenate(halves, axis=0)
    y_ref[...] = jnp.dot(z, wd_ref[...], preferred_element_type=F32) * gate_ref[...]


def _moe_ffn(idx_flat, gate_col, xr, g, shift, scale, wgu, wd, layer, *, rows):
    r, d = xr.shape
    _, ne, ff, _ = wd.shape
    assert ff % LANES == 0
    slots = idx_flat.shape[0]
    nc = slots // ne // rows
    vec = pl.BlockSpec((1, d), lambda e, c, idx: (0, 0))
    return pl.pallas_call(
        functools.partial(_moe_ffn_kernel, rows=rows, n_steps=ne * nc),
        out_shape=jax.ShapeDtypeStruct((slots, d), F32),
        grid_spec=pltpu.PrefetchScalarGridSpec(
            num_scalar_prefetch=1, grid=(ne, nc),
            in_specs=[pl.BlockSpec((rows, 1), lambda e, c, idx: (e * nc + c, 0)), vec, vec, vec,
                      pl.BlockSpec((None, None, d, 2 * ff), lambda e, c, idx: (layer, e, 0, 0)),
                      pl.BlockSpec((None, None, ff, d), lambda e, c, idx: (layer, e, 0, 0)),
                      pl.BlockSpec(memory_space=pl.ANY)],
            out_specs=pl.BlockSpec((rows, d), lambda e, c, idx: (e * nc + c, 0)),
            scratch_shapes=[pltpu.VMEM((2, rows, d), F32), pltpu.SemaphoreType.DMA((2,))]),
        compiler_params=_params(2, 56),
        name="moe_ffn",
    )(idx_flat, gate_col, g.reshape(1, d), shift, scale, wgu, wd, xr)


def _moe_combine_kernel(idx_ref, y_ref, gate_ref, x_hbm, o_hbm, abuf, gsem, ssem, *, rows, n_steps):
    del x_hbm
    c = pl.program_id(1)
    nc = pl.num_programs(1)
    step = pl.program_id(0) * nc + c
    slot = step % 2

    def scatter_copy(s, sl, r):
        return pltpu.make_async_copy(abuf.at[sl, pl.ds(r, 1), :], o_hbm.at[pl.ds(idx_ref[s * rows + r], 1), :],
                                     ssem.at[sl])

    def wait_scatter(sl):
        for _ in range(rows):
            pltpu.make_async_copy(abuf.at[sl, pl.ds(0, 1), :], o_hbm.at[pl.ds(0, 1), :], ssem.at[sl]).wait()

    @pl.when(step == 0)
    def _():
        _start_gather(idx_ref, o_hbm, abuf, gsem, 0, 0, rows)

    _wait_rows(o_hbm, abuf, gsem, slot, rows)
    abuf[slot] = abuf[slot] + gate_ref[...] * y_ref[...]

    def start_scatter(s):
        for r in range(rows):
            scatter_copy(step, s, r).start()
    _for_slot(slot, start_scatter)

    @pl.when(c > 0)
    def _():
        wait_scatter(1 - slot)

    @pl.when(c == nc - 1)
    def _():
        wait_scatter(slot)

    @pl.when(step + 1 < n_steps)
    def _():
        _for_slot(1 - slot, lambda s: _start_gather(idx_ref, o_hbm, abuf, gsem, step + 1, s, rows))


def _moe_combine(idx_flat, y, gate_vec, xr, *, n_experts, rows):
    r, d = xr.shape
    slots = idx_flat.shape[0]
    nc = slots // n_experts // rows
    return pl.pallas_call(
        functools.partial(_moe_combine_kernel, rows=rows, n_steps=n_experts * nc),
        out_shape=jax.ShapeDtypeStruct((r, d), F32),
        grid_spec=pltpu.PrefetchScalarGridSpec(
            num_scalar_prefetch=1, grid=(n_experts, nc),
            in_specs=[pl.BlockSpec((rows, d), lambda e, c, idx: (e * nc + c, 0)),
                      pl.BlockSpec((1, d), lambda e, c, idx: (0, 0)),
                      pl.BlockSpec(memory_space=pl.ANY)],
            out_specs=pl.BlockSpec(memory_space=pl.ANY),
            scratch_shapes=[pltpu.VMEM((2, rows, d), F32), pltpu.SemaphoreType.DMA((2,)),
                            pltpu.SemaphoreType.DMA((2,))]),
        input_output_aliases={3: 0},
        compiler_params=_params(2, 48),
        name="moe_combine",
    )(idx_flat, y, gate_vec, xr)


def _ec_moe(xr, g, shift, scale, out_gate, w_router_t, wgu, wd, layer):
    r = xr.shape[0]
    ne = w_router_t.shape[0]
    cap = EC_CAPACITY * r // ne
    rows = _pick(cap, 256, SUBLANES)
    aff_t = _router(xr, g, shift, scale, w_router_t)
    idx, gate = _select(aff_t, cap)
    idx_flat = idx.reshape(-1)
    y = _moe_ffn(idx_flat, gate.reshape(-1, 1), xr, g, shift, scale, wgu, wd, layer, rows=rows)
    return _moe_combine(idx_flat, y, out_gate, xr, n_experts=ne, rows=rows)


def _rope_tables(n, dim):
    half = dim // 2
    inv = ROPE_BASE ** (-jnp.arange(0, half, 2, dtype=F32) / half)
    rows = jnp.repeat(jnp.arange(n // GRID_W, dtype=F32), GRID_W)
    cols = (jnp.arange(n) % GRID_W).astype(F32)
    ar = rows[:, None] * inv[None, :]
    ac = cols[:, None] * inv[None, :]
    cos = jnp.concatenate([jnp.cos(ar), jnp.cos(ar), jnp.cos(ac), jnp.cos(ac)], axis=1)
    sin = jnp.concatenate([-jnp.sin(ar), jnp.sin(ar), -jnp.sin(ac), jnp.sin(ac)], axis=1)
    if dim < LANES:
        cos = jnp.concatenate([cos, jnp.ones((n, LANES - dim), F32)], axis=1)
        sin = jnp.concatenate([sin, jnp.zeros((n, LANES - dim), F32)], axis=1)
    return cos, sin


def _even_mixer(hx, hy, x, y, gate_x, gate_y, w_in, sink, dw, ln_g, ln_b, w_out, tables, need_ctx):
    ch = dw.shape[1]
    aq = w_out.shape[0] - ch
    akv = (w_in.shape[1] - aq - 2 * ch) // 2
    cos, sin = tables
    scale = HEAD_DIM ** -0.5 * LOG2_E
    pair = HEAD_DIM // 4
    wq = w_in[:, :aq].astype(BF16)
    wk = w_in[:, aq:aq + akv].astype(BF16)
    w_agv = jnp.concatenate([w_in[:, aq + 2 * akv:], w_in[:, aq + akv:aq + 2 * akv]], axis=1).astype(BF16)
    w_o1 = w_out[:aq].astype(BF16)
    w_o2 = w_out[aq:].astype(BF16)
    v_col0 = 2 * ch // HEAD_DIM

    qx = _matmul([hx], [wq], out_dtype=BF16, epilogue="rope", extras=(cos, sin), scale=scale, pair=pair)
    kx = _matmul([hx], [wk], out_dtype=BF16, epilogue="rope", extras=(cos, sin), pair=pair)
    agv_x = _matmul([hx], [w_agv], out_dtype=BF16)
    ky = _matmul([hy], [wk], out_dtype=BF16)
    agv_y = _matmul([hy], [w_agv], out_dtype=BF16)

    att_x = _gqa(qx, kx, agv_x, v_col0, ky, agv_y, sink, local=True)
    conv_x = _conformer_conv(agv_x, dw, ln_g, ln_b)
    x = _matmul([att_x, conv_x], [w_o1, w_o2], out_dtype=F32, epilogue="resid", extras=(x, gate_x), bn=512)
    if need_ctx:
        qy = _matmul([hy], [wq], out_dtype=BF16, scale=scale)
        att_y = _gqa(qy, None, None, v_col0, ky, agv_y, sink, local=False)
        conv_y = _conformer_conv(agv_y, dw, ln_g, ln_b)
        y = _matmul([att_y, conv_y], [w_o1, w_o2], out_dtype=F32, epilogue="resid", extras=(y, gate_y), bn=512)
    return x, y


def _mla_mixer(hx, hy, x, y, gate_x, gate_y, w_dn, q_g, kv_g, w_uq, w_ukv, w_o, tables, need_ctx):
    n, d = hx.shape
    lc = hy.shape[0]
    q_rank, kv_rank = q_g.shape[0], kv_g.shape[0]
    n_heads = w_o.shape[0] // HEAD_DIM
    rope_dim = w_dn.shape[1] - q_rank - kv_rank
    assert n % lc == 0 and rope_dim <= LANES
    cos, sin = tables
    pair = rope_dim // 4
    scale = (HEAD_DIM + rope_dim) ** -0.5 * LOG2_E
    w_dn_p = jnp.concatenate([w_dn, jnp.zeros((d, LANES - rope_dim), F32)], axis=1).astype(BF16)
    w_uq3 = w_uq.reshape(q_rank, n_heads, HEAD_DIM + rope_dim)
    w_qn = w_uq3[:, :, :HEAD_DIM].reshape(q_rank, n_heads * HEAD_DIM).astype(BF16)
    w_qr = jnp.concatenate([w_uq3[:, :, HEAD_DIM:], jnp.zeros((q_rank, n_heads, LANES - rope_dim), F32)],
                           axis=2).reshape(q_rank, n_heads * LANES).astype(BF16)
    w_kv3 = w_ukv.reshape(kv_rank, n_heads, 2 * HEAD_DIM)
    w_kv = jnp.concatenate([w_kv3[:, :, :HEAD_DIM].reshape(kv_rank, -1), w_kv3[:, :, HEAD_DIM:].reshape(kv_rank, -1)],
                           axis=1).astype(BF16)
    w_ob = w_o.astype(BF16)

    qlat_x, ckv_x, kr_x = _mla_down(hx, w_dn_p, q_g, kv_g, cos, sin, pair=pair, use_rope=True)
    qlat_y, ckv_y, kr_y = _mla_down(hy, w_dn_p, q_g, kv_g, cos[:lc], sin[:lc], pair=pair, use_rope=False)
    kv_all = _matmul([jnp.concatenate([ckv_x, ckv_y], axis=0)], [w_kv], out_dtype=BF16, bm=1280)
    kr_all = jnp.concatenate([kr_x, kr_y], axis=0)

    qn_x = _matmul([qlat_x], [w_qn], out_dtype=BF16, scale=scale)
    qr_x = _matmul([qlat_x], [w_qr], out_dtype=BF16, epilogue="rope", extras=(cos, sin), scale=scale, pair=pair)
    o_x = _mla_attn(qn_x, qr_x, kv_all, kr_all, n_heads=n_heads, key_rows=n + lc, key_block=0)
    x = _matmul([o_x], [w_ob], out_dtype=F32, epilogue="resid", extras=(x, gate_x), bn=512)
    if need_ctx:
        qn_y = _matmul([qlat_y], [w_qn], out_dtype=BF16, scale=scale)
        qr_y = _matmul([qlat_y], [w_qr], out_dtype=BF16, scale=scale)
        o_y = _mla_attn(qn_y, qr_y, kv_all, kr_all, n_heads=n_heads, key_rows=lc, key_block=n // lc)
        y = _matmul([o_y], [w_ob], out_dtype=F32, epilogue="resid", extras=(y, gate_y), bn=512)
    return x, y


def kernel(x, c, ctx, c_ctx, ada_w, ada_b, norm1_g, norm2_g, ev_w_in, ev_sink, ev_dw, ev_ln_g, ev_ln_b, ev_w_out, od_w_dn, od_q_norm_g, od_kv_norm_g, od_w_uq, od_w_ukv, od_w_o, moe_router, moe_w_gate, moe_w_up, moe_w_down, final_g):
    b, n, d = x.shape
    assert b == 1 and n % GRID_W == 0
    depth = ada_w.shape[0]
    xr, yr = x[0], ctx[0]
    mods = _adaln(c, c_ctx, ada_w, ada_b)
    tables_a = _rope_tables(n, HEAD_DIM)
    rope_dim = od_w_dn.shape[2] - od_q_norm_g.shape[1] - od_kv_norm_g.shape[1]
    tables_c = _rope_tables(n, rope_dim)
    wgu = jnp.concatenate([moe_w_gate, moe_w_up], axis=3).astype(BF16)
    wd = moe_w_down.astype(BF16)
    for l in range(depth):
        need_ctx = l < depth - 1
        mx = [mods[l, 0:1, i * d:(i + 1) * d] for i in range(6)]
        my = [mods[l, 1:2, i * d:(i + 1) * d] for i in range(6)]
        hx = _normmod(xr, norm1_g[l], mx[0], mx[1], BF16)
        hy = _normmod(yr, norm1_g[l], my[0], my[1], BF16)
        i = l // 2
        if l % 2 == 0:
            xr, yr = _even_mixer(hx, hy, xr, yr, mx[2], my[2], ev_w_in[i], ev_sink[i], ev_dw[i], ev_ln_g[i],
                                 ev_ln_b[i], ev_w_out[i], tables_a, need_ctx)
        else:
            xr, yr = _mla_mixer(hx, hy, xr, yr, mx[2], my[2], od_w_dn[i], od_q_norm_g[i], od_kv_norm_g[i],
                                od_w_uq[i], od_w_ukv[i], od_w_o[i], tables_c, need_ctx)
        w_rt = moe_router[l].T
        xr = _ec_moe(xr, norm2_g[l], mx[3], mx[4], mx[5], w_rt, wgu, wd, l)
        if need_ctx:
            yr = _ec_moe(yr, norm2_g[l], my[3], my[4], my[5], w_rt, wgu, wd, l)
    zero = jnp.zeros((1, d), F32)
    return _normmod(xr, final_g, zero, zero, F32)[None]
```

```python
import functools

import jax
import jax.numpy as jnp
from jax import lax
from jax.experimental import pallas as pl
from jax.experimental.pallas import tpu as pltpu

F32 = jnp.float32
BF16 = jnp.bfloat16

GRID_W = 64
ROPE_BASE = 10000.0
NORM_EPS = 1e-6
NEG_INF = -1e30
HEAD_DIM = 128
A_WINDOW = 128
EC_CAPACITY = 2
LOG2_E = 1.4426950408889634

LANES = 128
SUBLANES = 8
BF16_ROWS = 16
MIB = 1024 * 1024


def _params(n_axes, vmem_mib):
    return pltpu.CompilerParams(dimension_semantics=("arbitrary",) * n_axes, vmem_limit_bytes=vmem_mib * MIB)


def _pick(total, target, mult):
    if total <= target:
        return total
    best = None
    for d in range(mult, target + 1, mult):
        if total % d == 0:
            best = d
    assert best is not None, (total, target, mult)
    return best


def _sigmoid(v):
    return 1.0 / (1.0 + jnp.exp(-v))


def _norm_mod(xv, g, shift, scale):
    yv = xv * lax.rsqrt(jnp.mean(xv * xv, axis=-1, keepdims=True) + NORM_EPS)
    return (yv * g) * (1.0 + scale) + shift


def _rope(xv, cos, sin, pair):
    lane = lax.broadcasted_iota(jnp.int32, xv.shape, 1)
    first = (lane % (2 * pair)) < pair
    partner = jnp.where(first, pltpu.roll(xv, LANES - pair, 1), pltpu.roll(xv, pair, 1))
    return xv * cos + partner * sin


def _adaln_kernel(cc_ref, w_ref, b_ref, o_ref):
    cc = cc_ref[...]
    s = cc * _sigmoid(cc)
    w = w_ref[...]
    b = b_ref[...]
    r0 = jnp.sum(s[:, 0:1] * w, axis=0, keepdims=True) + b
    r1 = jnp.sum(s[:, 1:2] * w, axis=0, keepdims=True) + b
    o_ref[...] = jnp.concatenate([r0, r1, jnp.zeros((SUBLANES - 2, w.shape[1]), F32)], axis=0)


def _adaln(c, c_ctx, ada_w, ada_b):
    depth, d, n6 = ada_w.shape
    cc = jnp.zeros((d, SUBLANES), F32).at[:, 0].set(c[0]).at[:, 1].set(c_ctx)
    tn = _pick(n6, 512, LANES)
    return pl.pallas_call(
        _adaln_kernel,
        out_shape=jax.ShapeDtypeStruct((depth, SUBLANES, n6), F32),
        grid=(depth, n6 // tn),
        in_specs=[pl.BlockSpec((d, SUBLANES), lambda l, j: (0, 0)),
                  pl.BlockSpec((None, d, tn), lambda l, j: (l, 0, j)),
                  pl.BlockSpec((None, 1, tn), lambda l, j: (l, 0, j))],
        out_specs=pl.BlockSpec((None, SUBLANES, tn), lambda l, j: (l, 0, j)),
        compiler_params=_params(2, 48),
        name="adaln",
    )(cc, ada_w, ada_b.reshape(depth, 1, n6))


def _normmod_kernel(x_ref, g_ref, sh_ref, sc_ref, o_ref):
    o_ref[...] = _norm_mod(x_ref[...], g_ref[...], sh_ref[...], sc_ref[...]).astype(o_ref.dtype)


def _normmod(xr, g, shift, scale, out_dtype):
    r, d = xr.shape
    tr = _pick(r, 512, BF16_ROWS)
    vec = pl.BlockSpec((1, d), lambda i: (0, 0))
    return pl.pallas_call(
        _normmod_kernel,
        out_shape=jax.ShapeDtypeStruct((r, d), out_dtype),
        grid=(r // tr,),
        in_specs=[pl.BlockSpec((tr, d), lambda i: (i, 0)), vec, vec, vec],
        out_specs=pl.BlockSpec((tr, d), lambda i: (i, 0)),
        compiler_params=_params(1, 48),
        name="normmod",
    )(xr, g.reshape(1, d), shift, scale)


def _mm_kernel(*refs, n_pairs, epilogue, scale, pair):
    a_refs, w_refs, rest = refs[:n_pairs], refs[n_pairs:2 * n_pairs], refs[2 * n_pairs:]
    acc = None
    for a_ref, w_ref in zip(a_refs, w_refs):
        part = jnp.dot(a_ref[...], w_ref[...], preferred_element_type=F32)
        acc = part if acc is None else acc + part
    if epilogue == "plain":
        (o_ref,) = rest
        o_ref[...] = (acc * scale if scale != 1.0 else acc).astype(o_ref.dtype)
    elif epilogue == "rope":
        cos_ref, sin_ref, o_ref = rest
        cos, sin = cos_ref[...], sin_ref[...]
        for h in range(acc.shape[1] // LANES):
            sl = slice(h * LANES, (h + 1) * LANES)
            r = _rope(acc[:, sl], cos, sin, pair)
            o_ref[:, sl] = (r * scale if scale != 1.0 else r).astype(o_ref.dtype)
    else:
        x_ref, gate_ref, o_ref = rest
        o_ref[...] = x_ref[...] + gate_ref[...] * acc


def _matmul(a_list, w_list, *, out_dtype, epilogue="plain", extras=(), scale=1.0, pair=0, bm=1024, bn=1024):
    m = a_list[0].shape[0]
    n = w_list[0].shape[1]
    bm = _pick(m, bm, BF16_ROWS)
    bn = _pick(n, bn, LANES)
    in_specs = [pl.BlockSpec((bm, a.shape[1]), lambda j, i: (i, 0)) for a in a_list]
    in_specs += [pl.BlockSpec((w.shape[0], bn), lambda j, i: (0, j)) for w in w_list]
    if epilogue == "rope":
        in_specs += [pl.BlockSpec((bm, LANES), lambda j, i: (i, 0))] * 2
    elif epilogue == "resid":
        in_specs += [pl.BlockSpec((bm, bn), lambda j, i: (i, j)), pl.BlockSpec((1, bn), lambda j, i: (0, j))]
    return pl.pallas_call(
        functools.partial(_mm_kernel, n_pairs=len(a_list), epilogue=epilogue, scale=scale, pair=pair),
        out_shape=jax.ShapeDtypeStruct((m, n), out_dtype),
        grid=(n // bn, m // bm),
        in_specs=in_specs,
        out_specs=pl.BlockSpec((bm, bn), lambda j, i: (i, j)),
        compiler_params=_params(2, 56),
        name="mm_" + epilogue,
    )(*a_list, *w_list, *extras)


def _gqa_kernel(sink_ref, *refs, groups, tq, local, n_tokens):
    if local:
        q_ref, k_ref, v_ref, ky_ref, vy_ref, o_ref = refs
    else:
        q_ref, ky_ref, vy_ref, o_ref = refs
    hk = pl.program_id(0)
    nt = (((1,), (1,)), ((), ()))
    def with_ones(v):
        return jnp.concatenate([v, jnp.ones(v.shape, BF16)], axis=1)

    ky = ky_ref[...]
    vy = with_ones(vy_ref[...])
    if local:
        n = pl.program_id(1)
        win = tq + 2 * A_WINDOW
        start = pl.multiple_of(jnp.clip(n * tq - A_WINDOW, 0, n_tokens - win), BF16_ROWS)
        kw = k_ref[pl.ds(start, win), :]
        vw = with_ones(v_ref[pl.ds(start, win), :])
        qpos = n * tq + lax.broadcasted_iota(jnp.int32, (tq, win), 0)
        kpos = start + lax.broadcasted_iota(jnp.int32, (tq, win), 1)
        band = jnp.abs(kpos - qpos) <= A_WINDOW
    for gi in range(groups):
        sl = slice(gi * HEAD_DIM, (gi + 1) * HEAD_DIM)
        qg = q_ref[:, sl]
        sink = sink_ref[hk * groups + gi] * LOG2_E
        s_ctx = lax.dot_general(qg, ky, nt, preferred_element_type=F32)
        m = jnp.maximum(jnp.max(s_ctx, axis=1, keepdims=True), sink)
        if local:
            s_loc = jnp.where(band, lax.dot_general(qg, kw, nt, preferred_element_type=F32), NEG_INF)
            m = jnp.maximum(m, jnp.max(s_loc, axis=1, keepdims=True))
        acc = jnp.dot(jnp.exp2(s_ctx - m).astype(BF16), vy, preferred_element_type=F32)
        if local:
            acc = acc + jnp.dot(jnp.exp2(s_loc - m).astype(BF16), vw, preferred_element_type=F32)
        den = acc[:, HEAD_DIM:] + jnp.exp2(sink - m)
        o_ref[:, sl] = (acc[:, :HEAD_DIM] / den).astype(o_ref.dtype)


def _gqa(q, kx, vsrc_x, v_col0, ky, vsrc_y, sink, *, local):
    r, aq = q.shape
    hkv = ky.shape[1] // HEAD_DIM
    groups = aq // HEAD_DIM // hkv
    lc = ky.shape[0]
    gw = groups * HEAD_DIM
    tq = _pick(r, 256, BF16_ROWS)
    n_tokens = kx.shape[0] if local else 0
    if local:
        assert n_tokens >= tq + 2 * A_WINDOW
    q_spec = pl.BlockSpec((tq, gw), lambda h, i, s: (i, h))
    ctx_specs = [pl.BlockSpec((lc, HEAD_DIM), lambda h, i, s: (0, h)),
                 pl.BlockSpec((lc, HEAD_DIM), lambda h, i, s: (0, v_col0 + h))]
    if local:
        in_specs = [q_spec,
                    pl.BlockSpec((n_tokens, HEAD_DIM), lambda h, i, s: (0, h)),
                    pl.BlockSpec((n_tokens, HEAD_DIM), lambda h, i, s: (0, v_col0 + h))] + ctx_specs
        operands = (q, kx, vsrc_x, ky, vsrc_y)
    else:
        in_specs = [q_spec] + ctx_specs
        operands = (q, ky, vsrc_y)
    return pl.pallas_call(
        functools.partial(_gqa_kernel, groups=groups, tq=tq, local=local, n_tokens=n_tokens),
        out_shape=jax.ShapeDtypeStruct((r, aq), BF16),
        grid_spec=pltpu.PrefetchScalarGridSpec(
            num_scalar_prefetch=1, grid=(hkv, r // tq), in_specs=in_specs,
            out_specs=pl.BlockSpec((tq, gw), lambda h, i, s: (i, h))),
        compiler_params=_params(2, 48),
        name="gqa_local" if local else "gqa_ctx",
    )(sink, *operands)


CONV_HALO = 16
CONV_ROWS = 32
CONV_LANES = 256


def _conv_kernel(a_ref, g_ref, ap_ref, gp_ref, an_ref, gn_ref, dw_ref, lg_ref, lb_ref, o_ref, u_ref, z_ref,
                 *, taps, n_blocks):
    i = pl.program_id(0)
    tr, ch = z_ref.shape
    pad = (taps - 1) // 2

    def glu(a, g):
        return a.astype(F32) * _sigmoid(g.astype(F32))

    u_ref[0, 0:CONV_HALO, :] = glu(ap_ref[...], gp_ref[...]) * (i > 0).astype(F32)
    u_ref[0, CONV_HALO:CONV_HALO + tr, :] = glu(a_ref[...], g_ref[...])
    u_ref[0, CONV_HALO + tr:, :] = glu(an_ref[...], gn_ref[...]) * (i < n_blocks - 1).astype(F32)
    span = u_ref.shape[1] - SUBLANES
    for s in range(1, SUBLANES):
        u_ref[s, 0:span, :] = u_ref[0, s:s + span, :]

    def lane_chunk(cc, carry):
        c0 = pl.multiple_of(cc * CONV_LANES, CONV_LANES)
        for r0 in range(0, tr, CONV_ROWS):
            acc = jnp.zeros((CONV_ROWS, CONV_LANES), F32)
            for j in range(taps):
                shift = j - pad + CONV_HALO
                acc = acc + (u_ref[shift % SUBLANES, pl.ds(r0 + shift - shift % SUBLANES, CONV_ROWS),
                                   pl.ds(c0, CONV_LANES)]
                             * dw_ref[pl.ds(j, 1), pl.ds(c0, CONV_LANES)])
            z_ref[pl.ds(r0, CONV_ROWS), pl.ds(c0, CONV_LANES)] = acc
        return carry

    lax.fori_loop(0, ch // CONV_LANES, lane_chunk, 0)
    z = z_ref[...]
    mu = jnp.mean(z, axis=-1, keepdims=True)
    zc = z - mu
    var = jnp.mean(zc * zc, axis=-1, keepdims=True)
    yv = zc * lax.rsqrt(var + NORM_EPS) * lg_ref[...] + lb_ref[...]
    o_ref[...] = (yv * _sigmoid(yv)).astype(o_ref.dtype)


def _conformer_conv(agv, dw, ln_g, ln_b):
    r = agv.shape[0]
    taps, ch = dw.shape
    assert (taps - 1) // 2 < CONV_HALO and ch % CONV_LANES == 0
    tr = _pick(r, 256, CONV_ROWS)
    nb = r // tr
    hb = tr // CONV_HALO
    last = r // CONV_HALO - 1
    dwp = jnp.zeros((2 * CONV_HALO, ch), F32).at[:taps].set(dw)
    cur = lambda col: pl.BlockSpec((tr, ch), lambda i: (i, col))
    prv = lambda col: pl.BlockSpec((CONV_HALO, ch), lambda i: (jnp.maximum(i * hb - 1, 0), col))
    nxt = lambda col: pl.BlockSpec((CONV_HALO, ch), lambda i: (jnp.minimum((i + 1) * hb, last), col))
    vec = pl.BlockSpec((1, ch), lambda i: (0, 0))
    return pl.pallas_call(
        functools.partial(_conv_kernel, taps=taps, n_blocks=nb),
        out_shape=jax.ShapeDtypeStruct((r, ch), BF16),
        grid=(nb,),
        in_specs=[cur(0), cur(1), prv(0), prv(1), nxt(0), nxt(1),
                  pl.BlockSpec((2 * CONV_HALO, ch), lambda i: (0, 0)), vec, vec],
        out_specs=pl.BlockSpec((tr, ch), lambda i: (i, 0)),
        scratch_shapes=[pltpu.VMEM((SUBLANES, tr + 2 * CONV_HALO, ch), F32), pltpu.VMEM((tr, ch), F32)],
        compiler_params=_params(1, 48),
        name="conformer_conv",
    )(agv, agv, agv, agv, agv, agv, dwp, ln_g.reshape(1, ch), ln_b.reshape(1, ch))


def _mla_down_kernel(h_ref, w_ref, qg_ref, kg_ref, cos_ref, sin_ref, q_ref, kv_ref, kr_ref, *, q_rank, kv_rank,
                     pair, use_rope):
    acc = jnp.dot(h_ref[...], w_ref[...], preferred_element_type=F32)

    def rms(v, g):
        return v * lax.rsqrt(jnp.mean(v * v, axis=-1, keepdims=True) + NORM_EPS) * g

    q_ref[...] = rms(acc[:, :q_rank], qg_ref[...]).astype(q_ref.dtype)
    kv_ref[...] = rms(acc[:, q_rank:q_rank + kv_rank], kg_ref[...]).astype(kv_ref.dtype)
    kr = acc[:, q_rank + kv_rank:]
    if use_rope:
        kr = _rope(kr, cos_ref[...], sin_ref[...], pair)
    kr_ref[...] = kr.astype(kr_ref.dtype)


def _mla_down(h, w_dn_p, q_g, kv_g, cos, sin, *, pair, use_rope):
    m, d = h.shape
    q_rank, kv_rank = q_g.shape[0], kv_g.shape[0]
    wn = w_dn_p.shape[1]
    assert wn == q_rank + kv_rank + LANES
    bm = _pick(m, 512, BF16_ROWS)
    row = lambda w: pl.BlockSpec((bm, w), lambda i: (i, 0))
    return pl.pallas_call(
        functools.partial(_mla_down_kernel, q_rank=q_rank, kv_rank=kv_rank, pair=pair, use_rope=use_rope),
        out_shape=(jax.ShapeDtypeStruct((m, q_rank), BF16), jax.ShapeDtypeStruct((m, kv_rank), BF16),
                   jax.ShapeDtypeStruct((m, LANES), BF16)),
        grid=(m // bm,),
        in_specs=[row(d), pl.BlockSpec((d, wn), lambda i: (0, 0)),
                  pl.BlockSpec((1, q_rank), lambda i: (0, 0)), pl.BlockSpec((1, kv_rank), lambda i: (0, 0)),
                  row(LANES), row(LANES)],
        out_specs=(row(q_rank), row(kv_rank), row(LANES)),
        compiler_params=_params(1, 56),
        name="mla_down",
    )(h, w_dn_p, q_g.reshape(1, q_rank), kv_g.reshape(1, kv_rank), cos, sin)


MLA_TQ = 1024
MLA_SUB = 256
MLA_KEY_CHUNK = 1280
MLA_STAGES = 12


def _mla_attn_kernel(qn_ref, qr_ref, kn_ref, kr_ref, v_ref, o_ref, kcat_ref, vcat_ref, qcat_ref, s_ref, *, tk, sub):
    nk = kcat_ref.shape[0]
    tq = qn_ref.shape[0]
    n_sub = tq // sub

    @pl.when(pl.program_id(1) == 0)
    def _():
        kcat_ref[:, :HEAD_DIM] = kn_ref[...]
        kcat_ref[:, HEAD_DIM:] = kr_ref[...]
        vcat_ref[:, :HEAD_DIM] = v_ref[...]
        vcat_ref[:, HEAD_DIM:] = jnp.ones((nk, HEAD_DIM), BF16)

    qcat_ref[:, :HEAD_DIM] = qn_ref[...]
    qcat_ref[:, HEAD_DIM:] = qr_ref[...]
    nt = (((1,), (1,)), ((), ()))
    n_chunks = nk // tk

    def scores(buf, u, off):
        s_ref[buf, u] = lax.dot_general(qcat_ref[u * sub:(u + 1) * sub, :], kcat_ref[pl.ds(off, tk), :], nt,
                                        preferred_element_type=F32)

    def absorb(buf, u, off, m, acc):
        s = s_ref[buf, u]
        m_new = jnp.maximum(m, jnp.max(s, axis=1, keepdims=True))
        p = jnp.exp2(s - m_new).astype(BF16)
        acc = jnp.exp2(m - m_new) * acc + jnp.dot(p, vcat_ref[pl.ds(off, tk), :], preferred_element_type=F32)
        return m_new, acc

    def stage(buf, off, carry):
        out = []
        for u in range(n_sub):
            out.append(absorb(buf, u, off, *carry[u]))
            scores(1 - buf, u, off + tk)
        return tuple(out)

    def trip(j, carry):
        off = pl.multiple_of(j * (MLA_STAGES * tk), tk)
        for k in range(MLA_STAGES):
            carry = stage(k % 2, off + k * tk, carry)
        return carry

    for u in range(n_sub):
        scores(0, u, 0)
    carry = tuple((jnp.full((sub, 1), NEG_INF, F32), jnp.zeros((sub, 2 * HEAD_DIM), F32)) for _ in range(n_sub))
    n_stages = n_chunks - 1
    carry = lax.fori_loop(0, n_stages // MLA_STAGES, trip, carry)
    for k in range(n_stages - n_stages % MLA_STAGES, n_stages):
        carry = stage(k % 2, k * tk, carry)
    for u in range(n_sub):
        _, acc = absorb(n_stages % 2, u, n_stages * tk, *carry[u])
        o_ref[u * sub:(u + 1) * sub, :] = (acc[:, :HEAD_DIM] / acc[:, HEAD_DIM:]).astype(o_ref.dtype)


def _mla_attn(qn, qr, kv, kr, *, n_heads, key_rows, key_block):
    r = qn.shape[0]
    tq = _pick(r, MLA_TQ, MLA_SUB) if r >= MLA_SUB else r
    sub = min(MLA_SUB, tq)
    tk = _pick(key_rows, MLA_KEY_CHUNK, LANES)
    once = pl.Buffered(1)
    kspec = lambda col0: pl.BlockSpec((key_rows, HEAD_DIM), lambda h, i: (key_block, col0 + h), pipeline_mode=once)
    qspec = pl.BlockSpec((tq, HEAD_DIM), lambda h, i: (i, h))
    return pl.pallas_call(
        functools.partial(_mla_attn_kernel, tk=tk, sub=sub),
        out_shape=jax.ShapeDtypeStruct((r, n_heads * HEAD_DIM), BF16),
        grid=(n_heads, r // tq),
        in_specs=[qspec, qspec, kspec(0),
                  pl.BlockSpec((key_rows, LANES), lambda h, i: (key_block, 0), pipeline_mode=once), kspec(n_heads)],
        out_specs=pl.BlockSpec((tq, HEAD_DIM), lambda h, i: (i, h)),
        scratch_shapes=[pltpu.VMEM((key_rows, 2 * HEAD_DIM), BF16), pltpu.VMEM((key_rows, 2 * HEAD_DIM), BF16),
                        pltpu.VMEM((tq, 2 * HEAD_DIM), BF16), pltpu.VMEM((2, tq // sub, sub, tk), F32)],
        compiler_params=_params(2, 60),
        name="mla_attn",
    )(qn, qr, kv, kr, kv)


def _router_kernel(x_ref, g_ref, sh_ref, sc_ref, wt_ref, aff_ref):
    h = _norm_mod(x_ref[...], g_ref[...], sh_ref[...], sc_ref[...])
    w = wt_ref[...]
    ne = w.shape[0]
    h_hi = h.astype(BF16)
    h_lo = (h - h_hi.astype(F32)).astype(BF16)
    w_hi = w.astype(BF16)
    w_lo = (w - w_hi.astype(F32)).astype(BF16)
    nt = (((1,), (1,)), ((), ()))
    both = lax.dot_general(jnp.concatenate([w_hi, w_lo], axis=0), h_hi, nt, preferred_element_type=F32)
    logits = both[:ne] + both[ne:] + lax.dot_general(w_hi, h_lo, nt, preferred_element_type=F32)
    e = jnp.exp(logits - jnp.max(logits, axis=0, keepdims=True))
    aff_ref[...] = e / jnp.sum(e, axis=0, keepdims=True)


def _router(xr, g, shift, scale, w_router_t):
    r, d = xr.shape
    ne = w_router_t.shape[0]
    tr = _pick(r, 512, LANES)
    vec = pl.BlockSpec((1, d), lambda i: (0, 0))
    return pl.pallas_call(
        _router_kernel,
        out_shape=jax.ShapeDtypeStruct((ne, r), F32),
        grid=(r // tr,),
        in_specs=[pl.BlockSpec((tr, d), lambda i: (i, 0)), vec, vec, vec, pl.BlockSpec((ne, d), lambda i: (0, 0))],
        out_specs=pl.BlockSpec((ne, tr), lambda i: (0, i)),
        compiler_params=_params(1, 48),
        name="router",
    )(xr, g.reshape(1, d), shift, scale, w_router_t)


def _lane_cumsum(v):
    lane = lax.broadcasted_iota(jnp.int32, v.shape, 1)
    k = 1
    while k < LANES:
        v = v + jnp.where(lane >= k, pltpu.roll(v, k, 1), 0)
        k *= 2
    return v


def _select_kernel(a_ref, key_ref, *, cap, n_pad):
    a = a_ref[...]
    ne, g, _ = a.shape
    bits = lax.bitcast_convert_type(a, jnp.int32)

    def count(mask):
        return jnp.sum(jnp.sum(mask.astype(F32), axis=2, keepdims=True), axis=1, keepdims=True)

    thr = jnp.zeros((ne, 1, 1), jnp.int32)
    for b in range(30, -1, -1):
        cand = thr | (1 << b)
        thr = jnp.where(count(bits >= cand) >= cap, cand, thr)
    above = bits > thr
    tie = bits == thr
    need = cap - count(above)
    tie_i = tie.astype(jnp.int32)
    incl = _lane_cumsum(tie_i.reshape(ne * g, LANES)).reshape(ne, g, LANES)
    lower = (lax.broadcasted_iota(jnp.int32, (g, g), 1) < lax.broadcasted_iota(jnp.int32, (g, g), 0)).astype(F32)
    tok = lax.broadcasted_iota(jnp.int32, (g, LANES), 0) * LANES + lax.broadcasted_iota(jnp.int32, (g, LANES), 1)
    for e in range(ne):
        row_tot = jnp.broadcast_to(incl[e, :, LANES - 1:LANES].astype(F32), (g, LANES))
        row_off = jnp.dot(lower, row_tot, precision=lax.Precision.HIGHEST, preferred_element_type=F32)
        rank = (incl[e] - tie_i[e]).astype(F32) + row_off
        sel = above[e] | (tie[e] & (rank < need[e]))
        key_ref[e] = jnp.where(sel, tok, n_pad)


def _select(aff_t, cap):
    ne, r = aff_t.shape
    assert r % LANES == 0
    n_pad = max(r, SUBLANES * LANES)
    g = n_pad // LANES
    a3 = jnp.pad(aff_t, ((0, 0), (0, n_pad - r))).reshape(ne, g, LANES)
    keys = pl.pallas_call(
        functools.partial(_select_kernel, cap=cap, n_pad=n_pad),
        out_shape=jax.ShapeDtypeStruct((ne, g, LANES), jnp.int32),
        grid=(1,),
        in_specs=[pl.BlockSpec((ne, g, LANES), lambda i: (0, 0, 0))],
        out_specs=pl.BlockSpec((ne, g, LANES), lambda i: (0, 0, 0)),
        compiler_params=_params(1, 32),
        name="ec_select",
    )(a3)
    idx = jnp.sort(keys.reshape(ne, n_pad), axis=1)[:, :cap]
    return idx, jnp.take_along_axis(aff_t, idx, axis=1)


def _row_copy(src_hbm, buf, sem, slot, tok, r):
    return pltpu.make_async_copy(src_hbm.at[pl.ds(tok, 1), :], buf.at[slot, pl.ds(r, 1), :], sem.at[slot])


def _start_gather(idx_ref, src_hbm, buf, sem, step, slot, rows):
    for r in range(rows):
        _row_copy(src_hbm, buf, sem, slot, idx_ref[step * rows + r], r).start()


def _for_slot(slot, fn, n_slots=2):
    for s in range(n_slots):
        pl.when(slot == s)(functools.partial(fn, s))


def _wait_rows(src_hbm, buf, sem, slot, rows):
    for _ in range(rows):
        _row_copy(src_hbm, buf, sem, slot, 0, 0).wait()


def _moe_ffn_kernel(idx_ref, gate_ref, g_ref, sh_ref, sc_ref, wgu_ref, wd_ref, x_hbm, y_ref, xbuf, sem,
                    *, rows, n_steps):
    step = pl.program_id(0) * pl.num_programs(1) + pl.program_id(1)
    slot = step % 2

    @pl.when(step == 0)
    def _():
        _start_gather(idx_ref, x_hbm, xbuf, sem, 0, 0, rows)

    @pl.when(step + 1 < n_steps)
    def _():
        _for_slot(1 - slot, lambda s: _start_gather(idx_ref, x_hbm, xbuf, sem, step + 1, s, rows))

    _wait_rows(x_hbm, xbuf, sem, slot, rows)
    h = _norm_mod(xbuf[slot], g_ref[...], sh_ref[...], sc_ref[...]).astype(BF16)
    ff = wd_ref.shape[0]
    halves = []
    for hh in (h[:rows // 2], h[rows // 2:]):
        au = jnp.dot(hh, wgu_ref[...], preferred_element_type=F32)
        a, u = au[:, :ff], au[:, ff:]
        halves.append((a * _sigmoid(a) * u).astype(BF16))
    z = jnp.concatenate(halves, axis=0)
    y_ref[...] = jnp.dot(z, wd_ref[...], preferred_element_type=F32) * gate_ref[...]


def _moe_ffn(idx_flat, gate_col, xr, g, shift, scale, wgu, wd, layer, *, rows):
    r, d = xr.shape
    _, ne, ff, _ = wd.shape
    assert ff % LANES == 0
    slots = idx_flat.shape[0]
    nc = slots // ne // rows
    vec = pl.BlockSpec((1, d), lambda e, c, idx: (0, 0))
    return pl.pallas_call(
        functools.partial(_moe_ffn_kernel, rows=rows, n_steps=ne * nc),
        out_shape=jax.ShapeDtypeStruct((slots, d), F32),
        grid_spec=pltpu.PrefetchScalarGridSpec(
            num_scalar_prefetch=1, grid=(ne, nc),
            in_specs=[pl.BlockSpec((rows, 1), lambda e, c, idx: (e * nc + c, 0)), vec, vec, vec,
                      pl.BlockSpec((None, None, d, 2 * ff), lambda e, c, idx: (layer, e, 0, 0)),
                      pl.BlockSpec((None, None, ff, d), lambda e, c, idx: (layer, e, 0, 0)),
                      pl.BlockSpec(memory_space=pl.ANY)],
            out_specs=pl.BlockSpec((rows, d), lambda e, c, idx: (e * nc + c, 0)),
            scratch_shapes=[pltpu.VMEM((2, rows, d), F32), pltpu.SemaphoreType.DMA((2,))]),
        compiler_params=_params(2, 56),
        name="moe_ffn",
    )(idx_flat, gate_col, g.reshape(1, d), shift, scale, wgu, wd, xr)


MOE_ROWS = 256
COMBINE_BUFS = 3


def _moe_combine_kernel(idx_ref, y_ref, gate_ref, x_hbm, o_hbm, abuf, gsem, ssem, *, rows, n_steps, nc):
    del x_hbm
    c = pl.program_id(1)
    step = pl.program_id(0) * nc + c
    slot = step % COMBINE_BUFS
    nxt = (step + 1) % COMBINE_BUFS
    prv = (step + 2) % COMBINE_BUFS

    def scatter_copy(s, sl, r):
        return pltpu.make_async_copy(abuf.at[sl, pl.ds(r, 1), :], o_hbm.at[pl.ds(idx_ref[s * rows + r], 1), :],
                                     ssem.at[sl])

    def wait_scatter(sl):
        for _ in range(rows):
            pltpu.make_async_copy(abuf.at[sl, pl.ds(0, 1), :], o_hbm.at[pl.ds(0, 1), :], ssem.at[sl]).wait()

    def gather_next():
        _for_slot(nxt, lambda s: _start_gather(idx_ref, o_hbm, abuf, gsem, step + 1, s, rows), COMBINE_BUFS)

    @pl.when(step == 0)
    def _():
        _start_gather(idx_ref, o_hbm, abuf, gsem, 0, 0, rows)

    @pl.when(c < nc - 1)
    def _():
        @pl.when(c >= 2)
        def _():
            wait_scatter(nxt)
        gather_next()

    _wait_rows(o_hbm, abuf, gsem, slot, rows)
    abuf[slot] = abuf[slot] + gate_ref[...] * y_ref[...]

    def start_scatter(s):
        for r in range(rows):
            scatter_copy(step, s, r).start()
    _for_slot(slot, start_scatter, COMBINE_BUFS)

    @pl.when(c == nc - 1)
    def _():
        if nc >= 3:
            wait_scatter(nxt)
        if nc >= 2:
            wait_scatter(prv)
        wait_scatter(slot)

        @pl.when(step + 1 < n_steps)
        def _():
            gather_next()


def _moe_combine(idx_flat, y, gate_vec, xr, *, n_experts, rows):
    r, d = xr.shape
    slots = idx_flat.shape[0]
    nc = slots // n_experts // rows
    return pl.pallas_call(
        functools.partial(_moe_combine_kernel, rows=rows, n_steps=n_experts * nc, nc=nc),
        out_shape=jax.ShapeDtypeStruct((r, d), F32),
        grid_spec=pltpu.PrefetchScalarGridSpec(
            num_scalar_prefetch=1, grid=(n_experts, nc),
            in_specs=[pl.BlockSpec((rows, d), lambda e, c, idx: (e * nc + c, 0)),
                      pl.BlockSpec((1, d), lambda e, c, idx: (0, 0)),
                      pl.BlockSpec(memory_space=pl.ANY)],
            out_specs=pl.BlockSpec(memory_space=pl.ANY),
            scratch_shapes=[pltpu.VMEM((COMBINE_BUFS, rows, d), F32), pltpu.SemaphoreType.DMA((COMBINE_BUFS,)),
                            pltpu.SemaphoreType.DMA((COMBINE_BUFS,))]),
        input_output_aliases={3: 0},
        compiler_params=_params(2, 48),
        name="moe_combine",
    )(idx_flat, y, gate_vec, xr)


def _ec_moe(xr, g, shift, scale, out_gate, w_router_t, wgu, wd, layer):
    r = xr.shape[0]
    ne = w_router_t.shape[0]
    cap = EC_CAPACITY * r // ne
    rows = _pick(cap, MOE_ROWS, 2 * BF16_ROWS)
    aff_t = _router(xr, g, shift, scale, w_router_t)
    idx, gate = _select(aff_t, cap)
    idx_flat = idx.reshape(-1)
    y = _moe_ffn(idx_flat, gate.reshape(-1, 1), xr, g, shift, scale, wgu, wd, layer, rows=rows)
    return _moe_combine(idx_flat, y, out_gate, xr, n_experts=ne, rows=rows)


def _rope_tables(n, dim):
    half = dim // 2
    inv = ROPE_BASE ** (-jnp.arange(0, half, 2, dtype=F32) / half)
    rows = jnp.repeat(jnp.arange(n // GRID_W, dtype=F32), GRID_W)
    cols = (jnp.arange(n) % GRID_W).astype(F32)
    ar = rows[:, None] * inv[None, :]
    ac = cols[:, None] * inv[None, :]
    cos = jnp.concatenate([jnp.cos(ar), jnp.cos(ar), jnp.cos(ac), jnp.cos(ac)], axis=1)
    sin = jnp.concatenate([-jnp.sin(ar), jnp.sin(ar), -jnp.sin(ac), jnp.sin(ac)], axis=1)
    if dim < LANES:
        cos = jnp.concatenate([cos, jnp.ones((n, LANES - dim), F32)], axis=1)
        sin = jnp.concatenate([sin, jnp.zeros((n, LANES - dim), F32)], axis=1)
    return cos, sin


def _even_mixer(hx, hy, x, y, gate_x, gate_y, w_in, sink, dw, ln_g, ln_b, w_out, tables, need_ctx):
    ch = dw.shape[1]
    aq = w_out.shape[0] - ch
    akv = (w_in.shape[1] - aq - 2 * ch) // 2
    cos, sin = tables
    scale = HEAD_DIM ** -0.5 * LOG2_E
    pair = HEAD_DIM // 4
    wq = w_in[:, :aq].astype(BF16)
    wk = w_in[:, aq:aq + akv].astype(BF16)
    w_agv = jnp.concatenate([w_in[:, aq + 2 * akv:], w_in[:, aq + akv:aq + 2 * akv]], axis=1).astype(BF16)
    w_o1 = w_out[:aq].astype(BF16)
    w_o2 = w_out[aq:].astype(BF16)
    v_col0 = 2 * ch // HEAD_DIM

    qx = _matmul([hx], [wq], out_dtype=BF16, epilogue="rope", extras=(cos, sin), scale=scale, pair=pair)
    kx = _matmul([hx], [wk], out_dtype=BF16, epilogue="rope", extras=(cos, sin), pair=pair)
    agv_x = _matmul([hx], [w_agv], out_dtype=BF16)
    ky = _matmul([hy], [wk], out_dtype=BF16)
    agv_y = _matmul([hy], [w_agv], out_dtype=BF16)

    att_x = _gqa(qx, kx, agv_x, v_col0, ky, agv_y, sink, local=True)
    conv_x = _conformer_conv(agv_x, dw, ln_g, ln_b)
    x = _matmul([att_x, conv_x], [w_o1, w_o2], out_dtype=F32, epilogue="resid", extras=(x, gate_x), bn=512)
    if need_ctx:
        qy = _matmul([hy], [wq], out_dtype=BF16, scale=scale)
        att_y = _gqa(qy, None, None, v_col0, ky, agv_y, sink, local=False)
        conv_y = _conformer_conv(agv_y, dw, ln_g, ln_b)
        y = _matmul([att_y, conv_y], [w_o1, w_o2], out_dtype=F32, epilogue="resid", extras=(y, gate_y), bn=512)
    return x, y


def _mla_mixer(hx, hy, x, y, gate_x, gate_y, w_dn, q_g, kv_g, w_uq, w_ukv, w_o, tables, need_ctx):
    n, d = hx.shape
    lc = hy.shape[0]
    q_rank, kv_rank = q_g.shape[0], kv_g.shape[0]
    n_heads = w_o.shape[0] // HEAD_DIM
    rope_dim = w_dn.shape[1] - q_rank - kv_rank
    assert n % lc == 0 and rope_dim <= LANES
    cos, sin = tables
    pair = rope_dim // 4
    scale = (HEAD_DIM + rope_dim) ** -0.5 * LOG2_E
    w_dn_p = jnp.concatenate([w_dn, jnp.zeros((d, LANES - rope_dim), F32)], axis=1).astype(BF16)
    w_uq3 = w_uq.reshape(q_rank, n_heads, HEAD_DIM + rope_dim)
    w_qn = w_uq3[:, :, :HEAD_DIM].reshape(q_rank, n_heads * HEAD_DIM).astype(BF16)
    w_qr = jnp.concatenate([w_uq3[:, :, HEAD_DIM:], jnp.zeros((q_rank, n_heads, LANES - rope_dim), F32)],
                           axis=2).reshape(q_rank, n_heads * LANES).astype(BF16)
    w_kv3 = w_ukv.reshape(kv_rank, n_heads, 2 * HEAD_DIM)
    w_kv = jnp.concatenate([w_kv3[:, :, :HEAD_DIM].reshape(kv_rank, -1), w_kv3[:, :, HEAD_DIM:].reshape(kv_rank, -1)],
                           axis=1).astype(BF16)
    w_ob = w_o.astype(BF16)

    qlat_x, ckv_x, kr_x = _mla_down(hx, w_dn_p, q_g, kv_g, cos, sin, pair=pair, use_rope=True)
    qlat_y, ckv_y, kr_y = _mla_down(hy, w_dn_p, q_g, kv_g, cos[:lc], sin[:lc], pair=pair, use_rope=False)
    kv_all = _matmul([jnp.concatenate([ckv_x, ckv_y], axis=0)], [w_kv], out_dtype=BF16, bm=1280)
    kr_all = jnp.concatenate([kr_x, kr_y], axis=0)

    qn_x = _matmul([qlat_x], [w_qn], out_dtype=BF16, scale=scale)
    qr_x = _matmul([qlat_x], [w_qr], out_dtype=BF16, epilogue="rope", extras=(cos, sin), scale=scale, pair=pair)
    o_x = _mla_attn(qn_x, qr_x, kv_all, kr_all, n_heads=n_heads, key_rows=n + lc, key_block=0)
    x = _matmul([o_x], [w_ob], out_dtype=F32, epilogue="resid", extras=(x, gate_x), bn=512)
    if need_ctx:
        qn_y = _matmul([qlat_y], [w_qn], out_dtype=BF16, scale=scale)
        qr_y = _matmul([qlat_y], [w_qr], out_dtype=BF16, scale=scale)
        o_y = _mla_attn(qn_y, qr_y, kv_all, kr_all, n_heads=n_heads, key_rows=lc, key_block=n // lc)
        y = _matmul([o_y], [w_ob], out_dtype=F32, epilogue="resid", extras=(y, gate_y), bn=512)
    return x, y


def kernel(x, c, ctx, c_ctx, ada_w, ada_b, norm1_g, norm2_g, ev_w_in, ev_sink, ev_dw, ev_ln_g, ev_ln_b, ev_w_out, od_w_dn, od_q_norm_g, od_kv_norm_g, od_w_uq, od_w_ukv, od_w_o, moe_router, moe_w_gate, moe_w_up, moe_w_down, final_g):
    b, n, d = x.shape
    assert b == 1 and n % GRID_W == 0
    depth = ada_w.shape[0]
    xr, yr = x[0], ctx[0]
    mods = _adaln(c, c_ctx, ada_w, ada_b)
    tables_a = _rope_tables(n, HEAD_DIM)
    rope_dim = od_w_dn.shape[2] - od_q_norm_g.shape[1] - od_kv_norm_g.shape[1]
    tables_c = _rope_tables(n, rope_dim)
    wgu = jnp.concatenate([moe_w_gate, moe_w_up], axis=3).astype(BF16)
    wd = moe_w_down.astype(BF16)
    for l in range(depth):
        need_ctx = l < depth - 1
        mx = [mods[l, 0:1, i * d:(i + 1) * d] for i in range(6)]
        my = [mods[l, 1:2, i * d:(i + 1) * d] for i in range(6)]
        hx = _normmod(xr, norm1_g[l], mx[0], mx[1], BF16)
        hy = _normmod(yr, norm1_g[l], my[0], my[1], BF16)
        i = l // 2
        if l % 2 == 0:
            xr, yr = _even_mixer(hx, hy, xr, yr, mx[2], my[2], ev_w_in[i], ev_sink[i], ev_dw[i], ev_ln_g[i],
                                 ev_ln_b[i], ev_w_out[i], tables_a, need_ctx)
        else:
            xr, yr = _mla_mixer(hx, hy, xr, yr, mx[2], my[2], od_w_dn[i], od_q_norm_g[i], od_kv_norm_g[i],
                                od_w_uq[i], od_w_ukv[i], od_w_o[i], tables_c, need_ctx)
        w_rt = moe_router[l].T
        xr = _ec_moe(xr, norm2_g[l], mx[3], mx[4], mx[5], w_rt, wgu, wd, l)
        if need_ctx:
            yr = _ec_moe(yr, norm2_g[l], my[3], my[4], my[5], w_rt, wgu, wd, l)
    zero = jnp.zeros((1, d), F32)
    return _normmod(xr, final_g, zero, zero, F32)[None]
```

```python
import functools

import jax
import jax.numpy as jnp
from jax import lax
from jax.experimental import pallas as pl
from jax.experimental.pallas import tpu as pltpu

F32 = jnp.float32
BF16 = jnp.bfloat16

GRID_W = 64
ROPE_BASE = 10000.0
NORM_EPS = 1e-6
NEG_INF = -1e30
HEAD_DIM = 128
A_WINDOW = 128
EC_CAPACITY = 2
LOG2_E = 1.4426950408889634

LANES = 128
SUBLANES = 8
BF16_ROWS = 16
MIB = 1024 * 1024


def _params(n_axes, vmem_mib):
    return pltpu.CompilerParams(dimension_semantics=("arbitrary",) * n_axes, vmem_limit_bytes=vmem_mib * MIB)


def _pick(total, target, mult):
    if total <= target:
        return total
    best = None
    for d in range(mult, target + 1, mult):
        if total % d == 0:
            best = d
    assert best is not None, (total, target, mult)
    return best


def _sigmoid(v):
    return 1.0 / (1.0 + jnp.exp(-v))


def _norm_mod(xv, g, shift, scale):
    yv = xv * lax.rsqrt(jnp.mean(xv * xv, axis=-1, keepdims=True) + NORM_EPS)
    return (yv * g) * (1.0 + scale) + shift


def _rope(xv, cos, sin, pair):
    lane = lax.broadcasted_iota(jnp.int32, xv.shape, 1)
    first = (lane % (2 * pair)) < pair
    partner = jnp.where(first, pltpu.roll(xv, LANES - pair, 1), pltpu.roll(xv, pair, 1))
    return xv * cos + partner * sin


def _adaln_kernel(cc_ref, w_ref, b_ref, o_ref):
    cc = cc_ref[...]
    s = cc * _sigmoid(cc)
    w = w_ref[...]
    b = b_ref[...]
    r0 = jnp.sum(s[:, 0:1] * w, axis=0, keepdims=True) + b
    r1 = jnp.sum(s[:, 1:2] * w, axis=0, keepdims=True) + b
    o_ref[...] = jnp.concatenate([r0, r1, jnp.zeros((SUBLANES - 2, w.shape[1]), F32)], axis=0)


def _adaln(c, c_ctx, ada_w, ada_b):
    depth, d, n6 = ada_w.shape
    cc = jnp.zeros((d, SUBLANES), F32).at[:, 0].set(c[0]).at[:, 1].set(c_ctx)
    tn = _pick(n6, 512, LANES)
    return pl.pallas_call(
        _adaln_kernel,
        out_shape=jax.ShapeDtypeStruct((depth, SUBLANES, n6), F32),
        grid=(depth, n6 // tn),
        in_specs=[pl.BlockSpec((d, SUBLANES), lambda l, j: (0, 0)),
                  pl.BlockSpec((None, d, tn), lambda l, j: (l, 0, j)),
                  pl.BlockSpec((None, 1, tn), lambda l, j: (l, 0, j))],
        out_specs=pl.BlockSpec((None, SUBLANES, tn), lambda l, j: (l, 0, j)),
        compiler_params=_params(2, 48),
        name="adaln",
    )(cc, ada_w, ada_b.reshape(depth, 1, n6))


def _normmod_kernel(x_ref, g_ref, sh_ref, sc_ref, o_ref):
    o_ref[...] = _norm_mod(x_ref[...], g_ref[...], sh_ref[...], sc_ref[...]).astype(o_ref.dtype)


def _normmod(xr, g, shift, scale, out_dtype):
    r, d = xr.shape
    tr = _pick(r, 512, BF16_ROWS)
    vec = pl.BlockSpec((1, d), lambda i: (0, 0))
    return pl.pallas_call(
        _normmod_kernel,
        out_shape=jax.ShapeDtypeStruct((r, d), out_dtype),
        grid=(r // tr,),
        in_specs=[pl.BlockSpec((tr, d), lambda i: (i, 0)), vec, vec, vec],
        out_specs=pl.BlockSpec((tr, d), lambda i: (i, 0)),
        compiler_params=_params(1, 48),
        name="normmod",
    )(xr, g.reshape(1, d), shift, scale)


def _mm_kernel(*refs, n_pairs, epilogue, scale, pair):
    a_refs, w_refs, rest = refs[:n_pairs], refs[n_pairs:2 * n_pairs], refs[2 * n_pairs:]
    acc = None
    for a_ref, w_ref in zip(a_refs, w_refs):
        part = jnp.dot(a_ref[...], w_ref[...], preferred_element_type=F32)
        acc = part if acc is None else acc + part
    if epilogue == "plain":
        (o_ref,) = rest
        o_ref[...] = (acc * scale if scale != 1.0 else acc).astype(o_ref.dtype)
    elif epilogue == "rope":
        cos_ref, sin_ref, o_ref = rest
        cos, sin = cos_ref[...], sin_ref[...]
        for h in range(acc.shape[1] // LANES):
            sl = slice(h * LANES, (h + 1) * LANES)
            r = _rope(acc[:, sl], cos, sin, pair)
            o_ref[:, sl] = (r * scale if scale != 1.0 else r).astype(o_ref.dtype)
    else:
        x_ref, gate_ref, o_ref = rest
        o_ref[...] = x_ref[...] + gate_ref[...] * acc


def _matmul(a_list, w_list, *, out_dtype, epilogue="plain", extras=(), scale=1.0, pair=0, bm=1024, bn=1024):
    m = a_list[0].shape[0]
    n = w_list[0].shape[1]
    bm = _pick(m, bm, BF16_ROWS)
    bn = _pick(n, bn, LANES)
    in_specs = [pl.BlockSpec((bm, a.shape[1]), lambda j, i: (i, 0)) for a in a_list]
    in_specs += [pl.BlockSpec((w.shape[0], bn), lambda j, i: (0, j)) for w in w_list]
    if epilogue == "rope":
        in_specs += [pl.BlockSpec((bm, LANES), lambda j, i: (i, 0))] * 2
    elif epilogue == "resid":
        in_specs += [pl.BlockSpec((bm, bn), lambda j, i: (i, j)), pl.BlockSpec((1, bn), lambda j, i: (0, j))]
    return pl.pallas_call(
        functools.partial(_mm_kernel, n_pairs=len(a_list), epilogue=epilogue, scale=scale, pair=pair),
        out_shape=jax.ShapeDtypeStruct((m, n), out_dtype),
        grid=(n // bn, m // bm),
        in_specs=in_specs,
        out_specs=pl.BlockSpec((bm, bn), lambda j, i: (i, j)),
        compiler_params=_params(2, 56),
        name="mm_" + epilogue,
    )(*a_list, *w_list, *extras)


def _gqa_kernel(sink_ref, *refs, groups, tq, local, n_tokens):
    if local:
        q_ref, k_ref, v_ref, ky_ref, vy_ref, o_ref = refs
    else:
        q_ref, ky_ref, vy_ref, o_ref = refs
    hk = pl.program_id(0)
    nt = (((1,), (1,)), ((), ()))
    def with_ones(v):
        return jnp.concatenate([v, jnp.ones(v.shape, BF16)], axis=1)

    ky = ky_ref[...]
    vy = with_ones(vy_ref[...])
    if local:
        n = pl.program_id(1)
        win = tq + 2 * A_WINDOW
        start = pl.multiple_of(jnp.clip(n * tq - A_WINDOW, 0, n_tokens - win), BF16_ROWS)
        kw = k_ref[pl.ds(start, win), :]
        vw = with_ones(v_ref[pl.ds(start, win), :])
        qpos = n * tq + lax.broadcasted_iota(jnp.int32, (tq, win), 0)
        kpos = start + lax.broadcasted_iota(jnp.int32, (tq, win), 1)
        band = jnp.abs(kpos - qpos) <= A_WINDOW
    for gi in range(groups):
        sl = slice(gi * HEAD_DIM, (gi + 1) * HEAD_DIM)
        qg = q_ref[:, sl]
        sink = sink_ref[hk * groups + gi] * LOG2_E
        s_ctx = lax.dot_general(qg, ky, nt, preferred_element_type=F32)
        m = jnp.maximum(jnp.max(s_ctx, axis=1, keepdims=True), sink)
        if local:
            s_loc = jnp.where(band, lax.dot_general(qg, kw, nt, preferred_element_type=F32), NEG_INF)
            m = jnp.maximum(m, jnp.max(s_loc, axis=1, keepdims=True))
        acc = jnp.dot(jnp.exp2(s_ctx - m).astype(BF16), vy, preferred_element_type=F32)
        if local:
            acc = acc + jnp.dot(jnp.exp2(s_loc - m).astype(BF16), vw, preferred_element_type=F32)
        den = acc[:, HEAD_DIM:] + jnp.exp2(sink - m)
        o_ref[:, sl] = (acc[:, :HEAD_DIM] / den).astype(o_ref.dtype)


def _gqa(q, kx, vsrc_x, v_col0, ky, vsrc_y, sink, *, local):
    r, aq = q.shape
    hkv = ky.shape[1] // HEAD_DIM
    groups = aq // HEAD_DIM // hkv
    lc = ky.shape[0]
    gw = groups * HEAD_DIM
    tq = _pick(r, 256, BF16_ROWS)
    n_tokens = kx.shape[0] if local else 0
    if local:
        assert n_tokens >= tq + 2 * A_WINDOW
    q_spec = pl.BlockSpec((tq, gw), lambda h, i, s: (i, h))
    ctx_specs = [pl.BlockSpec((lc, HEAD_DIM), lambda h, i, s: (0, h)),
                 pl.BlockSpec((lc, HEAD_DIM), lambda h, i, s: (0, v_col0 + h))]
    if local:
        in_specs = [q_spec,
                    pl.BlockSpec((n_tokens, HEAD_DIM), lambda h, i, s: (0, h)),
                    pl.BlockSpec((n_tokens, HEAD_DIM), lambda h, i, s: (0, v_col0 + h))] + ctx_specs
        operands = (q, kx, vsrc_x, ky, vsrc_y)
    else:
        in_specs = [q_spec] + ctx_specs
        operands = (q, ky, vsrc_y)
    return pl.pallas_call(
        functools.partial(_gqa_kernel, groups=groups, tq=tq, local=local, n_tokens=n_tokens),
        out_shape=jax.ShapeDtypeStruct((r, aq), BF16),
        grid_spec=pltpu.PrefetchScalarGridSpec(
            num_scalar_prefetch=1, grid=(hkv, r // tq), in_specs=in_specs,
            out_specs=pl.BlockSpec((tq, gw), lambda h, i, s: (i, h))),
        compiler_params=_params(2, 48),
        name="gqa_local" if local else "gqa_ctx",
    )(sink, *operands)


CONV_HALO = 16
CONV_ROWS = 32
CONV_LANES = 256


def _conv_kernel(a_ref, g_ref, ap_ref, gp_ref, an_ref, gn_ref, dw_ref, lg_ref, lb_ref, o_ref, u_ref, z_ref,
                 *, taps, n_blocks):
    i = pl.program_id(0)
    tr, ch = z_ref.shape
    pad = (taps - 1) // 2

    def glu(a, g):
        return a.astype(F32) * _sigmoid(g.astype(F32))

    u_ref[0, 0:CONV_HALO, :] = glu(ap_ref[...], gp_ref[...]) * (i > 0).astype(F32)
    u_ref[0, CONV_HALO:CONV_HALO + tr, :] = glu(a_ref[...], g_ref[...])
    u_ref[0, CONV_HALO + tr:, :] = glu(an_ref[...], gn_ref[...]) * (i < n_blocks - 1).astype(F32)
    span = u_ref.shape[1] - SUBLANES
    for s in range(1, SUBLANES):
        u_ref[s, 0:span, :] = u_ref[0, s:s + span, :]

    def lane_chunk(cc, carry):
        c0 = pl.multiple_of(cc * CONV_LANES, CONV_LANES)
        for r0 in range(0, tr, CONV_ROWS):
            acc = jnp.zeros((CONV_ROWS, CONV_LANES), F32)
            for j in range(taps):
                shift = j - pad + CONV_HALO
                acc = acc + (u_ref[shift % SUBLANES, pl.ds(r0 + shift - shift % SUBLANES, CONV_ROWS),
                                   pl.ds(c0, CONV_LANES)]
                             * dw_ref[pl.ds(j, 1), pl.ds(c0, CONV_LANES)])
            z_ref[pl.ds(r0, CONV_ROWS), pl.ds(c0, CONV_LANES)] = acc
        return carry

    lax.fori_loop(0, ch // CONV_LANES, lane_chunk, 0)
    z = z_ref[...]
    mu = jnp.mean(z, axis=-1, keepdims=True)
    zc = z - mu
    var = jnp.mean(zc * zc, axis=-1, keepdims=True)
    yv = zc * lax.rsqrt(var + NORM_EPS) * lg_ref[...] + lb_ref[...]
    o_ref[...] = (yv * _sigmoid(yv)).astype(o_ref.dtype)


def _conformer_conv(agv, dw, ln_g, ln_b):
    r = agv.shape[0]
    taps, ch = dw.shape
    assert (taps - 1) // 2 < CONV_HALO and ch % CONV_LANES == 0
    tr = _pick(r, 256, CONV_ROWS)
    nb = r // tr
    hb = tr // CONV_HALO
    last = r // CONV_HALO - 1
    dwp = jnp.zeros((2 * CONV_HALO, ch), F32).at[:taps].set(dw)
    cur = lambda col: pl.BlockSpec((tr, ch), lambda i: (i, col))
    prv = lambda col: pl.BlockSpec((CONV_HALO, ch), lambda i: (jnp.maximum(i * hb - 1, 0), col))
    nxt = lambda col: pl.BlockSpec((CONV_HALO, ch), lambda i: (jnp.minimum((i + 1) * hb, last), col))
    vec = pl.BlockSpec((1, ch), lambda i: (0, 0))
    return pl.pallas_call(
        functools.partial(_conv_kernel, taps=taps, n_blocks=nb),
        out_shape=jax.ShapeDtypeStruct((r, ch), BF16),
        grid=(nb,),
        in_specs=[cur(0), cur(1), prv(0), prv(1), nxt(0), nxt(1),
                  pl.BlockSpec((2 * CONV_HALO, ch), lambda i: (0, 0)), vec, vec],
        out_specs=pl.BlockSpec((tr, ch), lambda i: (i, 0)),
        scratch_shapes=[pltpu.VMEM((SUBLANES, tr + 2 * CONV_HALO, ch), F32), pltpu.VMEM((tr, ch), F32)],
        compiler_params=_params(1, 48),
        name="conformer_conv",
    )(agv, agv, agv, agv, agv, agv, dwp, ln_g.reshape(1, ch), ln_b.reshape(1, ch))


def _mla_down_kernel(h_ref, w_ref, qg_ref, kg_ref, cos_ref, sin_ref, q_ref, kv_ref, kr_ref, *, q_rank, kv_rank,
                     pair, use_rope):
    acc = jnp.dot(h_ref[...], w_ref[...], preferred_element_type=F32)

    def rms(v, g):
        return v * lax.rsqrt(jnp.mean(v * v, axis=-1, keepdims=True) + NORM_EPS) * g

    q_ref[...] = rms(acc[:, :q_rank], qg_ref[...]).astype(q_ref.dtype)
    kv_ref[...] = rms(acc[:, q_rank:q_rank + kv_rank], kg_ref[...]).astype(kv_ref.dtype)
    kr = acc[:, q_rank + kv_rank:]
    if use_rope:
        kr = _rope(kr, cos_ref[...], sin_ref[...], pair)
    kr_ref[...] = kr.astype(kr_ref.dtype)


def _mla_down(h, w_dn_p, q_g, kv_g, cos, sin, *, pair, use_rope):
    m, d = h.shape
    q_rank, kv_rank = q_g.shape[0], kv_g.shape[0]
    wn = w_dn_p.shape[1]
    assert wn == q_rank + kv_rank + LANES
    bm = _pick(m, 512, BF16_ROWS)
    row = lambda w: pl.BlockSpec((bm, w), lambda i: (i, 0))
    return pl.pallas_call(
        functools.partial(_mla_down_kernel, q_rank=q_rank, kv_rank=kv_rank, pair=pair, use_rope=use_rope),
        out_shape=(jax.ShapeDtypeStruct((m, q_rank), BF16), jax.ShapeDtypeStruct((m, kv_rank), BF16),
                   jax.ShapeDtypeStruct((m, LANES), BF16)),
        grid=(m // bm,),
        in_specs=[row(d), pl.BlockSpec((d, wn), lambda i: (0, 0)),
                  pl.BlockSpec((1, q_rank), lambda i: (0, 0)), pl.BlockSpec((1, kv_rank), lambda i: (0, 0)),
                  row(LANES), row(LANES)],
        out_specs=(row(q_rank), row(kv_rank), row(LANES)),
        compiler_params=_params(1, 56),
        name="mla_down",
    )(h, w_dn_p, q_g.reshape(1, q_rank), kv_g.reshape(1, kv_rank), cos, sin)


MLA_TQ = 1024
MLA_SUB = 256
MLA_KEY_CHUNK = 1280
MLA_STAGES = 12


def _mla_attn_kernel(qn_ref, qr_ref, kn_ref, kr_ref, v_ref, o_ref, kcat_ref, vcat_ref, qcat_ref, s_ref, *, tk, sub):
    nk = kcat_ref.shape[0]
    tq = qn_ref.shape[0]
    n_sub = tq // sub

    @pl.when(pl.program_id(1) == 0)
    def _():
        kcat_ref[:, :HEAD_DIM] = kn_ref[...]
        kcat_ref[:, HEAD_DIM:] = kr_ref[...]
        vcat_ref[:, :HEAD_DIM] = v_ref[...]
        vcat_ref[:, HEAD_DIM:] = jnp.ones((nk, HEAD_DIM), BF16)

    qcat_ref[:, :HEAD_DIM] = qn_ref[...]
    qcat_ref[:, HEAD_DIM:] = qr_ref[...]
    nt = (((1,), (1,)), ((), ()))
    n_chunks = nk // tk

    def scores(buf, u, off):
        s_ref[buf, u] = lax.dot_general(qcat_ref[u * sub:(u + 1) * sub, :], kcat_ref[pl.ds(off, tk), :], nt,
                                        preferred_element_type=F32)

    def absorb(buf, u, off, m, acc):
        s = s_ref[buf, u]
        m_new = jnp.maximum(m, jnp.max(s, axis=1, keepdims=True))
        p = jnp.exp2(s - m_new).astype(BF16)
        acc = jnp.exp2(m - m_new) * acc + jnp.dot(p, vcat_ref[pl.ds(off, tk), :], preferred_element_type=F32)
        return m_new, acc

    def stage(buf, off, carry):
        out = []
        for u in range(n_sub):
            out.append(absorb(buf, u, off, *carry[u]))
            scores(1 - buf, u, off + tk)
        return tuple(out)

    def trip(j, carry):
        off = pl.multiple_of(j * (MLA_STAGES * tk), tk)
        for k in range(MLA_STAGES):
            carry = stage(k % 2, off + k * tk, carry)
        return carry

    for u in range(n_sub):
        scores(0, u, 0)
    carry = tuple((jnp.full((sub, 1), NEG_INF, F32), jnp.zeros((sub, 2 * HEAD_DIM), F32)) for _ in range(n_sub))
    n_stages = n_chunks - 1
    carry = lax.fori_loop(0, n_stages // MLA_STAGES, trip, carry)
    for k in range(n_stages - n_stages % MLA_STAGES, n_stages):
        carry = stage(k % 2, k * tk, carry)
    for u in range(n_sub):
        _, acc = absorb(n_stages % 2, u, n_stages * tk, *carry[u])
        o_ref[u * sub:(u + 1) * sub, :] = (acc[:, :HEAD_DIM] / acc[:, HEAD_DIM:]).astype(o_ref.dtype)


def _mla_attn(qn, qr, kv, kr, *, n_heads, key_rows, key_block):
    r = qn.shape[0]
    tq = _pick(r, MLA_TQ, MLA_SUB) if r >= MLA_SUB else r
    sub = min(MLA_SUB, tq)
    tk = _pick(key_rows, MLA_KEY_CHUNK, LANES)
    once = pl.Buffered(1)
    kspec = lambda col0: pl.BlockSpec((key_rows, HEAD_DIM), lambda h, i: (key_block, col0 + h), pipeline_mode=once)
    qspec = pl.BlockSpec((tq, HEAD_DIM), lambda h, i: (i, h))
    return pl.pallas_call(
        functools.partial(_mla_attn_kernel, tk=tk, sub=sub),
        out_shape=jax.ShapeDtypeStruct((r, n_heads * HEAD_DIM), BF16),
        grid=(n_heads, r // tq),
        in_specs=[qspec, qspec, kspec(0),
                  pl.BlockSpec((key_rows, LANES), lambda h, i: (key_block, 0), pipeline_mode=once), kspec(n_heads)],
        out_specs=pl.BlockSpec((tq, HEAD_DIM), lambda h, i: (i, h)),
        scratch_shapes=[pltpu.VMEM((key_rows, 2 * HEAD_DIM), BF16), pltpu.VMEM((key_rows, 2 * HEAD_DIM), BF16),
                        pltpu.VMEM((tq, 2 * HEAD_DIM), BF16), pltpu.VMEM((2, tq // sub, sub, tk), F32)],
        compiler_params=_params(2, 60),
        name="mla_attn",
    )(qn, qr, kv, kr, kv)


def _router_kernel(x_ref, g_ref, sh_ref, sc_ref, wt_ref, aff_ref):
    h = _norm_mod(x_ref[...], g_ref[...], sh_ref[...], sc_ref[...])
    w = wt_ref[...]
    ne = w.shape[0]
    h_hi = h.astype(BF16)
    h_lo = (h - h_hi.astype(F32)).astype(BF16)
    w_hi = w.astype(BF16)
    w_lo = (w - w_hi.astype(F32)).astype(BF16)
    nt = (((1,), (1,)), ((), ()))
    both = lax.dot_general(jnp.concatenate([w_hi, w_lo], axis=0), h_hi, nt, preferred_element_type=F32)
    logits = both[:ne] + both[ne:] + lax.dot_general(w_hi, h_lo, nt, preferred_element_type=F32)
    e = jnp.exp(logits - jnp.max(logits, axis=0, keepdims=True))
    aff_ref[...] = e / jnp.sum(e, axis=0, keepdims=True)


def _router(xr, g, shift, scale, w_router_t):
    r, d = xr.shape
    ne = w_router_t.shape[0]
    tr = _pick(r, 512, LANES)
    vec = pl.BlockSpec((1, d), lambda i: (0, 0))
    return pl.pallas_call(
        _router_kernel,
        out_shape=jax.ShapeDtypeStruct((ne, r), F32),
        grid=(r // tr,),
        in_specs=[pl.BlockSpec((tr, d), lambda i: (i, 0)), vec, vec, vec, pl.BlockSpec((ne, d), lambda i: (0, 0))],
        out_specs=pl.BlockSpec((ne, tr), lambda i: (0, i)),
        compiler_params=_params(1, 48),
        name="router",
    )(xr, g.reshape(1, d), shift, scale, w_router_t)


def _lane_cumsum(v):
    lane = lax.broadcasted_iota(jnp.int32, v.shape, 1)
    k = 1
    while k < LANES:
        v = v + jnp.where(lane >= k, pltpu.roll(v, k, 1), 0)
        k *= 2
    return v


def _select_kernel(a_ref, idx_ref, thr_ref, *, cap):
    a = a_ref[...]
    ne, g, _ = a.shape
    n_slots = idx_ref.shape[2]
    bits = lax.bitcast_convert_type(a, jnp.int32)

    def count(mask):
        return jnp.sum(jnp.sum(mask.astype(F32), axis=2, keepdims=True), axis=1, keepdims=True)

    thr = jnp.zeros((ne, 1, 1), jnp.int32)
    for b in range(30, -1, -1):
        cand = thr | (1 << b)
        thr = jnp.where(count(bits >= cand) >= cap, cand, thr)
    thr_ref[...] = jnp.broadcast_to(thr, thr_ref.shape)
    before = lax.broadcasted_iota(jnp.int32, (g, g), 1) < lax.broadcasted_iota(jnp.int32, (g, g), 0)
    upto = lax.broadcasted_iota(jnp.int32, (g, g), 1) <= lax.broadcasted_iota(jnp.int32, (g, g), 0)
    row_id = lax.broadcasted_iota(jnp.int32, (g, n_slots), 0).astype(F32)
    slot = lax.broadcasted_iota(jnp.int32, (1, n_slots), 1).astype(F32)

    def over_rows(tri, per_row):
        return jnp.dot(tri.astype(F32), jnp.broadcast_to(per_row, (g, LANES)), precision=lax.Precision.HIGHEST,
                       preferred_element_type=F32)[:, 0:1]

    def per_expert(e, carry):
        bits_e = lax.bitcast_convert_type(a_ref[e], jnp.int32)
        thr_e = thr_ref[e][0:1, 0:1]
        above = bits_e > thr_e
        tie = bits_e == thr_e
        need = cap - jnp.sum(jnp.sum(above.astype(F32), axis=1, keepdims=True), axis=0, keepdims=True)
        tie_i = tie.astype(jnp.int32)
        incl = _lane_cumsum(tie_i)
        rank = (incl - tie_i).astype(F32) + over_rows(before, incl[:, LANES - 1:LANES].astype(F32))
        sel = above | (tie & (rank < need))
        cum = _lane_cumsum(sel.astype(jnp.int32)).astype(F32)
        row_cnt = cum[:, LANES - 1:LANES]
        row_end = over_rows(upto, row_cnt)
        g_of = jnp.sum((row_end <= slot).astype(F32), axis=0, keepdims=True)
        onehot = (row_id == g_of).astype(F32)
        start_of = jnp.sum(onehot * (row_end - row_cnt), axis=0, keepdims=True)
        cum_of = lax.dot_general(cum, onehot, (((0,), (0,)), ((), ())), preferred_element_type=F32)
        lane_of = jnp.sum((cum_of <= slot - start_of).astype(F32), axis=0, keepdims=True)
        idx_ref[e] = (g_of * LANES + lane_of).astype(jnp.int32)
        return carry

    lax.fori_loop(0, ne, per_expert, 0)


def _select(aff_t, cap):
    ne, r = aff_t.shape
    assert r % LANES == 0
    n_pad = max(r, SUBLANES * LANES)
    g = n_pad // LANES
    a3 = jnp.pad(aff_t, ((0, 0), (0, n_pad - r))).reshape(ne, g, LANES)
    n_slots = max(cap, LANES)
    idx = pl.pallas_call(
        functools.partial(_select_kernel, cap=cap),
        out_shape=jax.ShapeDtypeStruct((ne, 1, n_slots), jnp.int32),
        grid=(1,),
        in_specs=[pl.BlockSpec((ne, g, LANES), lambda i: (0, 0, 0))],
        out_specs=pl.BlockSpec((ne, 1, n_slots), lambda i: (0, 0, 0)),
        scratch_shapes=[pltpu.VMEM((ne, SUBLANES, LANES), jnp.int32)],
        compiler_params=_params(1, 32),
        name="ec_select",
    )(a3)[:, 0, :cap]
    return idx, jnp.take_along_axis(aff_t, idx, axis=1)


def _row_copy(src_hbm, buf, sem, slot, tok, r):
    return pltpu.make_async_copy(src_hbm.at[pl.ds(tok, 1), :], buf.at[slot, pl.ds(r, 1), :], sem.at[slot])


def _start_gather(idx_ref, src_hbm, buf, sem, step, slot, rows):
    for r in range(rows):
        _row_copy(src_hbm, buf, sem, slot, idx_ref[step * rows + r], r).start()


def _for_slot(slot, fn, n_slots=2):
    for s in range(n_slots):
        pl.when(slot == s)(functools.partial(fn, s))


def _wait_rows(src_hbm, buf, sem, slot, rows):
    for _ in range(rows):
        _row_copy(src_hbm, buf, sem, slot, 0, 0).wait()


def _moe_ffn_kernel(idx_ref, gate_ref, g_ref, sh_ref, sc_ref, wgu_ref, wd_ref, x_hbm, y_ref, xbuf, sem,
                    *, rows, n_steps):
    step = pl.program_id(0) * pl.num_programs(1) + pl.program_id(1)
    slot = step % 2

    @pl.when(step == 0)
    def _():
        _start_gather(idx_ref, x_hbm, xbuf, sem, 0, 0, rows)

    @pl.when(step + 1 < n_steps)
    def _():
        _for_slot(1 - slot, lambda s: _start_gather(idx_ref, x_hbm, xbuf, sem, step + 1, s, rows))

    _wait_rows(x_hbm, xbuf, sem, slot, rows)
    h = _norm_mod(xbuf[slot], g_ref[...], sh_ref[...], sc_ref[...]).astype(BF16)
    ff = wd_ref.shape[0]
    halves = []
    for hh in (h[:rows // 2], h[rows // 2:]):
        au = jnp.dot(hh, wgu_ref[...], preferred_element_type=F32)
        a, u = au[:, :ff], au[:, ff:]
        halves.append((a * _sigmoid(a) * u).astype(BF16))
    z = jnp.concatenate(halves, axis=0)
    y_ref[...] = jnp.dot(z, wd_ref[...], preferred_element_type=F32) * gate_ref[...]


def _moe_ffn(idx_flat, gate_col, xr, g, shift, scale, wgu, wd, layer, *, rows):
    r, d = xr.shape
    _, ne, ff, _ = wd.shape
    assert ff % LANES == 0
    slots = idx_flat.shape[0]
    nc = slots // ne // rows
    vec = pl.BlockSpec((1, d), lambda e, c, idx: (0, 0))
    return pl.pallas_call(
        functools.partial(_moe_ffn_kernel, rows=rows, n_steps=ne * nc),
        out_shape=jax.ShapeDtypeStruct((slots, d), F32),
        grid_spec=pltpu.PrefetchScalarGridSpec(
            num_scalar_prefetch=1, grid=(ne, nc),
            in_specs=[pl.BlockSpec((rows, 1), lambda e, c, idx: (e * nc + c, 0)), vec, vec, vec,
                      pl.BlockSpec((None, None, d, 2 * ff), lambda e, c, idx: (layer, e, 0, 0)),
                      pl.BlockSpec((None, None, ff, d), lambda e, c, idx: (layer, e, 0, 0)),
                      pl.BlockSpec(memory_space=pl.ANY)],
            out_specs=pl.BlockSpec((rows, d), lambda e, c, idx: (e * nc + c, 0)),
            scratch_shapes=[pltpu.VMEM((2, rows, d), F32), pltpu.SemaphoreType.DMA((2,))]),
        compiler_params=_params(2, 56),
        name="moe_ffn",
    )(idx_flat, gate_col, g.reshape(1, d), shift, scale, wgu, wd, xr)


MOE_ROWS = 256
COMBINE_BUFS = 3


def _moe_combine_kernel(idx_ref, y_ref, gate_ref, x_hbm, o_hbm, abuf, gsem, ssem, *, rows, n_steps, nc):
    del x_hbm
    c = pl.program_id(1)
    step = pl.program_id(0) * nc + c
    slot = step % COMBINE_BUFS
    nxt = (step + 1) % COMBINE_BUFS
    prv = (step + 2) % COMBINE_BUFS

    def scatter_copy(s, sl, r):
        return pltpu.make_async_copy(abuf.at[sl, pl.ds(r, 1), :], o_hbm.at[pl.ds(idx_ref[s * rows + r], 1), :],
                                     ssem.at[sl])

    def wait_scatter(sl):
        for _ in range(rows):
            pltpu.make_async_copy(abuf.at[sl, pl.ds(0, 1), :], o_hbm.at[pl.ds(0, 1), :], ssem.at[sl]).wait()

    def gather_next():
        _for_slot(nxt, lambda s: _start_gather(idx_ref, o_hbm, abuf, gsem, step + 1, s, rows), COMBINE_BUFS)

    @pl.when(step == 0)
    def _():
        _start_gather(idx_ref, o_hbm, abuf, gsem, 0, 0, rows)

    @pl.when(c < nc - 1)
    def _():
        @pl.when(c >= 2)
        def _():
            wait_scatter(nxt)
        gather_next()

    _wait_rows(o_hbm, abuf, gsem, slot, rows)
    abuf[slot] = abuf[slot] + gate_ref[...] * y_ref[...]

    def start_scatter(s):
        for r in range(rows):
            scatter_copy(step, s, r).start()
    _for_slot(slot, start_scatter, COMBINE_BUFS)

    @pl.when(c == nc - 1)
    def _():
        if nc >= 3:
            wait_scatter(nxt)
        if nc >= 2:
            wait_scatter(prv)
        wait_scatter(slot)

        @pl.when(step + 1 < n_steps)
        def _():
            gather_next()


def _moe_combine(idx_flat, y, gate_vec, xr, *, n_experts, rows):
    r, d = xr.shape
    slots = idx_flat.shape[0]
    nc = slots // n_experts // rows
    return pl.pallas_call(
        functools.partial(_moe_combine_kernel, rows=rows, n_steps=n_experts * nc, nc=nc),
        out_shape=jax.ShapeDtypeStruct((r, d), F32),
        grid_spec=pltpu.PrefetchScalarGridSpec(
            num_scalar_prefetch=1, grid=(n_experts, nc),
            in_specs=[pl.BlockSpec((rows, d), lambda e, c, idx: (e * nc + c, 0)),
                      pl.BlockSpec((1, d), lambda e, c, idx: (0, 0)),
                      pl.BlockSpec(memory_space=pl.ANY)],
            out_specs=pl.BlockSpec(memory_space=pl.ANY),
            scratch_shapes=[pltpu.VMEM((COMBINE_BUFS, rows, d), F32), pltpu.SemaphoreType.DMA((COMBINE_BUFS,)),
                            pltpu.SemaphoreType.DMA((COMBINE_BUFS,))]),
        input_output_aliases={3: 0},
        compiler_params=_params(2, 48),
        name="moe_combine",
    )(idx_flat, y, gate_vec, xr)


def _ec_moe(xr, g, shift, scale, out_gate, w_router_t, wgu, wd, layer):
    r = xr.shape[0]
    ne = w_router_t.shape[0]
    cap = EC_CAPACITY * r // ne
    rows = _pick(cap, MOE_ROWS, 2 * BF16_ROWS)
    aff_t = _router(xr, g, shift, scale, w_router_t)
    idx, gate = _select(aff_t, cap)
    idx_flat = idx.reshape(-1)
    y = _moe_ffn(idx_flat, gate.reshape(-1, 1), xr, g, shift, scale, wgu, wd, layer, rows=rows)
    return _moe_combine(idx_flat, y, out_gate, xr, n_experts=ne, rows=rows)


def _rope_tables(n, dim):
    half = dim // 2
    inv = ROPE_BASE ** (-jnp.arange(0, half, 2, dtype=F32) / half)
    rows = jnp.repeat(jnp.arange(n // GRID_W, dtype=F32), GRID_W)
    cols = (jnp.arange(n) % GRID_W).astype(F32)
    ar = rows[:, None] * inv[None, :]
    ac = cols[:, None] * inv[None, :]
    cos = jnp.concatenate([jnp.cos(ar), jnp.cos(ar), jnp.cos(ac), jnp.cos(ac)], axis=1)
    sin = jnp.concatenate([-jnp.sin(ar), jnp.sin(ar), -jnp.sin(ac), jnp.sin(ac)], axis=1)
    if dim < LANES:
        cos = jnp.concatenate([cos, jnp.ones((n, LANES - dim), F32)], axis=1)
        sin = jnp.concatenate([sin, jnp.zeros((n, LANES - dim), F32)], axis=1)
    return cos, sin


def _even_mixer(hx, hy, x, y, gate_x, gate_y, w_in, sink, dw, ln_g, ln_b, w_out, tables, need_ctx):
    ch = dw.shape[1]
    aq = w_out.shape[0] - ch
    akv = (w_in.shape[1] - aq - 2 * ch) // 2
    cos, sin = tables
    scale = HEAD_DIM ** -0.5 * LOG2_E
    pair = HEAD_DIM // 4
    wq = w_in[:, :aq].astype(BF16)
    wk = w_in[:, aq:aq + akv].astype(BF16)
    w_agv = jnp.concatenate([w_in[:, aq + 2 * akv:], w_in[:, aq + akv:aq + 2 * akv]], axis=1).astype(BF16)
    w_o1 = w_out[:aq].astype(BF16)
    w_o2 = w_out[aq:].astype(BF16)
    v_col0 = 2 * ch // HEAD_DIM

    qx = _matmul([hx], [wq], out_dtype=BF16, epilogue="rope", extras=(cos, sin), scale=scale, pair=pair)
    kx = _matmul([hx], [wk], out_dtype=BF16, epilogue="rope", extras=(cos, sin), pair=pair)
    agv_x = _matmul([hx], [w_agv], out_dtype=BF16)
    ky = _matmul([hy], [wk], out_dtype=BF16)
    agv_y = _matmul([hy], [w_agv], out_dtype=BF16)

    att_x = _gqa(qx, kx, agv_x, v_col0, ky, agv_y, sink, local=True)
    conv_x = _conformer_conv(agv_x, dw, ln_g, ln_b)
    x = _matmul([att_x, conv_x], [w_o1, w_o2], out_dtype=F32, epilogue="resid", extras=(x, gate_x), bn=512)
    if need_ctx:
        qy = _matmul([hy], [wq], out_dtype=BF16, scale=scale)
        att_y = _gqa(qy, None, None, v_col0, ky, agv_y, sink, local=False)
        conv_y = _conformer_conv(agv_y, dw, ln_g, ln_b)
        y = _matmul([att_y, conv_y], [w_o1, w_o2], out_dtype=F32, epilogue="resid", extras=(y, gate_y), bn=512)
    return x, y


def _mla_mixer(hx, hy, x, y, gate_x, gate_y, w_dn, q_g, kv_g, w_uq, w_ukv, w_o, tables, need_ctx):
    n, d = hx.shape
    lc = hy.shape[0]
    q_rank, kv_rank = q_g.shape[0], kv_g.shape[0]
    n_heads = w_o.shape[0] // HEAD_DIM
    rope_dim = w_dn.shape[1] - q_rank - kv_rank
    assert n % lc == 0 and rope_dim <= LANES
    cos, sin = tables
    pair = rope_dim // 4
    scale = (HEAD_DIM + rope_dim) ** -0.5 * LOG2_E
    w_dn_p = jnp.concatenate([w_dn, jnp.zeros((d, LANES - rope_dim), F32)], axis=1).astype(BF16)
    w_uq3 = w_uq.reshape(q_rank, n_heads, HEAD_DIM + rope_dim)
    w_qn = w_uq3[:, :, :HEAD_DIM].reshape(q_rank, n_heads * HEAD_DIM).astype(BF16)
    w_qr = jnp.concatenate([w_uq3[:, :, HEAD_DIM:], jnp.zeros((q_rank, n_heads, LANES - rope_dim), F32)],
                           axis=2).reshape(q_rank, n_heads * LANES).astype(BF16)
    w_kv3 = w_ukv.reshape(kv_rank, n_heads, 2 * HEAD_DIM)
    w_kv = jnp.concatenate([w_kv3[:, :, :HEAD_DIM].reshape(kv_rank, -1), w_kv3[:, :, HEAD_DIM:].reshape(kv_rank, -1)],
                           axis=1).astype(BF16)
    w_ob = w_o.astype(BF16)

    qlat_x, ckv_x, kr_x = _mla_down(hx, w_dn_p, q_g, kv_g, cos, sin, pair=pair, use_rope=True)
    qlat_y, ckv_y, kr_y = _mla_down(hy, w_dn_p, q_g, kv_g, cos[:lc], sin[:lc], pair=pair, use_rope=False)
    kv_all = _matmul([jnp.concatenate([ckv_x, ckv_y], axis=0)], [w_kv], out_dtype=BF16, bm=1280)
    kr_all = jnp.concatenate([kr_x, kr_y], axis=0)

    qn_x = _matmul([qlat_x], [w_qn], out_dtype=BF16, scale=scale)
    qr_x = _matmul([qlat_x], [w_qr], out_dtype=BF16, epilogue="rope", extras=(cos, sin), scale=scale, pair=pair)
    o_x = _mla_attn(qn_x, qr_x, kv_all, kr_all, n_heads=n_heads, key_rows=n + lc, key_block=0)
    x = _matmul([o_x], [w_ob], out_dtype=F32, epilogue="resid", extras=(x, gate_x), bn=512)
    if need_ctx:
        qn_y = _matmul([qlat_y], [w_qn], out_dtype=BF16, scale=scale)
        qr_y = _matmul([qlat_y], [w_qr], out_dtype=BF16, scale=scale)
        o_y = _mla_attn(qn_y, qr_y, kv_all, kr_all, n_heads=n_heads, key_rows=lc, key_block=n // lc)
        y = _matmul([o_y], [w_ob], out_dtype=F32, epilogue="resid", extras=(y, gate_y), bn=512)
    return x, y


def kernel(x, c, ctx, c_ctx, ada_w, ada_b, norm1_g, norm2_g, ev_w_in, ev_sink, ev_dw, ev_ln_g, ev_ln_b, ev_w_out, od_w_dn, od_q_norm_g, od_kv_norm_g, od_w_uq, od_w_ukv, od_w_o, moe_router, moe_w_gate, moe_w_up, moe_w_down, final_g):
    b, n, d = x.shape
    assert b == 1 and n % GRID_W == 0
    depth = ada_w.shape[0]
    xr, yr = x[0], ctx[0]
    mods = _adaln(c, c_ctx, ada_w, ada_b)
    tables_a = _rope_tables(n, HEAD_DIM)
    rope_dim = od_w_dn.shape[2] - od_q_norm_g.shape[1] - od_kv_norm_g.shape[1]
    tables_c = _rope_tables(n, rope_dim)
    wgu = jnp.concatenate([moe_w_gate, moe_w_up], axis=3).astype(BF16)
    wd = moe_w_down.astype(BF16)
    for l in range(depth):
        need_ctx = l < depth - 1
        mx = [mods[l, 0:1, i * d:(i + 1) * d] for i in range(6)]
        my = [mods[l, 1:2, i * d:(i + 1) * d] for i in range(6)]
        hx = _normmod(xr, norm1_g[l], mx[0], mx[1], BF16)
        hy = _normmod(yr, norm1_g[l], my[0], my[1], BF16)
        i = l // 2
        if l % 2 == 0:
            xr, yr = _even_mixer(hx, hy, xr, yr, mx[2], my[2], ev_w_in[i], ev_sink[i], ev_dw[i], ev_ln_g[i],
                                 ev_ln_b[i], ev_w_out[i], tables_a, need_ctx)
        else:
            xr, yr = _mla_mixer(hx, hy, xr, yr, mx[2], my[2], od_w_dn[i], od_q_norm_g[i], od_kv_norm_g[i],
                                od_w_uq[i], od_w_ukv[i], od_w_o[i], tables_c, need_ctx)
        w_rt = moe_router[l].T
        xr = _ec_moe(xr, norm2_g[l], mx[3], mx[4], mx[5], w_rt, wgu, wd, l)
        if need_ctx:
            yr = _ec_moe(yr, norm2_g[l], my[3], my[4], my[5], w_rt, wgu, wd, l)
    zero = jnp.zeros((1, d), F32)
    return _normmod(xr, final_g, zero, zero, F32)[None]
```

```python
import functools

import jax
import jax.numpy as jnp
from jax import lax
from jax.experimental import pallas as pl
from jax.experimental.pallas import tpu as pltpu

F32 = jnp.float32
BF16 = jnp.bfloat16

GRID_W = 64
ROPE_BASE = 10000.0
NORM_EPS = 1e-6
NEG_INF = -1e30
HEAD_DIM = 128
A_WINDOW = 128
EC_CAPACITY = 2
LOG2_E = 1.4426950408889634

LANES = 128
SUBLANES = 8
BF16_ROWS = 16
MIB = 1024 * 1024


def _params(n_axes, vmem_mib):
    return pltpu.CompilerParams(dimension_semantics=("arbitrary",) * n_axes, vmem_limit_bytes=vmem_mib * MIB)


def _pick(total, target, mult):
    if total <= target:
        return total
    best = None
    for d in range(mult, target + 1, mult):
        if total % d == 0:
            best = d
    assert best is not None, (total, target, mult)
    return best


def _sigmoid(v):
    return 1.0 / (1.0 + jnp.exp(-v))


def _norm_mod(xv, g, shift, scale):
    yv = xv * lax.rsqrt(jnp.mean(xv * xv, axis=-1, keepdims=True) + NORM_EPS)
    return (yv * g) * (1.0 + scale) + shift


def _rope(xv, cos, sin, pair):
    lane = lax.broadcasted_iota(jnp.int32, xv.shape, 1)
    first = (lane % (2 * pair)) < pair
    partner = jnp.where(first, pltpu.roll(xv, LANES - pair, 1), pltpu.roll(xv, pair, 1))
    return xv * cos + partner * sin


def _adaln_kernel(cc_ref, w_ref, b_ref, o_ref):
    cc = cc_ref[...]
    s = cc * _sigmoid(cc)
    w = w_ref[...]
    b = b_ref[...]
    r0 = jnp.sum(s[:, 0:1] * w, axis=0, keepdims=True) + b
    r1 = jnp.sum(s[:, 1:2] * w, axis=0, keepdims=True) + b
    o_ref[...] = jnp.concatenate([r0, r1, jnp.zeros((SUBLANES - 2, w.shape[1]), F32)], axis=0)


def _adaln(c, c_ctx, ada_w, ada_b):
    depth, d, n6 = ada_w.shape
    cc = jnp.zeros((d, SUBLANES), F32).at[:, 0].set(c[0]).at[:, 1].set(c_ctx)
    tn = _pick(n6, 512, LANES)
    return pl.pallas_call(
        _adaln_kernel,
        out_shape=jax.ShapeDtypeStruct((depth, SUBLANES, n6), F32),
        grid=(depth, n6 // tn),
        in_specs=[pl.BlockSpec((d, SUBLANES), lambda l, j: (0, 0)),
                  pl.BlockSpec((None, d, tn), lambda l, j: (l, 0, j)),
                  pl.BlockSpec((None, 1, tn), lambda l, j: (l, 0, j))],
        out_specs=pl.BlockSpec((None, SUBLANES, tn), lambda l, j: (l, 0, j)),
        compiler_params=_params(2, 48),
        name="adaln",
    )(cc, ada_w, ada_b.reshape(depth, 1, n6))


def _normmod_kernel(x_ref, g_ref, sh_ref, sc_ref, o_ref):
    o_ref[...] = _norm_mod(x_ref[...], g_ref[...], sh_ref[...], sc_ref[...]).astype(o_ref.dtype)


def _normmod(xr, g, shift, scale, out_dtype):
    r, d = xr.shape
    tr = _pick(r, 512, BF16_ROWS)
    vec = pl.BlockSpec((1, d), lambda i: (0, 0))
    return pl.pallas_call(
        _normmod_kernel,
        out_shape=jax.ShapeDtypeStruct((r, d), out_dtype),
        grid=(r // tr,),
        in_specs=[pl.BlockSpec((tr, d), lambda i: (i, 0)), vec, vec, vec],
        out_specs=pl.BlockSpec((tr, d), lambda i: (i, 0)),
        compiler_params=_params(1, 48),
        name="normmod",
    )(xr, g.reshape(1, d), shift, scale)


def _mm_kernel(*refs, n_pairs, epilogue, scale, pair):
    a_refs, w_refs, rest = refs[:n_pairs], refs[n_pairs:2 * n_pairs], refs[2 * n_pairs:]
    acc = None
    for a_ref, w_ref in zip(a_refs, w_refs):
        part = jnp.dot(a_ref[...], w_ref[...], preferred_element_type=F32)
        acc = part if acc is None else acc + part
    if epilogue == "plain":
        (o_ref,) = rest
        o_ref[...] = (acc * scale if scale != 1.0 else acc).astype(o_ref.dtype)
    elif epilogue == "rope":
        cos_ref, sin_ref, o_ref = rest
        cos, sin = cos_ref[...], sin_ref[...]
        for h in range(acc.shape[1] // LANES):
            sl = slice(h * LANES, (h + 1) * LANES)
            r = _rope(acc[:, sl], cos, sin, pair)
            o_ref[:, sl] = (r * scale if scale != 1.0 else r).astype(o_ref.dtype)
    else:
        x_ref, gate_ref, o_ref = rest
        o_ref[...] = x_ref[...] + gate_ref[...] * acc


def _matmul(a_list, w_list, *, out_dtype, epilogue="plain", extras=(), scale=1.0, pair=0, bm=1024, bn=1024):
    m = a_list[0].shape[0]
    n = w_list[0].shape[1]
    bm = _pick(m, bm, BF16_ROWS)
    bn = _pick(n, bn, LANES)
    in_specs = [pl.BlockSpec((bm, a.shape[1]), lambda j, i: (i, 0)) for a in a_list]
    in_specs += [pl.BlockSpec((w.shape[0], bn), lambda j, i: (0, j)) for w in w_list]
    if epilogue == "rope":
        in_specs += [pl.BlockSpec((bm, LANES), lambda j, i: (i, 0))] * 2
    elif epilogue == "resid":
        in_specs += [pl.BlockSpec((bm, bn), lambda j, i: (i, j)), pl.BlockSpec((1, bn), lambda j, i: (0, j))]
    return pl.pallas_call(
        functools.partial(_mm_kernel, n_pairs=len(a_list), epilogue=epilogue, scale=scale, pair=pair),
        out_shape=jax.ShapeDtypeStruct((m, n), out_dtype),
        grid=(n // bn, m // bm),
        in_specs=in_specs,
        out_specs=pl.BlockSpec((bm, bn), lambda j, i: (i, j)),
        compiler_params=_params(2, 56),
        name="mm_" + epilogue,
    )(*a_list, *w_list, *extras)


def _gqa_kernel(sink_ref, *refs, groups, tq, local, n_tokens):
    if local:
        q_ref, k_ref, v_ref, ky_ref, vy_ref, o_ref = refs
    else:
        q_ref, ky_ref, vy_ref, o_ref = refs
    hk = pl.program_id(0)
    nt = (((1,), (1,)), ((), ()))
    def with_ones(v):
        return jnp.concatenate([v, jnp.ones(v.shape, BF16)], axis=1)

    ky = ky_ref[...]
    vy = with_ones(vy_ref[...])
    if local:
        n = pl.program_id(1)
        win = tq + 2 * A_WINDOW
        start = pl.multiple_of(jnp.clip(n * tq - A_WINDOW, 0, n_tokens - win), BF16_ROWS)
        kw = k_ref[pl.ds(start, win), :]
        vw = with_ones(v_ref[pl.ds(start, win), :])
        qpos = n * tq + lax.broadcasted_iota(jnp.int32, (tq, win), 0)
        kpos = start + lax.broadcasted_iota(jnp.int32, (tq, win), 1)
        band = jnp.abs(kpos - qpos) <= A_WINDOW
    for gi in range(groups):
        sl = slice(gi * HEAD_DIM, (gi + 1) * HEAD_DIM)
        qg = q_ref[:, sl]
        sink = sink_ref[hk * groups + gi] * LOG2_E
        s_ctx = lax.dot_general(qg, ky, nt, preferred_element_type=F32)
        m = jnp.maximum(jnp.max(s_ctx, axis=1, keepdims=True), sink)
        if local:
            s_loc = jnp.where(band, lax.dot_general(qg, kw, nt, preferred_element_type=F32), NEG_INF)
            m = jnp.maximum(m, jnp.max(s_loc, axis=1, keepdims=True))
        acc = jnp.dot(jnp.exp2(s_ctx - m).astype(BF16), vy, preferred_element_type=F32)
        if local:
            acc = acc + jnp.dot(jnp.exp2(s_loc - m).astype(BF16), vw, preferred_element_type=F32)
        den = acc[:, HEAD_DIM:] + jnp.exp2(sink - m)
        o_ref[:, sl] = (acc[:, :HEAD_DIM] / den).astype(o_ref.dtype)


def _gqa(q, kx, vsrc_x, v_col0, ky, vsrc_y, sink, *, local):
    r, aq = q.shape
    hkv = ky.shape[1] // HEAD_DIM
    groups = aq // HEAD_DIM // hkv
    lc = ky.shape[0]
    gw = groups * HEAD_DIM
    tq = _pick(r, 256, BF16_ROWS)
    n_tokens = kx.shape[0] if local else 0
    if local:
        assert n_tokens >= tq + 2 * A_WINDOW
    q_spec = pl.BlockSpec((tq, gw), lambda h, i, s: (i, h))
    ctx_specs = [pl.BlockSpec((lc, HEAD_DIM), lambda h, i, s: (0, h)),
                 pl.BlockSpec((lc, HEAD_DIM), lambda h, i, s: (0, v_col0 + h))]
    if local:
        in_specs = [q_spec,
                    pl.BlockSpec((n_tokens, HEAD_DIM), lambda h, i, s: (0, h)),
                    pl.BlockSpec((n_tokens, HEAD_DIM), lambda h, i, s: (0, v_col0 + h))] + ctx_specs
        operands = (q, kx, vsrc_x, ky, vsrc_y)
    else:
        in_specs = [q_spec] + ctx_specs
        operands = (q, ky, vsrc_y)
    return pl.pallas_call(
        functools.partial(_gqa_kernel, groups=groups, tq=tq, local=local, n_tokens=n_tokens),
        out_shape=jax.ShapeDtypeStruct((r, aq), BF16),
        grid_spec=pltpu.PrefetchScalarGridSpec(
            num_scalar_prefetch=1, grid=(hkv, r // tq), in_specs=in_specs,
            out_specs=pl.BlockSpec((tq, gw), lambda h, i, s: (i, h))),
        compiler_params=_params(2, 48),
        name="gqa_local" if local else "gqa_ctx",
    )(sink, *operands)


CONV_HALO = 16
CONV_ROWS = 32
CONV_LANES = 256


def _conv_kernel(a_ref, g_ref, ap_ref, gp_ref, an_ref, gn_ref, dw_ref, lg_ref, lb_ref, o_ref, u_ref, z_ref,
                 *, taps, n_blocks):
    i = pl.program_id(0)
    tr, ch = z_ref.shape
    pad = (taps - 1) // 2

    def glu(a, g):
        return a.astype(F32) * _sigmoid(g.astype(F32))

    u_ref[0, 0:CONV_HALO, :] = glu(ap_ref[...], gp_ref[...]) * (i > 0).astype(F32)
    u_ref[0, CONV_HALO:CONV_HALO + tr, :] = glu(a_ref[...], g_ref[...])
    u_ref[0, CONV_HALO + tr:, :] = glu(an_ref[...], gn_ref[...]) * (i < n_blocks - 1).astype(F32)
    span = u_ref.shape[1] - SUBLANES
    for s in range(1, SUBLANES):
        u_ref[s, 0:span, :] = u_ref[0, s:s + span, :]

    def lane_chunk(cc, carry):
        c0 = pl.multiple_of(cc * CONV_LANES, CONV_LANES)
        for r0 in range(0, tr, CONV_ROWS):
            acc = jnp.zeros((CONV_ROWS, CONV_LANES), F32)
            for j in range(taps):
                shift = j - pad + CONV_HALO
                acc = acc + (u_ref[shift % SUBLANES, pl.ds(r0 + shift - shift % SUBLANES, CONV_ROWS),
                                   pl.ds(c0, CONV_LANES)]
                             * dw_ref[pl.ds(j, 1), pl.ds(c0, CONV_LANES)])
            z_ref[pl.ds(r0, CONV_ROWS), pl.ds(c0, CONV_LANES)] = acc
        return carry

    lax.fori_loop(0, ch // CONV_LANES, lane_chunk, 0)
    z = z_ref[...]
    mu = jnp.mean(z, axis=-1, keepdims=True)
    zc = z - mu
    var = jnp.mean(zc * zc, axis=-1, keepdims=True)
    yv = zc * lax.rsqrt(var + NORM_EPS) * lg_ref[...] + lb_ref[...]
    o_ref[...] = (yv * _sigmoid(yv)).astype(o_ref.dtype)


def _conformer_conv(agv, dw, ln_g, ln_b):
    r = agv.shape[0]
    taps, ch = dw.shape
    assert (taps - 1) // 2 < CONV_HALO and ch % CONV_LANES == 0
    tr = _pick(r, 256, CONV_ROWS)
    nb = r // tr
    hb = tr // CONV_HALO
    last = r // CONV_HALO - 1
    dwp = jnp.zeros((2 * CONV_HALO, ch), F32).at[:taps].set(dw)
    cur = lambda col: pl.BlockSpec((tr, ch), lambda i: (i, col))
    prv = lambda col: pl.BlockSpec((CONV_HALO, ch), lambda i: (jnp.maximum(i * hb - 1, 0), col))
    nxt = lambda col: pl.BlockSpec((CONV_HALO, ch), lambda i: (jnp.minimum((i + 1) * hb, last), col))
    vec = pl.BlockSpec((1, ch), lambda i: (0, 0))
    return pl.pallas_call(
        functools.partial(_conv_kernel, taps=taps, n_blocks=nb),
        out_shape=jax.ShapeDtypeStruct((r, ch), BF16),
        grid=(nb,),
        in_specs=[cur(0), cur(1), prv(0), prv(1), nxt(0), nxt(1),
                  pl.BlockSpec((2 * CONV_HALO, ch), lambda i: (0, 0)), vec, vec],
        out_specs=pl.BlockSpec((tr, ch), lambda i: (i, 0)),
        scratch_shapes=[pltpu.VMEM((SUBLANES, tr + 2 * CONV_HALO, ch), F32), pltpu.VMEM((tr, ch), F32)],
        compiler_params=_params(1, 48),
        name="conformer_conv",
    )(agv, agv, agv, agv, agv, agv, dwp, ln_g.reshape(1, ch), ln_b.reshape(1, ch))


def _mla_down_kernel(h_ref, w_ref, qg_ref, kg_ref, cos_ref, sin_ref, q_ref, kv_ref, kr_ref, *, q_rank, kv_rank,
                     pair, use_rope):
    acc = jnp.dot(h_ref[...], w_ref[...], preferred_element_type=F32)

    def rms(v, g):
        return v * lax.rsqrt(jnp.mean(v * v, axis=-1, keepdims=True) + NORM_EPS) * g

    q_ref[...] = rms(acc[:, :q_rank], qg_ref[...]).astype(q_ref.dtype)
    kv_ref[...] = rms(acc[:, q_rank:q_rank + kv_rank], kg_ref[...]).astype(kv_ref.dtype)
    kr = acc[:, q_rank + kv_rank:]
    if use_rope:
        kr = _rope(kr, cos_ref[...], sin_ref[...], pair)
    kr_ref[...] = kr.astype(kr_ref.dtype)


def _mla_down(h, w_dn_p, q_g, kv_g, cos, sin, *, pair, use_rope):
    m, d = h.shape
    q_rank, kv_rank = q_g.shape[0], kv_g.shape[0]
    wn = w_dn_p.shape[1]
    assert wn == q_rank + kv_rank + LANES
    bm = _pick(m, 512, BF16_ROWS)
    row = lambda w: pl.BlockSpec((bm, w), lambda i: (i, 0))
    return pl.pallas_call(
        functools.partial(_mla_down_kernel, q_rank=q_rank, kv_rank=kv_rank, pair=pair, use_rope=use_rope),
        out_shape=(jax.ShapeDtypeStruct((m, q_rank), BF16), jax.ShapeDtypeStruct((m, kv_rank), BF16),
                   jax.ShapeDtypeStruct((m, LANES), BF16)),
        grid=(m // bm,),
        in_specs=[row(d), pl.BlockSpec((d, wn), lambda i: (0, 0)),
                  pl.BlockSpec((1, q_rank), lambda i: (0, 0)), pl.BlockSpec((1, kv_rank), lambda i: (0, 0)),
                  row(LANES), row(LANES)],
        out_specs=(row(q_rank), row(kv_rank), row(LANES)),
        compiler_params=_params(1, 56),
        name="mla_down",
    )(h, w_dn_p, q_g.reshape(1, q_rank), kv_g.reshape(1, kv_rank), cos, sin)


MLA_TQ = 1024
MLA_SUB = 1024
MLA_KEY_CHUNK = 1280
MLA_STAGES = 12


def _mla_attn_kernel(qn_ref, qr_ref, kn_ref, kr_ref, v_ref, o_ref, kcat_ref, vcat_ref, qcat_ref, s_ref, *, tk, sub):
    nk = kcat_ref.shape[0]
    tq = qn_ref.shape[0]
    n_sub = tq // sub

    @pl.when(pl.program_id(1) == 0)
    def _():
        kcat_ref[:, :HEAD_DIM] = kn_ref[...]
        kcat_ref[:, HEAD_DIM:] = kr_ref[...]
        vcat_ref[:, :HEAD_DIM] = v_ref[...]
        vcat_ref[:, HEAD_DIM:] = jnp.ones((nk, HEAD_DIM), BF16)

    qcat_ref[:, :HEAD_DIM] = qn_ref[...]
    qcat_ref[:, HEAD_DIM:] = qr_ref[...]
    nt = (((1,), (1,)), ((), ()))
    n_chunks = nk // tk

    def scores(buf, u, off):
        s_ref[buf, u] = lax.dot_general(qcat_ref[u * sub:(u + 1) * sub, :], kcat_ref[pl.ds(off, tk), :], nt,
                                        preferred_element_type=F32)

    def absorb(buf, u, off, m, acc):
        s = s_ref[buf, u]
        m_new = jnp.maximum(m, jnp.max(s, axis=1, keepdims=True))
        p = jnp.exp2(s - m_new).astype(BF16)
        acc = jnp.exp2(m - m_new) * acc + jnp.dot(p, vcat_ref[pl.ds(off, tk), :], preferred_element_type=F32)
        return m_new, acc

    def stage(buf, off, carry):
        out = []
        for u in range(n_sub):
            out.append(absorb(buf, u, off, *carry[u]))
            scores(1 - buf, u, off + tk)
        return tuple(out)

    def trip(j, carry):
        off = pl.multiple_of(j * (MLA_STAGES * tk), tk)
        for k in range(MLA_STAGES):
            carry = stage(k % 2, off + k * tk, carry)
        return carry

    for u in range(n_sub):
        scores(0, u, 0)
    carry = tuple((jnp.full((sub, 1), NEG_INF, F32), jnp.zeros((sub, 2 * HEAD_DIM), F32)) for _ in range(n_sub))
    n_stages = n_chunks - 1
    carry = lax.fori_loop(0, n_stages // MLA_STAGES, trip, carry)
    for k in range(n_stages - n_stages % MLA_STAGES, n_stages):
        carry = stage(k % 2, k * tk, carry)
    for u in range(n_sub):
        _, acc = absorb(n_stages % 2, u, n_stages * tk, *carry[u])
        o_ref[u * sub:(u + 1) * sub, :] = (acc[:, :HEAD_DIM] / acc[:, HEAD_DIM:]).astype(o_ref.dtype)


def _mla_attn(qn, qr, kv, kr, *, n_heads, key_rows, key_block):
    r = qn.shape[0]
    tq = _pick(r, MLA_TQ, MLA_SUB) if r >= MLA_SUB else r
    sub = min(MLA_SUB, tq)
    tk = _pick(key_rows, MLA_KEY_CHUNK, LANES)
    once = pl.Buffered(1)
    kspec = lambda col0: pl.BlockSpec((key_rows, HEAD_DIM), lambda h, i: (key_block, col0 + h), pipeline_mode=once)
    qspec = pl.BlockSpec((tq, HEAD_DIM), lambda h, i: (i, h))
    return pl.pallas_call(
        functools.partial(_mla_attn_kernel, tk=tk, sub=sub),
        out_shape=jax.ShapeDtypeStruct((r, n_heads * HEAD_DIM), BF16),
        grid=(n_heads, r // tq),
        in_specs=[qspec, qspec, kspec(0),
                  pl.BlockSpec((key_rows, LANES), lambda h, i: (key_block, 0), pipeline_mode=once), kspec(n_heads)],
        out_specs=pl.BlockSpec((tq, HEAD_DIM), lambda h, i: (i, h)),
        scratch_shapes=[pltpu.VMEM((key_rows, 2 * HEAD_DIM), BF16), pltpu.VMEM((key_rows, 2 * HEAD_DIM), BF16),
                        pltpu.VMEM((tq, 2 * HEAD_DIM), BF16), pltpu.VMEM((2, tq // sub, sub, tk), F32)],
        compiler_params=_params(2, 60),
        name="mla_attn",
    )(qn, qr, kv, kr, kv)


def _router_kernel(x_ref, g_ref, sh_ref, sc_ref, wt_ref, aff_ref):
    h = _norm_mod(x_ref[...], g_ref[...], sh_ref[...], sc_ref[...])
    w = wt_ref[...]
    ne = w.shape[0]
    h_hi = h.astype(BF16)
    h_lo = (h - h_hi.astype(F32)).astype(BF16)
    w_hi = w.astype(BF16)
    w_lo = (w - w_hi.astype(F32)).astype(BF16)
    nt = (((1,), (1,)), ((), ()))
    both = lax.dot_general(jnp.concatenate([w_hi, w_lo], axis=0), h_hi, nt, preferred_element_type=F32)
    logits = both[:ne] + both[ne:] + lax.dot_general(w_hi, h_lo, nt, preferred_element_type=F32)
    e = jnp.exp(logits - jnp.max(logits, axis=0, keepdims=True))
    aff_ref[...] = e / jnp.sum(e, axis=0, keepdims=True)


def _router(xr, g, shift, scale, w_router_t):
    r, d = xr.shape
    ne = w_router_t.shape[0]
    tr = _pick(r, 512, LANES)
    vec = pl.BlockSpec((1, d), lambda i: (0, 0))
    return pl.pallas_call(
        _router_kernel,
        out_shape=jax.ShapeDtypeStruct((ne, r), F32),
        grid=(r // tr,),
        in_specs=[pl.BlockSpec((tr, d), lambda i: (i, 0)), vec, vec, vec, pl.BlockSpec((ne, d), lambda i: (0, 0))],
        out_specs=pl.BlockSpec((ne, tr), lambda i: (0, i)),
        compiler_params=_params(1, 48),
        name="router",
    )(xr, g.reshape(1, d), shift, scale, w_router_t)


def _lane_cumsum(v):
    lane = lax.broadcasted_iota(jnp.int32, v.shape, 1)
    k = 1
    while k < LANES:
        v = v + jnp.where(lane >= k, pltpu.roll(v, k, 1), 0)
        k *= 2
    return v


def _select_kernel(a_ref, idx_ref, thr_ref, *, cap):
    a = a_ref[...]
    ne, g, _ = a.shape
    n_slots = idx_ref.shape[2]
    bits = lax.bitcast_convert_type(a, jnp.int32)

    def count(mask):
        return jnp.sum(jnp.sum(mask.astype(F32), axis=2, keepdims=True), axis=1, keepdims=True)

    thr = jnp.zeros((ne, 1, 1), jnp.int32)
    for b in range(30, -1, -1):
        cand = thr | (1 << b)
        thr = jnp.where(count(bits >= cand) >= cap, cand, thr)
    thr_ref[...] = jnp.broadcast_to(thr, thr_ref.shape)
    before = lax.broadcasted_iota(jnp.int32, (g, g), 1) < lax.broadcasted_iota(jnp.int32, (g, g), 0)
    upto = lax.broadcasted_iota(jnp.int32, (g, g), 1) <= lax.broadcasted_iota(jnp.int32, (g, g), 0)
    row_id = lax.broadcasted_iota(jnp.int32, (g, n_slots), 0).astype(F32)
    slot = lax.broadcasted_iota(jnp.int32, (1, n_slots), 1).astype(F32)

    def over_rows(tri, per_row):
        return jnp.dot(tri.astype(F32), jnp.broadcast_to(per_row, (g, LANES)), precision=lax.Precision.HIGHEST,
                       preferred_element_type=F32)[:, 0:1]

    def per_expert(e, carry):
        bits_e = lax.bitcast_convert_type(a_ref[e], jnp.int32)
        thr_e = thr_ref[e][0:1, 0:1]
        above = bits_e > thr_e
        tie = bits_e == thr_e
        need = cap - jnp.sum(jnp.sum(above.astype(F32), axis=1, keepdims=True), axis=0, keepdims=True)
        tie_i = tie.astype(jnp.int32)
        incl = _lane_cumsum(tie_i)
        rank = (incl - tie_i).astype(F32) + over_rows(before, incl[:, LANES - 1:LANES].astype(F32))
        sel = above | (tie & (rank < need))
        cum = _lane_cumsum(sel.astype(jnp.int32)).astype(F32)
        row_cnt = cum[:, LANES - 1:LANES]
        row_end = over_rows(upto, row_cnt)
        g_of = jnp.sum((row_end <= slot).astype(F32), axis=0, keepdims=True)
        onehot = (row_id == g_of).astype(F32)
        start_of = jnp.sum(onehot * (row_end - row_cnt), axis=0, keepdims=True)
        cum_of = lax.dot_general(cum, onehot, (((0,), (0,)), ((), ())), preferred_element_type=F32)
        lane_of = jnp.sum((cum_of <= slot - start_of).astype(F32), axis=0, keepdims=True)
        idx_ref[e] = (g_of * LANES + lane_of).astype(jnp.int32)
        return carry

    lax.fori_loop(0, ne, per_expert, 0)


def _select(aff_t, cap):
    ne, r = aff_t.shape
    assert r % LANES == 0
    n_pad = max(r, SUBLANES * LANES)
    g = n_pad // LANES
    a3 = jnp.pad(aff_t, ((0, 0), (0, n_pad - r))).reshape(ne, g, LANES)
    n_slots = max(cap, LANES)
    idx = pl.pallas_call(
        functools.partial(_select_kernel, cap=cap),
        out_shape=jax.ShapeDtypeStruct((ne, 1, n_slots), jnp.int32),
        grid=(1,),
        in_specs=[pl.BlockSpec((ne, g, LANES), lambda i: (0, 0, 0))],
        out_specs=pl.BlockSpec((ne, 1, n_slots), lambda i: (0, 0, 0)),
        scratch_shapes=[pltpu.VMEM((ne, SUBLANES, LANES), jnp.int32)],
        compiler_params=_params(1, 32),
        name="ec_select",
    )(a3)[:, 0, :cap]
    return idx, jnp.take_along_axis(aff_t, idx, axis=1)


def _row_copy(src_hbm, buf, sem, slot, tok, r):
    return pltpu.make_async_copy(src_hbm.at[pl.ds(tok, 1), :], buf.at[slot, pl.ds(r, 1), :], sem.at[slot])


def _start_gather(idx_ref, src_hbm, buf, sem, step, slot, rows):
    for r in range(rows):
        _row_copy(src_hbm, buf, sem, slot, idx_ref[step * rows + r], r).start()


def _for_slot(slot, fn, n_slots=2):
    for s in range(n_slots):
        pl.when(slot == s)(functools.partial(fn, s))


def _wait_rows(src_hbm, buf, sem, slot, rows):
    for _ in range(rows):
        _row_copy(src_hbm, buf, sem, slot, 0, 0).wait()


def _moe_ffn_kernel(idx_ref, gate_ref, g_ref, sh_ref, sc_ref, wgu_ref, wd_ref, x_hbm, y_ref, xbuf, sem,
                    *, rows, n_steps):
    step = pl.program_id(0) * pl.num_programs(1) + pl.program_id(1)
    slot = step % 2

    @pl.when(step == 0)
    def _():
        _start_gather(idx_ref, x_hbm, xbuf, sem, 0, 0, rows)

    @pl.when(step + 1 < n_steps)
    def _():
        _for_slot(1 - slot, lambda s: _start_gather(idx_ref, x_hbm, xbuf, sem, step + 1, s, rows))

    _wait_rows(x_hbm, xbuf, sem, slot, rows)
    h = _norm_mod(xbuf[slot], g_ref[...], sh_ref[...], sc_ref[...]).astype(BF16)
    ff = wd_ref.shape[0]
    halves = []
    for hh in (h[:rows // 2], h[rows // 2:]):
        au = jnp.dot(hh, wgu_ref[...], preferred_element_type=F32)
        a, u = au[:, :ff], au[:, ff:]
        halves.append((a * _sigmoid(a) * u).astype(BF16))
    z = jnp.concatenate(halves, axis=0)
    y_ref[...] = jnp.dot(z, wd_ref[...], preferred_element_type=F32) * gate_ref[...]


def _moe_ffn(idx_flat, gate_col, xr, g, shift, scale, wgu, wd, layer, *, rows):
    r, d = xr.shape
    _, ne, ff, _ = wd.shape
    assert ff % LANES == 0
    slots = idx_flat.shape[0]
    nc = slots // ne // rows
    vec = pl.BlockSpec((1, d), lambda e, c, idx: (0, 0))
    return pl.pallas_call(
        functools.partial(_moe_ffn_kernel, rows=rows, n_steps=ne * nc),
        out_shape=jax.ShapeDtypeStruct((slots, d), F32),
        grid_spec=pltpu.PrefetchScalarGridSpec(
            num_scalar_prefetch=1, grid=(ne, nc),
            in_specs=[pl.BlockSpec((rows, 1), lambda e, c, idx: (e * nc + c, 0)), vec, vec, vec,
                      pl.BlockSpec((None, None, d, 2 * ff), lambda e, c, idx: (layer, e, 0, 0)),
                      pl.BlockSpec((None, None, ff, d), lambda e, c, idx: (layer, e, 0, 0)),
                      pl.BlockSpec(memory_space=pl.ANY)],
            out_specs=pl.BlockSpec((rows, d), lambda e, c, idx: (e * nc + c, 0)),
            scratch_shapes=[pltpu.VMEM((2, rows, d), F32), pltpu.SemaphoreType.DMA((2,))]),
        compiler_params=_params(2, 56),
        name="moe_ffn",
    )(idx_flat, gate_col, g.reshape(1, d), shift, scale, wgu, wd, xr)


MOE_ROWS = 256
COMBINE_BUFS = 3


def _moe_combine_kernel(idx_ref, y_ref, gate_ref, x_hbm, o_hbm, abuf, gsem, ssem, *, rows, n_steps, nc):
    del x_hbm
    c = pl.program_id(1)
    step = pl.program_id(0) * nc + c
    slot = step % COMBINE_BUFS
    nxt = (step + 1) % COMBINE_BUFS
    prv = (step + 2) % COMBINE_BUFS

    def scatter_copy(s, sl, r):
        return pltpu.make_async_copy(abuf.at[sl, pl.ds(r, 1), :], o_hbm.at[pl.ds(idx_ref[s * rows + r], 1), :],
                                     ssem.at[sl])

    def wait_scatter(sl):
        for _ in range(rows):
            pltpu.make_async_copy(abuf.at[sl, pl.ds(0, 1), :], o_hbm.at[pl.ds(0, 1), :], ssem.at[sl]).wait()

    def gather_next():
        _for_slot(nxt, lambda s: _start_gather(idx_ref, o_hbm, abuf, gsem, step + 1, s, rows), COMBINE_BUFS)

    @pl.when(step == 0)
    def _():
        _start_gather(idx_ref, o_hbm, abuf, gsem, 0, 0, rows)

    @pl.when(c < nc - 1)
    def _():
        @pl.when(c >= 2)
        def _():
            wait_scatter(nxt)
        gather_next()

    _wait_rows(o_hbm, abuf, gsem, slot, rows)
    abuf[slot] = abuf[slot] + gate_ref[...] * y_ref[...]

    def start_scatter(s):
        for r in range(rows):
            scatter_copy(step, s, r).start()
    _for_slot(slot, start_scatter, COMBINE_BUFS)

    @pl.when(c == nc - 1)
    def _():
        if nc >= 3:
            wait_scatter(nxt)
        if nc >= 2:
            wait_scatter(prv)
        wait_scatter(slot)

        @pl.when(step + 1 < n_steps)
        def _():
            gather_next()


def _moe_combine(idx_flat, y, gate_vec, xr, *, n_experts, rows):
    r, d = xr.shape
    slots = idx_flat.shape[0]
    nc = slots // n_experts // rows
    return pl.pallas_call(
        functools.partial(_moe_combine_kernel, rows=rows, n_steps=n_experts * nc, nc=nc),
        out_shape=jax.ShapeDtypeStruct((r, d), F32),
        grid_spec=pltpu.PrefetchScalarGridSpec(
            num_scalar_prefetch=1, grid=(n_experts, nc),
            in_specs=[pl.BlockSpec((rows, d), lambda e, c, idx: (e * nc + c, 0)),
                      pl.BlockSpec((1, d), lambda e, c, idx: (0, 0)),
                      pl.BlockSpec(memory_space=pl.ANY)],
            out_specs=pl.BlockSpec(memory_space=pl.ANY),
            scratch_shapes=[pltpu.VMEM((COMBINE_BUFS, rows, d), F32), pltpu.SemaphoreType.DMA((COMBINE_BUFS,)),
                            pltpu.SemaphoreType.DMA((COMBINE_BUFS,))]),
        input_output_aliases={3: 0},
        compiler_params=_params(2, 48),
        name="moe_combine",
    )(idx_flat, y, gate_vec, xr)


def _ec_moe(xr, g, shift, scale, out_gate, w_router_t, wgu, wd, layer):
    r = xr.shape[0]
    ne = w_router_t.shape[0]
    cap = EC_CAPACITY * r // ne
    rows = _pick(cap, MOE_ROWS, 2 * BF16_ROWS)
    aff_t = _router(xr, g, shift, scale, w_router_t)
    idx, gate = _select(aff_t, cap)
    idx_flat = idx.reshape(-1)
    y = _moe_ffn(idx_flat, gate.reshape(-1, 1), xr, g, shift, scale, wgu, wd, layer, rows=rows)
    return _moe_combine(idx_flat, y, out_gate, xr, n_experts=ne, rows=rows)


def _rope_tables(n, dim):
    half = dim // 2
    inv = ROPE_BASE ** (-jnp.arange(0, half, 2, dtype=F32) / half)
    rows = jnp.repeat(jnp.arange(n // GRID_W, dtype=F32), GRID_W)
    cols = (jnp.arange(n) % GRID_W).astype(F32)
    ar = rows[:, None] * inv[None, :]
    ac = cols[:, None] * inv[None, :]
    cos = jnp.concatenate([jnp.cos(ar), jnp.cos(ar), jnp.cos(ac), jnp.cos(ac)], axis=1)
    sin = jnp.concatenate([-jnp.sin(ar), jnp.sin(ar), -jnp.sin(ac), jnp.sin(ac)], axis=1)
    if dim < LANES:
        cos = jnp.concatenate([cos, jnp.ones((n, LANES - dim), F32)], axis=1)
        sin = jnp.concatenate([sin, jnp.zeros((n, LANES - dim), F32)], axis=1)
    return cos, sin


def _even_mixer(hx, hy, x, y, gate_x, gate_y, w_in, sink, dw, ln_g, ln_b, w_out, tables, need_ctx):
    ch = dw.shape[1]
    aq = w_out.shape[0] - ch
    akv = (w_in.shape[1] - aq - 2 * ch) // 2
    cos, sin = tables
    scale = HEAD_DIM ** -0.5 * LOG2_E
    pair = HEAD_DIM // 4
    wq = w_in[:, :aq].astype(BF16)
    wk = w_in[:, aq:aq + akv].astype(BF16)
    w_agv = jnp.concatenate([w_in[:, aq + 2 * akv:], w_in[:, aq + akv:aq + 2 * akv]], axis=1).astype(BF16)
    w_o1 = w_out[:aq].astype(BF16)
    w_o2 = w_out[aq:].astype(BF16)
    v_col0 = 2 * ch // HEAD_DIM

    qx = _matmul([hx], [wq], out_dtype=BF16, epilogue="rope", extras=(cos, sin), scale=scale, pair=pair)
    kx = _matmul([hx], [wk], out_dtype=BF16, epilogue="rope", extras=(cos, sin), pair=pair)
    agv_x = _matmul([hx], [w_agv], out_dtype=BF16)
    ky = _matmul([hy], [wk], out_dtype=BF16)
    agv_y = _matmul([hy], [w_agv], out_dtype=BF16)

    att_x = _gqa(qx, kx, agv_x, v_col0, ky, agv_y, sink, local=True)
    conv_x = _conformer_conv(agv_x, dw, ln_g, ln_b)
    x = _matmul([att_x, conv_x], [w_o1, w_o2], out_dtype=F32, epilogue="resid", extras=(x, gate_x), bn=512)
    if need_ctx:
        qy = _matmul([hy], [wq], out_dtype=BF16, scale=scale)
        att_y = _gqa(qy, None, None, v_col0, ky, agv_y, sink, local=False)
        conv_y = _conformer_conv(agv_y, dw, ln_g, ln_b)
        y = _matmul([att_y, conv_y], [w_o1, w_o2], out_dtype=F32, epilogue="resid", extras=(y, gate_y), bn=512)
    return x, y


def _mla_mixer(hx, hy, x, y, gate_x, gate_y, w_dn, q_g, kv_g, w_uq, w_ukv, w_o, tables, need_ctx):
    n, d = hx.shape
    lc = hy.shape[0]
    q_rank, kv_rank = q_g.shape[0], kv_g.shape[0]
    n_heads = w_o.shape[0] // HEAD_DIM
    rope_dim = w_dn.shape[1] - q_rank - kv_rank
    assert n % lc == 0 and rope_dim <= LANES
    cos, sin = tables
    pair = rope_dim // 4
    scale = (HEAD_DIM + rope_dim) ** -0.5 * LOG2_E
    w_dn_p = jnp.concatenate([w_dn, jnp.zeros((d, LANES - rope_dim), F32)], axis=1).astype(BF16)
    w_uq3 = w_uq.reshape(q_rank, n_heads, HEAD_DIM + rope_dim)
    w_qn = w_uq3[:, :, :HEAD_DIM].reshape(q_rank, n_heads * HEAD_DIM).astype(BF16)
    w_qr = jnp.concatenate([w_uq3[:, :, HEAD_DIM:], jnp.zeros((q_rank, n_heads, LANES - rope_dim), F32)],
                           axis=2).reshape(q_rank, n_heads * LANES).astype(BF16)
    w_kv3 = w_ukv.reshape(kv_rank, n_heads, 2 * HEAD_DIM)
    w_kv = jnp.concatenate([w_kv3[:, :, :HEAD_DIM].reshape(kv_rank, -1), w_kv3[:, :, HEAD_DIM:].reshape(kv_rank, -1)],
                           axis=1).astype(BF16)
    w_ob = w_o.astype(BF16)

    qlat_x, ckv_x, kr_x = _mla_down(hx, w_dn_p, q_g, kv_g, cos, sin, pair=pair, use_rope=True)
    qlat_y, ckv_y, kr_y = _mla_down(hy, w_dn_p, q_g, kv_g, cos[:lc], sin[:lc], pair=pair, use_rope=False)
    kv_all = _matmul([jnp.concatenate([ckv_x, ckv_y], axis=0)], [w_kv], out_dtype=BF16, bm=1280)
    kr_all = jnp.concatenate([kr_x, kr_y], axis=0)

    qn_x = _matmul([qlat_x], [w_qn], out_dtype=BF16, scale=scale)
    qr_x = _matmul([qlat_x], [w_qr], out_dtype=BF16, epilogue="rope", extras=(cos, sin), scale=scale, pair=pair)
    o_x = _mla_attn(qn_x, qr_x, kv_all, kr_all, n_heads=n_heads, key_rows=n + lc, key_block=0)
    x = _matmul([o_x], [w_ob], out_dtype=F32, epilogue="resid", extras=(x, gate_x), bn=512)
    if need_ctx:
        qn_y = _matmul([qlat_y], [w_qn], out_dtype=BF16, scale=scale)
        qr_y = _matmul([qlat_y], [w_qr], out_dtype=BF16, scale=scale)
        o_y = _mla_attn(qn_y, qr_y, kv_all, kr_all, n_heads=n_heads, key_rows=lc, key_block=n // lc)
        y = _matmul([o_y], [w_ob], out_dtype=F32, epilogue="resid", extras=(y, gate_y), bn=512)
    return x, y


def kernel(x, c, ctx, c_ctx, ada_w, ada_b, norm1_g, norm2_g, ev_w_in, ev_sink, ev_dw, ev_ln_g, ev_ln_b, ev_w_out, od_w_dn, od_q_norm_g, od_kv_norm_g, od_w_uq, od_w_ukv, od_w_o, moe_router, moe_w_gate, moe_w_up, moe_w_down, final_g):
    b, n, d = x.shape
    assert b == 1 and n % GRID_W == 0
    depth = ada_w.shape[0]
    xr, yr = x[0], ctx[0]
    mods = _adaln(c, c_ctx, ada_w, ada_b)
    tables_a = _rope_tables(n, HEAD_DIM)
    rope_dim = od_w_dn.shape[2] - od_q_norm_g.shape[1] - od_kv_norm_g.shape[1]
    tables_c = _rope_tables(n, rope_dim)
    wgu = jnp.concatenate([moe_w_gate, moe_w_up], axis=3).astype(BF16)
    wd = moe_w_down.astype(BF16)
    for l in range(depth):
        need_ctx = l < depth - 1
        mx = [mods[l, 0:1, i * d:(i + 1) * d] for i in range(6)]
        my = [mods[l, 1:2, i * d:(i + 1) * d] for i in range(6)]
        hx = _normmod(xr, norm1_g[l], mx[0], mx[1], BF16)
        hy = _normmod(yr, norm1_g[l], my[0], my[1], BF16)
        i = l // 2
        if l % 2 == 0:
            xr, yr = _even_mixer(hx, hy, xr, yr, mx[2], my[2], ev_w_in[i], ev_sink[i], ev_dw[i], ev_ln_g[i],
                                 ev_ln_b[i], ev_w_out[i], tables_a, need_ctx)
        else:
            xr, yr = _mla_mixer(hx, hy, xr, yr, mx[2], my[2], od_w_dn[i], od_q_norm_g[i], od_kv_norm_g[i],
                                od_w_uq[i], od_w_ukv[i], od_w_o[i], tables_c, need_ctx)
        w_rt = moe_router[l].T
        xr = _ec_moe(xr, norm2_g[l], mx[3], mx[4], mx[5], w_rt, wgu, wd, l)
        if need_ctx:
            yr = _ec_moe(yr, norm2_g[l], my[3], my[4], my[5], w_rt, wgu, wd, l)
    zero = jnp.zeros((1, d), F32)
    return _normmod(xr, final_g, zero, zero, F32)[None]
```

```python
import functools

import jax
import jax.numpy as jnp
from jax import lax
from jax.experimental import pallas as pl
from jax.experimental.pallas import tpu as pltpu

F32 = jnp.float32
BF16 = jnp.bfloat16

GRID_W = 64
ROPE_BASE = 10000.0
NORM_EPS = 1e-6
NEG_INF = -1e30
HEAD_DIM = 128
A_WINDOW = 128
EC_CAPACITY = 2
LOG2_E = 1.4426950408889634

LANES = 128
SUBLANES = 8
BF16_ROWS = 16
MIB = 1024 * 1024


def _params(n_axes, vmem_mib):
    return pltpu.CompilerParams(dimension_semantics=("arbitrary",) * n_axes, vmem_limit_bytes=vmem_mib * MIB)


def _pick(total, target, mult):
    if total <= target:
        return total
    best = None
    for d in range(mult, target + 1, mult):
        if total % d == 0:
            best = d
    assert best is not None, (total, target, mult)
    return best


def _sigmoid(v):
    return 1.0 / (1.0 + jnp.exp(-v))


def _norm_mod(xv, g, shift, scale):
    yv = xv * lax.rsqrt(jnp.mean(xv * xv, axis=-1, keepdims=True) + NORM_EPS)
    return (yv * g) * (1.0 + scale) + shift


def _rope(xv, cos, sin, pair):
    lane = lax.broadcasted_iota(jnp.int32, xv.shape, 1)
    first = (lane % (2 * pair)) < pair
    partner = jnp.where(first, pltpu.roll(xv, LANES - pair, 1), pltpu.roll(xv, pair, 1))
    return xv * cos + partner * sin


def _adaln_kernel(cc_ref, w_ref, b_ref, o_ref):
    cc = cc_ref[...]
    s = cc * _sigmoid(cc)
    w = w_ref[...]
    b = b_ref[...]
    r0 = jnp.sum(s[:, 0:1] * w, axis=0, keepdims=True) + b
    r1 = jnp.sum(s[:, 1:2] * w, axis=0, keepdims=True) + b
    o_ref[...] = jnp.concatenate([r0, r1, jnp.zeros((SUBLANES - 2, w.shape[1]), F32)], axis=0)


def _adaln(c, c_ctx, ada_w, ada_b):
    depth, d, n6 = ada_w.shape
    cc = jnp.zeros((d, SUBLANES), F32).at[:, 0].set(c[0]).at[:, 1].set(c_ctx)
    tn = _pick(n6, 512, LANES)
    return pl.pallas_call(
        _adaln_kernel,
        out_shape=jax.ShapeDtypeStruct((depth, SUBLANES, n6), F32),
        grid=(depth, n6 // tn),
        in_specs=[pl.BlockSpec((d, SUBLANES), lambda l, j: (0, 0)),
                  pl.BlockSpec((None, d, tn), lambda l, j: (l, 0, j)),
                  pl.BlockSpec((None, 1, tn), lambda l, j: (l, 0, j))],
        out_specs=pl.BlockSpec((None, SUBLANES, tn), lambda l, j: (l, 0, j)),
        compiler_params=_params(2, 48),
        name="adaln",
    )(cc, ada_w, ada_b.reshape(depth, 1, n6))


def _normmod_kernel(x_ref, g_ref, sh_ref, sc_ref, o_ref):
    o_ref[...] = _norm_mod(x_ref[...], g_ref[...], sh_ref[...], sc_ref[...]).astype(o_ref.dtype)


def _normmod(xr, g, shift, scale, out_dtype):
    r, d = xr.shape
    tr = _pick(r, 512, BF16_ROWS)
    vec = pl.BlockSpec((1, d), lambda i: (0, 0))
    return pl.pallas_call(
        _normmod_kernel,
        out_shape=jax.ShapeDtypeStruct((r, d), out_dtype),
        grid=(r // tr,),
        in_specs=[pl.BlockSpec((tr, d), lambda i: (i, 0)), vec, vec, vec],
        out_specs=pl.BlockSpec((tr, d), lambda i: (i, 0)),
        compiler_params=_params(1, 48),
        name="normmod",
    )(xr, g.reshape(1, d), shift, scale)


def _mm_kernel(*refs, n_pairs, epilogue, scale, pair):
    a_refs, w_refs, rest = refs[:n_pairs], refs[n_pairs:2 * n_pairs], refs[2 * n_pairs:]
    acc = None
    for a_ref, w_ref in zip(a_refs, w_refs):
        part = jnp.dot(a_ref[...], w_ref[...], preferred_element_type=F32)
        acc = part if acc is None else acc + part
    if epilogue == "plain":
        (o_ref,) = rest
        o_ref[...] = (acc * scale if scale != 1.0 else acc).astype(o_ref.dtype)
    elif epilogue == "rope":
        cos_ref, sin_ref, o_ref = rest
        cos, sin = cos_ref[...], sin_ref[...]
        for h in range(acc.shape[1] // LANES):
            sl = slice(h * LANES, (h + 1) * LANES)
            r = _rope(acc[:, sl], cos, sin, pair)
            o_ref[:, sl] = (r * scale if scale != 1.0 else r).astype(o_ref.dtype)
    else:
        x_ref, gate_ref, o_ref = rest
        o_ref[...] = x_ref[...] + gate_ref[...] * acc


def _matmul(a_list, w_list, *, out_dtype, epilogue="plain", extras=(), scale=1.0, pair=0, bm=1024, bn=1024):
    m = a_list[0].shape[0]
    n = w_list[0].shape[1]
    bm = _pick(m, bm, BF16_ROWS)
    bn = _pick(n, bn, LANES)
    in_specs = [pl.BlockSpec((bm, a.shape[1]), lambda j, i: (i, 0)) for a in a_list]
    in_specs += [pl.BlockSpec((w.shape[0], bn), lambda j, i: (0, j)) for w in w_list]
    if epilogue == "rope":
        in_specs += [pl.BlockSpec((bm, LANES), lambda j, i: (i, 0))] * 2
    elif epilogue == "resid":
        in_specs += [pl.BlockSpec((bm, bn), lambda j, i: (i, j)), pl.BlockSpec((1, bn), lambda j, i: (0, j))]
    return pl.pallas_call(
        functools.partial(_mm_kernel, n_pairs=len(a_list), epilogue=epilogue, scale=scale, pair=pair),
        out_shape=jax.ShapeDtypeStruct((m, n), out_dtype),
        grid=(n // bn, m // bm),
        in_specs=in_specs,
        out_specs=pl.BlockSpec((bm, bn), lambda j, i: (i, j)),
        compiler_params=_params(2, 56),
        name="mm_" + epilogue,
    )(*a_list, *w_list, *extras)


def _gqa_kernel(sink_ref, *refs, groups, tq, local, n_tokens):
    if local:
        q_ref, k_ref, v_ref, ky_ref, vy_ref, o_ref = refs
    else:
        q_ref, ky_ref, vy_ref, o_ref = refs
    hk = pl.program_id(0)
    nt = (((1,), (1,)), ((), ()))
    def with_ones(v):
        return jnp.concatenate([v, jnp.ones(v.shape, BF16)], axis=1)

    ky = ky_ref[...]
    vy = with_ones(vy_ref[...])
    if local:
        n = pl.program_id(1)
        win = tq + 2 * A_WINDOW
        start = pl.multiple_of(jnp.clip(n * tq - A_WINDOW, 0, n_tokens - win), BF16_ROWS)
        kw = k_ref[pl.ds(start, win), :]
        vw = with_ones(v_ref[pl.ds(start, win), :])
        qpos = n * tq + lax.broadcasted_iota(jnp.int32, (tq, win), 0)
        kpos = start + lax.broadcasted_iota(jnp.int32, (tq, win), 1)
        band = jnp.abs(kpos - qpos) <= A_WINDOW
    for gi in range(groups):
        sl = slice(gi * HEAD_DIM, (gi + 1) * HEAD_DIM)
        qg = q_ref[:, sl]
        sink = sink_ref[hk * groups + gi] * LOG2_E
        s_ctx = lax.dot_general(qg, ky, nt, preferred_element_type=F32)
        m = jnp.maximum(jnp.max(s_ctx, axis=1, keepdims=True), sink)
        if local:
            s_loc = jnp.where(band, lax.dot_general(qg, kw, nt, preferred_element_type=F32), NEG_INF)
            m = jnp.maximum(m, jnp.max(s_loc, axis=1, keepdims=True))
        acc = jnp.dot(jnp.exp2(s_ctx - m).astype(BF16), vy, preferred_element_type=F32)
        if local:
            acc = acc + jnp.dot(jnp.exp2(s_loc - m).astype(BF16), vw, preferred_element_type=F32)
        den = acc[:, HEAD_DIM:] + jnp.exp2(sink - m)
        o_ref[:, sl] = (acc[:, :HEAD_DIM] / den).astype(o_ref.dtype)


def _gqa(q, kx, vsrc_x, v_col0, ky, vsrc_y, sink, *, local):
    r, aq = q.shape
    hkv = ky.shape[1] // HEAD_DIM
    groups = aq // HEAD_DIM // hkv
    lc = ky.shape[0]
    gw = groups * HEAD_DIM
    tq = _pick(r, 256, BF16_ROWS)
    n_tokens = kx.shape[0] if local else 0
    if local:
        assert n_tokens >= tq + 2 * A_WINDOW
    q_spec = pl.BlockSpec((tq, gw), lambda h, i, s: (i, h))
    ctx_specs = [pl.BlockSpec((lc, HEAD_DIM), lambda h, i, s: (0, h)),
                 pl.BlockSpec((lc, HEAD_DIM), lambda h, i, s: (0, v_col0 + h))]
    if local:
        in_specs = [q_spec,
                    pl.BlockSpec((n_tokens, HEAD_DIM), lambda h, i, s: (0, h)),
                    pl.BlockSpec((n_tokens, HEAD_DIM), lambda h, i, s: (0, v_col0 + h))] + ctx_specs
        operands = (q, kx, vsrc_x, ky, vsrc_y)
    else:
        in_specs = [q_spec] + ctx_specs
        operands = (q, ky, vsrc_y)
    return pl.pallas_call(
        functools.partial(_gqa_kernel, groups=groups, tq=tq, local=local, n_tokens=n_tokens),
        out_shape=jax.ShapeDtypeStruct((r, aq), BF16),
        grid_spec=pltpu.PrefetchScalarGridSpec(
            num_scalar_prefetch=1, grid=(hkv, r // tq), in_specs=in_specs,
            out_specs=pl.BlockSpec((tq, gw), lambda h, i, s: (i, h))),
        compiler_params=_params(2, 48),
        name="gqa_local" if local else "gqa_ctx",
    )(sink, *operands)


CONV_HALO = 16
CONV_ROWS = 32
CONV_LANES = 256


def _conv_kernel(a_ref, g_ref, ap_ref, gp_ref, an_ref, gn_ref, dw_ref, lg_ref, lb_ref, o_ref, u_ref, z_ref,
                 *, taps, n_blocks):
    i = pl.program_id(0)
    tr, ch = z_ref.shape
    pad = (taps - 1) // 2

    def glu(a, g):
        return a.astype(F32) * _sigmoid(g.astype(F32))

    u_ref[0, 0:CONV_HALO, :] = glu(ap_ref[...], gp_ref[...]) * (i > 0).astype(F32)
    u_ref[0, CONV_HALO:CONV_HALO + tr, :] = glu(a_ref[...], g_ref[...])
    u_ref[0, CONV_HALO + tr:, :] = glu(an_ref[...], gn_ref[...]) * (i < n_blocks - 1).astype(F32)
    span = u_ref.shape[1] - SUBLANES
    for s in range(1, SUBLANES):
        u_ref[s, 0:span, :] = u_ref[0, s:s + span, :]

    def lane_chunk(cc, carry):
        c0 = pl.multiple_of(cc * CONV_LANES, CONV_LANES)
        for r0 in range(0, tr, CONV_ROWS):
            acc = jnp.zeros((CONV_ROWS, CONV_LANES), F32)
            for j in range(taps):
                shift = j - pad + CONV_HALO
                acc = acc + (u_ref[shift % SUBLANES, pl.ds(r0 + shift - shift % SUBLANES, CONV_ROWS),
                                   pl.ds(c0, CONV_LANES)]
                             * dw_ref[pl.ds(j, 1), pl.ds(c0, CONV_LANES)])
            z_ref[pl.ds(r0, CONV_ROWS), pl.ds(c0, CONV_LANES)] = acc
        return carry

    lax.fori_loop(0, ch // CONV_LANES, lane_chunk, 0)
    z = z_ref[...]
    mu = jnp.mean(z, axis=-1, keepdims=True)
    zc = z - mu
    var = jnp.mean(zc * zc, axis=-1, keepdims=True)
    yv = zc * lax.rsqrt(var + NORM_EPS) * lg_ref[...] + lb_ref[...]
    o_ref[...] = (yv * _sigmoid(yv)).astype(o_ref.dtype)


def _conformer_conv(agv, dw, ln_g, ln_b):
    r = agv.shape[0]
    taps, ch = dw.shape
    assert (taps - 1) // 2 < CONV_HALO and ch % CONV_LANES == 0
    tr = _pick(r, 256, CONV_ROWS)
    nb = r // tr
    hb = tr // CONV_HALO
    last = r // CONV_HALO - 1
    dwp = jnp.zeros((2 * CONV_HALO, ch), F32).at[:taps].set(dw)
    cur = lambda col: pl.BlockSpec((tr, ch), lambda i: (i, col))
    prv = lambda col: pl.BlockSpec((CONV_HALO, ch), lambda i: (jnp.maximum(i * hb - 1, 0), col))
    nxt = lambda col: pl.BlockSpec((CONV_HALO, ch), lambda i: (jnp.minimum((i + 1) * hb, last), col))
    vec = pl.BlockSpec((1, ch), lambda i: (0, 0))
    return pl.pallas_call(
        functools.partial(_conv_kernel, taps=taps, n_blocks=nb),
        out_shape=jax.ShapeDtypeStruct((r, ch), BF16),
        grid=(nb,),
        in_specs=[cur(0), cur(1), prv(0), prv(1), nxt(0), nxt(1),
                  pl.BlockSpec((2 * CONV_HALO, ch), lambda i: (0, 0)), vec, vec],
        out_specs=pl.BlockSpec((tr, ch), lambda i: (i, 0)),
        scratch_shapes=[pltpu.VMEM((SUBLANES, tr + 2 * CONV_HALO, ch), F32), pltpu.VMEM((tr, ch), F32)],
        compiler_params=_params(1, 48),
        name="conformer_conv",
    )(agv, agv, agv, agv, agv, agv, dwp, ln_g.reshape(1, ch), ln_b.reshape(1, ch))


def _mla_down_kernel(h_ref, w_ref, qg_ref, kg_ref, cos_ref, sin_ref, q_ref, kv_ref, kr_ref, *, q_rank, kv_rank,
                     pair, use_rope):
    acc = jnp.dot(h_ref[...], w_ref[...], preferred_element_type=F32)

    def rms(v, g):
        return v * lax.rsqrt(jnp.mean(v * v, axis=-1, keepdims=True) + NORM_EPS) * g

    q_ref[...] = rms(acc[:, :q_rank], qg_ref[...]).astype(q_ref.dtype)
    kv_ref[...] = rms(acc[:, q_rank:q_rank + kv_rank], kg_ref[...]).astype(kv_ref.dtype)
    kr = acc[:, q_rank + kv_rank:]
    if use_rope:
        kr = _rope(kr, cos_ref[...], sin_ref[...], pair)
    kr_ref[...] = kr.astype(kr_ref.dtype)


def _mla_down(h, w_dn_p, q_g, kv_g, cos, sin, *, pair, use_rope):
    m, d = h.shape
    q_rank, kv_rank = q_g.shape[0], kv_g.shape[0]
    wn = w_dn_p.shape[1]
    assert wn == q_rank + kv_rank + LANES
    bm = _pick(m, 512, BF16_ROWS)
    row = lambda w: pl.BlockSpec((bm, w), lambda i: (i, 0))
    return pl.pallas_call(
        functools.partial(_mla_down_kernel, q_rank=q_rank, kv_rank=kv_rank, pair=pair, use_rope=use_rope),
        out_shape=(jax.ShapeDtypeStruct((m, q_rank), BF16), jax.ShapeDtypeStruct((m, kv_rank), BF16),
                   jax.ShapeDtypeStruct((m, LANES), BF16)),
        grid=(m // bm,),
        in_specs=[row(d), pl.BlockSpec((d, wn), lambda i: (0, 0)),
                  pl.BlockSpec((1, q_rank), lambda i: (0, 0)), pl.BlockSpec((1, kv_rank), lambda i: (0, 0)),
                  row(LANES), row(LANES)],
        out_specs=(row(q_rank), row(kv_rank), row(LANES)),
        compiler_params=_params(1, 56),
        name="mla_down",
    )(h, w_dn_p, q_g.reshape(1, q_rank), kv_g.reshape(1, kv_rank), cos, sin)


MLA_TQ = 1024
MLA_SUB = 256
MLA_KEY_CHUNK = 1280
MLA_STAGES = 12


def _mla_attn_kernel(qn_ref, qr_ref, kn_ref, kr_ref, v_ref, o_ref, kcat_ref, vcat_ref, qcat_ref, s_ref, *, tk, sub):
    nk = kcat_ref.shape[0]
    tq = qn_ref.shape[0]
    n_sub = tq // sub

    @pl.when(pl.program_id(1) == 0)
    def _():
        kcat_ref[:, :HEAD_DIM] = kn_ref[...]
        kcat_ref[:, HEAD_DIM:] = kr_ref[...]
        vcat_ref[:, :HEAD_DIM] = v_ref[...]
        vcat_ref[:, HEAD_DIM:] = jnp.ones((nk, HEAD_DIM), BF16)

    qcat_ref[:, :HEAD_DIM] = qn_ref[...]
    qcat_ref[:, HEAD_DIM:] = qr_ref[...]
    nt = (((1,), (1,)), ((), ()))
    n_chunks = nk // tk

    def scores(buf, u, off):
        s_ref[buf, u] = lax.dot_general(qcat_ref[u * sub:(u + 1) * sub, :], kcat_ref[pl.ds(off, tk), :], nt,
                                        preferred_element_type=F32)

    def absorb(buf, u, off, m, acc):
        s = s_ref[buf, u]
        m_new = jnp.maximum(m, jnp.max(s, axis=1, keepdims=True))
        p = jnp.exp2(s - m_new).astype(BF16)
        acc = jnp.exp2(m - m_new) * acc + jnp.dot(p, vcat_ref[pl.ds(off, tk), :], preferred_element_type=F32)
        return m_new, acc

    def stage(buf, off, carry):
        out = []
        for u in range(n_sub):
            out.append(absorb(buf, u, off, *carry[u]))
            scores(1 - buf, u, off + tk)
        return tuple(out)

    def trip(j, carry):
        off = pl.multiple_of(j * (MLA_STAGES * tk), tk)
        for k in range(MLA_STAGES):
            carry = stage(k % 2, off + k * tk, carry)
        return carry

    for u in range(n_sub):
        scores(0, u, 0)
    carry = tuple((jnp.full((sub, 1), NEG_INF, F32), jnp.zeros((sub, 2 * HEAD_DIM), F32)) for _ in range(n_sub))
    n_stages = n_chunks - 1
    carry = lax.fori_loop(0, n_stages // MLA_STAGES, trip, carry)
    for k in range(n_stages - n_stages % MLA_STAGES, n_stages):
        carry = stage(k % 2, k * tk, carry)
    for u in range(n_sub):
        _, acc = absorb(n_stages % 2, u, n_stages * tk, *carry[u])
        o_ref[u * sub:(u + 1) * sub, :] = (acc[:, :HEAD_DIM] / acc[:, HEAD_DIM:]).astype(o_ref.dtype)


def _mla_attn(qn, qr, kv, kr, *, n_heads, key_rows, key_block):
    r = qn.shape[0]
    tq = _pick(r, MLA_TQ, MLA_SUB) if r >= MLA_SUB else r
    sub = min(MLA_SUB, tq)
    tk = _pick(key_rows, MLA_KEY_CHUNK, LANES)
    once = pl.Buffered(1)
    kspec = lambda col0: pl.BlockSpec((key_rows, HEAD_DIM), lambda h, i: (key_block, col0 + h), pipeline_mode=once)
    qspec = pl.BlockSpec((tq, HEAD_DIM), lambda h, i: (i, h))
    return pl.pallas_call(
        functools.partial(_mla_attn_kernel, tk=tk, sub=sub),
        out_shape=jax.ShapeDtypeStruct((r, n_heads * HEAD_DIM), BF16),
        grid=(n_heads, r // tq),
        in_specs=[qspec, qspec, kspec(0),
                  pl.BlockSpec((key_rows, LANES), lambda h, i: (key_block, 0), pipeline_mode=once), kspec(n_heads)],
        out_specs=pl.BlockSpec((tq, HEAD_DIM), lambda h, i: (i, h)),
        scratch_shapes=[pltpu.VMEM((key_rows, 2 * HEAD_DIM), BF16), pltpu.VMEM((key_rows, 2 * HEAD_DIM), BF16),
                        pltpu.VMEM((tq, 2 * HEAD_DIM), BF16), pltpu.VMEM((2, tq // sub, sub, tk), F32)],
        compiler_params=_params(2, 60),
        name="mla_attn",
    )(qn, qr, kv, kr, kv)


def _router_kernel(x_ref, g_ref, sh_ref, sc_ref, wt_ref, aff_ref):
    h = _norm_mod(x_ref[...], g_ref[...], sh_ref[...], sc_ref[...])
    w = wt_ref[...]
    ne = w.shape[0]
    h_hi = h.astype(BF16)
    h_lo = (h - h_hi.astype(F32)).astype(BF16)
    w_hi = w.astype(BF16)
    w_lo = (w - w_hi.astype(F32)).astype(BF16)
    nt = (((1,), (1,)), ((), ()))
    both = lax.dot_general(jnp.concatenate([w_hi, w_lo], axis=0), h_hi, nt, preferred_element_type=F32)
    logits = both[:ne] + both[ne:] + lax.dot_general(w_hi, h_lo, nt, preferred_element_type=F32)
    e = jnp.exp(logits - jnp.max(logits, axis=0, keepdims=True))
    aff_ref[...] = e / jnp.sum(e, axis=0, keepdims=True)


def _router(xr, g, shift, scale, w_router_t):
    r, d = xr.shape
    ne = w_router_t.shape[0]
    tr = _pick(r, 512, LANES)
    vec = pl.BlockSpec((1, d), lambda i: (0, 0))
    return pl.pallas_call(
        _router_kernel,
        out_shape=jax.ShapeDtypeStruct((ne, r), F32),
        grid=(r // tr,),
        in_specs=[pl.BlockSpec((tr, d), lambda i: (i, 0)), vec, vec, vec, pl.BlockSpec((ne, d), lambda i: (0, 0))],
        out_specs=pl.BlockSpec((ne, tr), lambda i: (0, i)),
        compiler_params=_params(1, 48),
        name="router",
    )(xr, g.reshape(1, d), shift, scale, w_router_t)


def _lane_cumsum(v):
    lane = lax.broadcasted_iota(jnp.int32, v.shape, 1)
    k = 1
    while k < LANES:
        v = v + jnp.where(lane >= k, pltpu.roll(v, k, 1), 0)
        k *= 2
    return v


def _select_kernel(a_ref, idx_ref, thr_ref, *, cap):
    a = a_ref[...]
    ne, g, _ = a.shape
    n_slots = idx_ref.shape[2]
    bits = lax.bitcast_convert_type(a, jnp.int32)

    def count(mask):
        return jnp.sum(jnp.sum(mask.astype(F32), axis=2, keepdims=True), axis=1, keepdims=True)

    thr = jnp.zeros((ne, 1, 1), jnp.int32)
    for b in range(30, -1, -1):
        cand = thr | (1 << b)
        thr = jnp.where(count(bits >= cand) >= cap, cand, thr)
    thr_ref[...] = jnp.broadcast_to(thr, thr_ref.shape)
    before = lax.broadcasted_iota(jnp.int32, (g, g), 1) < lax.broadcasted_iota(jnp.int32, (g, g), 0)
    upto = lax.broadcasted_iota(jnp.int32, (g, g), 1) <= lax.broadcasted_iota(jnp.int32, (g, g), 0)
    row_id = lax.broadcasted_iota(jnp.int32, (g, n_slots), 0).astype(F32)
    slot = lax.broadcasted_iota(jnp.int32, (1, n_slots), 1).astype(F32)

    def over_rows(tri, per_row):
        return jnp.dot(tri.astype(F32), jnp.broadcast_to(per_row, (g, LANES)), precision=lax.Precision.HIGHEST,
                       preferred_element_type=F32)[:, 0:1]

    def per_expert(e, carry):
        bits_e = lax.bitcast_convert_type(a_ref[e], jnp.int32)
        thr_e = thr_ref[e][0:1, 0:1]
        above = bits_e > thr_e
        tie = bits_e == thr_e
        need = cap - jnp.sum(jnp.sum(above.astype(F32), axis=1, keepdims=True), axis=0, keepdims=True)
        tie_i = tie.astype(jnp.int32)
        incl = _lane_cumsum(tie_i)
        rank = (incl - tie_i).astype(F32) + over_rows(before, incl[:, LANES - 1:LANES].astype(F32))
        sel = above | (tie & (rank < need))
        cum = _lane_cumsum(sel.astype(jnp.int32)).astype(F32)
        row_cnt = cum[:, LANES - 1:LANES]
        row_end = over_rows(upto, row_cnt)
        g_of = jnp.sum((row_end <= slot).astype(F32), axis=0, keepdims=True)
        onehot = (row_id == g_of).astype(F32)
        start_of = jnp.sum(onehot * (row_end - row_cnt), axis=0, keepdims=True)
        cum_of = lax.dot_general(cum, onehot, (((0,), (0,)), ((), ())), preferred_element_type=F32)
        lane_of = jnp.sum((cum_of <= slot - start_of).astype(F32), axis=0, keepdims=True)
        idx_ref[e] = (g_of * LANES + lane_of).astype(jnp.int32)
        return carry

    lax.fori_loop(0, ne, per_expert, 0)


def _select(aff_t, cap):
    ne, r = aff_t.shape
    assert r % LANES == 0
    n_pad = max(r, SUBLANES * LANES)
    g = n_pad // LANES
    a3 = jnp.pad(aff_t, ((0, 0), (0, n_pad - r))).reshape(ne, g, LANES)
    n_slots = max(cap, LANES)
    idx = pl.pallas_call(
        functools.partial(_select_kernel, cap=cap),
        out_shape=jax.ShapeDtypeStruct((ne, 1, n_slots), jnp.int32),
        grid=(1,),
        in_specs=[pl.BlockSpec((ne, g, LANES), lambda i: (0, 0, 0))],
        out_specs=pl.BlockSpec((ne, 1, n_slots), lambda i: (0, 0, 0)),
        scratch_shapes=[pltpu.VMEM((ne, SUBLANES, LANES), jnp.int32)],
        compiler_params=_params(1, 32),
        name="ec_select",
    )(a3)[:, 0, :cap]
    return idx, jnp.take_along_axis(aff_t, idx, axis=1)


def _row_copy(src_hbm, buf, sem, slot, tok, r):
    return pltpu.make_async_copy(src_hbm.at[pl.ds(tok, 1), :], buf.at[slot, pl.ds(r, 1), :], sem.at[slot])


def _start_gather(idx_ref, src_hbm, buf, sem, step, slot, rows):
    for r in range(rows):
        _row_copy(src_hbm, buf, sem, slot, idx_ref[step * rows + r], r).start()


def _for_slot(slot, fn, n_slots=2):
    for s in range(n_slots):
        pl.when(slot == s)(functools.partial(fn, s))


def _wait_rows(src_hbm, buf, sem, slot, rows):
    for _ in range(rows):
        _row_copy(src_hbm, buf, sem, slot, 0, 0).wait()


def _moe_ffn_kernel(idx_ref, gate_ref, g_ref, sh_ref, sc_ref, wgu_ref, wd_ref, x_hbm, y_ref, xbuf, sem,
                    *, rows, n_steps):
    step = pl.program_id(0) * pl.num_programs(1) + pl.program_id(1)
    slot = step % 2

    @pl.when(step == 0)
    def _():
        _start_gather(idx_ref, x_hbm, xbuf, sem, 0, 0, rows)

    @pl.when(step + 1 < n_steps)
    def _():
        _for_slot(1 - slot, lambda s: _start_gather(idx_ref, x_hbm, xbuf, sem, step + 1, s, rows))

    _wait_rows(x_hbm, xbuf, sem, slot, rows)
    h = _norm_mod(xbuf[slot], g_ref[...], sh_ref[...], sc_ref[...]).astype(BF16)
    ff = wd_ref.shape[0]
    halves = []
    for hh in (h[:rows // 2], h[rows // 2:]):
        au = jnp.dot(hh, wgu_ref[...], preferred_element_type=F32)
        a, u = au[:, :ff], au[:, ff:]
        halves.append((a * _sigmoid(a) * u).astype(BF16))
    z = jnp.concatenate(halves, axis=0)
    y_ref[...] = jnp.dot(z, wd_ref[...], preferred_element_type=F32) * gate_ref[...]


def _moe_ffn(idx_flat, gate_col, xr, g, shift, scale, wgu, wd, layer, *, rows):
    r, d = xr.shape
    _, ne, ff, _ = wd.shape
    assert ff % LANES == 0
    slots = idx_flat.shape[0]
    nc = slots // ne // rows
    vec = pl.BlockSpec((1, d), lambda e, c, idx: (0, 0))
    return pl.pallas_call(
        functools.partial(_moe_ffn_kernel, rows=rows, n_steps=ne * nc),
        out_shape=jax.ShapeDtypeStruct((slots, d), F32),
        grid_spec=pltpu.PrefetchScalarGridSpec(
            num_scalar_prefetch=1, grid=(ne, nc),
            in_specs=[pl.BlockSpec((rows, 1), lambda e, c, idx: (e * nc + c, 0)), vec, vec, vec,
                      pl.BlockSpec((None, None, d, 2 * ff), lambda e, c, idx: (layer, e, 0, 0)),
                      pl.BlockSpec((None, None, ff, d), lambda e, c, idx: (layer, e, 0, 0)),
                      pl.BlockSpec(memory_space=pl.ANY)],
            out_specs=pl.BlockSpec((rows, d), lambda e, c, idx: (e * nc + c, 0)),
            scratch_shapes=[pltpu.VMEM((2, rows, d), F32), pltpu.SemaphoreType.DMA((2,))]),
        compiler_params=_params(2, 56),
        name="moe_ffn",
    )(idx_flat, gate_col, g.reshape(1, d), shift, scale, wgu, wd, xr)


MOE_ROWS = 256
COMBINE_BUFS = 3


def _moe_combine_kernel(idx_ref, y_ref, gate_ref, x_hbm, o_hbm, abuf, gsem, ssem, *, rows, n_steps, nc):
    del x_hbm
    c = pl.program_id(1)
    step = pl.program_id(0) * nc + c
    slot = step % COMBINE_BUFS
    nxt = (step + 1) % COMBINE_BUFS
    prv = (step + 2) % COMBINE_BUFS

    def scatter_copy(s, sl, r):
        return pltpu.make_async_copy(abuf.at[sl, pl.ds(r, 1), :], o_hbm.at[pl.ds(idx_ref[s * rows + r], 1), :],
                                     ssem.at[sl])

    def wait_scatter(sl):
        for _ in range(rows):
            pltpu.make_async_copy(abuf.at[sl, pl.ds(0, 1), :], o_hbm.at[pl.ds(0, 1), :], ssem.at[sl]).wait()

    def gather_next():
        _for_slot(nxt, lambda s: _start_gather(idx_ref, o_hbm, abuf, gsem, step + 1, s, rows), COMBINE_BUFS)

    @pl.when(step == 0)
    def _():
        _start_gather(idx_ref, o_hbm, abuf, gsem, 0, 0, rows)

    @pl.when(c < nc - 1)
    def _():
        @pl.when(c >= 2)
        def _():
            wait_scatter(nxt)
        gather_next()

    _wait_rows(o_hbm, abuf, gsem, slot, rows)
    abuf[slot] = abuf[slot] + gate_ref[...] * y_ref[...]

    def start_scatter(s):
        for r in range(rows):
            scatter_copy(step, s, r).start()
    _for_slot(slot, start_scatter, COMBINE_BUFS)

    @pl.when(c == nc - 1)
    def _():
        if nc >= 3:
            wait_scatter(nxt)
        if nc >= 2:
            wait_scatter(prv)
        wait_scatter(slot)

        @pl.when(step + 1 < n_steps)
        def _():
            gather_next()


def _moe_combine(idx_flat, y, gate_vec, xr, *, n_experts, rows):
    r, d = xr.shape
    slots = idx_flat.shape[0]
    nc = slots // n_experts // rows
    return pl.pallas_call(
        functools.partial(_moe_combine_kernel, rows=rows, n_steps=n_experts * nc, nc=nc),
        out_shape=jax.ShapeDtypeStruct((r, d), F32),
        grid_spec=pltpu.PrefetchScalarGridSpec(
            num_scalar_prefetch=1, grid=(n_experts, nc),
            in_specs=[pl.BlockSpec((rows, d), lambda e, c, idx: (e * nc + c, 0)),
                      pl.BlockSpec((1, d), lambda e, c, idx: (0, 0)),
                      pl.BlockSpec(memory_space=pl.ANY)],
            out_specs=pl.BlockSpec(memory_space=pl.ANY),
            scratch_shapes=[pltpu.VMEM((COMBINE_BUFS, rows, d), F32), pltpu.SemaphoreType.DMA((COMBINE_BUFS,)),
                            pltpu.SemaphoreType.DMA((COMBINE_BUFS,))]),
        input_output_aliases={3: 0},
        compiler_params=_params(2, 48),
        name="moe_combine",
    )(idx_flat, y, gate_vec, xr)


def _ec_moe(xr, g, shift, scale, out_gate, w_router_t, wgu, wd, layer):
    r = xr.shape[0]
    ne = w_router_t.shape[0]
    cap = EC_CAPACITY * r // ne
    rows = _pick(cap, MOE_ROWS, 2 * BF16_ROWS)
    aff_t = _router(xr, g, shift, scale, w_router_t)
    idx, gate = _select(aff_t, cap)
    idx_flat = idx.reshape(-1)
    y = _moe_ffn(idx_flat, gate.reshape(-1, 1), xr, g, shift, scale, wgu, wd, layer, rows=rows)
    return _moe_combine(idx_flat, y, out_gate, xr, n_experts=ne, rows=rows)


def _rope_tables(n, dim):
    half = dim // 2
    inv = ROPE_BASE ** (-jnp.arange(0, half, 2, dtype=F32) / half)
    rows = jnp.repeat(jnp.arange(n // GRID_W, dtype=F32), GRID_W)
    cols = (jnp.arange(n) % GRID_W).astype(F32)
    ar = rows[:, None] * inv[None, :]
    ac = cols[:, None] * inv[None, :]
    cos = jnp.concatenate([jnp.cos(ar), jnp.cos(ar), jnp.cos(ac), jnp.cos(ac)], axis=1)
    sin = jnp.concatenate([-jnp.sin(ar), jnp.sin(ar), -jnp.sin(ac), jnp.sin(ac)], axis=1)
    if dim < LANES:
        cos = jnp.concatenate([cos, jnp.ones((n, LANES - dim), F32)], axis=1)
        sin = jnp.concatenate([sin, jnp.zeros((n, LANES - dim), F32)], axis=1)
    return cos, sin


def _even_mixer(hx, hy, x, y, gate_x, gate_y, w_in, sink, dw, ln_g, ln_b, w_out, tables, need_ctx):
    ch = dw.shape[1]
    aq = w_out.shape[0] - ch
    akv = (w_in.shape[1] - aq - 2 * ch) // 2
    cos, sin = tables
    scale = HEAD_DIM ** -0.5 * LOG2_E
    pair = HEAD_DIM // 4
    wq = w_in[:, :aq].astype(BF16)
    wk = w_in[:, aq:aq + akv].astype(BF16)
    w_agv = jnp.concatenate([w_in[:, aq + 2 * akv:], w_in[:, aq + akv:aq + 2 * akv]], axis=1).astype(BF16)
    w_o1 = w_out[:aq].astype(BF16)
    w_o2 = w_out[aq:].astype(BF16)
    v_col0 = 2 * ch // HEAD_DIM

    qx = _matmul([hx], [wq], out_dtype=BF16, epilogue="rope", extras=(cos, sin), scale=scale, pair=pair)
    kx = _matmul([hx], [wk], out_dtype=BF16, epilogue="rope", extras=(cos, sin), pair=pair)
    agv_x = _matmul([hx], [w_agv], out_dtype=BF16)
    ky = _matmul([hy], [wk], out_dtype=BF16)
    agv_y = _matmul([hy], [w_agv], out_dtype=BF16)

    att_x = _gqa(qx, kx, agv_x, v_col0, ky, agv_y, sink, local=True)
    conv_x = _conformer_conv(agv_x, dw, ln_g, ln_b)
    x = _matmul([att_x, conv_x], [w_o1, w_o2], out_dtype=F32, epilogue="resid", extras=(x, gate_x), bn=1024)
    if need_ctx:
        qy = _matmul([hy], [wq], out_dtype=BF16, scale=scale)
        att_y = _gqa(qy, None, None, v_col0, ky, agv_y, sink, local=False)
        conv_y = _conformer_conv(agv_y, dw, ln_g, ln_b)
        y = _matmul([att_y, conv_y], [w_o1, w_o2], out_dtype=F32, epilogue="resid", extras=(y, gate_y), bn=512)
    return x, y


def _mla_mixer(hx, hy, x, y, gate_x, gate_y, w_dn, q_g, kv_g, w_uq, w_ukv, w_o, tables, need_ctx):
    n, d = hx.shape
    lc = hy.shape[0]
    q_rank, kv_rank = q_g.shape[0], kv_g.shape[0]
    n_heads = w_o.shape[0] // HEAD_DIM
    rope_dim = w_dn.shape[1] - q_rank - kv_rank
    assert n % lc == 0 and rope_dim <= LANES
    cos, sin = tables
    pair = rope_dim // 4
    scale = (HEAD_DIM + rope_dim) ** -0.5 * LOG2_E
    w_dn_p = jnp.concatenate([w_dn, jnp.zeros((d, LANES - rope_dim), F32)], axis=1).astype(BF16)
    w_uq3 = w_uq.reshape(q_rank, n_heads, HEAD_DIM + rope_dim)
    w_qn = w_uq3[:, :, :HEAD_DIM].reshape(q_rank, n_heads * HEAD_DIM).astype(BF16)
    w_qr = jnp.concatenate([w_uq3[:, :, HEAD_DIM:], jnp.zeros((q_rank, n_heads, LANES - rope_dim), F32)],
                           axis=2).reshape(q_rank, n_heads * LANES).astype(BF16)
    w_kv3 = w_ukv.reshape(kv_rank, n_heads, 2 * HEAD_DIM)
    w_kv = jnp.concatenate([w_kv3[:, :, :HEAD_DIM].reshape(kv_rank, -1), w_kv3[:, :, HEAD_DIM:].reshape(kv_rank, -1)],
                           axis=1).astype(BF16)
    w_ob = w_o.astype(BF16)

    qlat_x, ckv_x, kr_x = _mla_down(hx, w_dn_p, q_g, kv_g, cos, sin, pair=pair, use_rope=True)
    qlat_y, ckv_y, kr_y = _mla_down(hy, w_dn_p, q_g, kv_g, cos[:lc], sin[:lc], pair=pair, use_rope=False)
    kv_all = _matmul([jnp.concatenate([ckv_x, ckv_y], axis=0)], [w_kv], out_dtype=BF16, bm=1280)
    kr_all = jnp.concatenate([kr_x, kr_y], axis=0)

    qn_x = _matmul([qlat_x], [w_qn], out_dtype=BF16, scale=scale)
    qr_x = _matmul([qlat_x], [w_qr], out_dtype=BF16, epilogue="rope", extras=(cos, sin), scale=scale, pair=pair)
    o_x = _mla_attn(qn_x, qr_x, kv_all, kr_all, n_heads=n_heads, key_rows=n + lc, key_block=0)
    x = _matmul([o_x], [w_ob], out_dtype=F32, epilogue="resid", extras=(x, gate_x), bn=1024)
    if need_ctx:
        qn_y = _matmul([qlat_y], [w_qn], out_dtype=BF16, scale=scale)
        qr_y = _matmul([qlat_y], [w_qr], out_dtype=BF16, scale=scale)
        o_y = _mla_attn(qn_y, qr_y, kv_all, kr_all, n_heads=n_heads, key_rows=lc, key_block=n // lc)
        y = _matmul([o_y], [w_ob], out_dtype=F32, epilogue="resid", extras=(y, gate_y), bn=512)
    return x, y


def kernel(x, c, ctx, c_ctx, ada_w, ada_b, norm1_g, norm2_g, ev_w_in, ev_sink, ev_dw, ev_ln_g, ev_ln_b, ev_w_out, od_w_dn, od_q_norm_g, od_kv_norm_g, od_w_uq, od_w_ukv, od_w_o, moe_router, moe_w_gate, moe_w_up, moe_w_down, final_g):
    b, n, d = x.shape
    assert b == 1 and n % GRID_W == 0
    depth = ada_w.shape[0]
    xr, yr = x[0], ctx[0]
    mods = _adaln(c, c_ctx, ada_w, ada_b)
    tables_a = _rope_tables(n, HEAD_DIM)
    rope_dim = od_w_dn.shape[2] - od_q_norm_g.shape[1] - od_kv_norm_g.shape[1]
    tables_c = _rope_tables(n, rope_dim)
    wgu = jnp.concatenate([moe_w_gate, moe_w_up], axis=3).astype(BF16)
    wd = moe_w_down.astype(BF16)
    for l in range(depth):
        need_ctx = l < depth - 1
        mx = [mods[l, 0:1, i * d:(i + 1) * d] for i in range(6)]
        my = [mods[l, 1:2, i * d:(i + 1) * d] for i in range(6)]
        hx = _normmod(xr, norm1_g[l], mx[0], mx[1], BF16)
        hy = _normmod(yr, norm1_g[l], my[0], my[1], BF16)
        i = l // 2
        if l % 2 == 0:
            xr, yr = _even_mixer(hx, hy, xr, yr, mx[2], my[2], ev_w_in[i], ev_sink[i], ev_dw[i], ev_ln_g[i],
                                 ev_ln_b[i], ev_w_out[i], tables_a, need_ctx)
        else:
            xr, yr = _mla_mixer(hx, hy, xr, yr, mx[2], my[2], od_w_dn[i], od_q_norm_g[i], od_kv_norm_g[i],
                                od_w_uq[i], od_w_ukv[i], od_w_o[i], tables_c, need_ctx)
        w_rt = moe_router[l].T
        xr = _ec_moe(xr, norm2_g[l], mx[3], mx[4], mx[5], w_rt, wgu, wd, l)
        if need_ctx:
            yr = _ec_moe(yr, norm2_g[l], my[3], my[4], my[5], w_rt, wgu, wd, l)
    zero = jnp.zeros((1, d), F32)
    return _normmod(xr, final_g, zero, zero, F32)[None]
```

```python
import functools

import jax
import jax.numpy as jnp
from jax import lax
from jax.experimental import pallas as pl
from jax.experimental.pallas import tpu as pltpu

F32 = jnp.float32
BF16 = jnp.bfloat16

GRID_W = 64
ROPE_BASE = 10000.0
NORM_EPS = 1e-6
NEG_INF = -1e30
HEAD_DIM = 128
A_WINDOW = 128
EC_CAPACITY = 2
LOG2_E = 1.4426950408889634

LANES = 128
SUBLANES = 8
BF16_ROWS = 16
MIB = 1024 * 1024


def _params(n_axes, vmem_mib):
    return pltpu.CompilerParams(dimension_semantics=("arbitrary",) * n_axes, vmem_limit_bytes=vmem_mib * MIB)


def _pick(total, target, mult):
    if total <= target:
        return total
    best = None
    for d in range(mult, target + 1, mult):
        if total % d == 0:
            best = d
    assert best is not None, (total, target, mult)
    return best


def _sigmoid(v):
    return 1.0 / (1.0 + jnp.exp(-v))


def _norm_mod(xv, g, shift, scale):
    yv = xv * lax.rsqrt(jnp.mean(xv * xv, axis=-1, keepdims=True) + NORM_EPS)
    return (yv * g) * (1.0 + scale) + shift


def _rope(xv, cos, sin, pair):
    lane = lax.broadcasted_iota(jnp.int32, xv.shape, 1)
    first = (lane % (2 * pair)) < pair
    partner = jnp.where(first, pltpu.roll(xv, LANES - pair, 1), pltpu.roll(xv, pair, 1))
    return xv * cos + partner * sin


def _adaln_kernel(cc_ref, w_ref, b_ref, o_ref):
    cc = cc_ref[...]
    s = cc * _sigmoid(cc)
    w = w_ref[...]
    b = b_ref[...]
    r0 = jnp.sum(s[:, 0:1] * w, axis=0, keepdims=True) + b
    r1 = jnp.sum(s[:, 1:2] * w, axis=0, keepdims=True) + b
    o_ref[...] = jnp.concatenate([r0, r1, jnp.zeros((SUBLANES - 2, w.shape[1]), F32)], axis=0)


def _adaln(c, c_ctx, ada_w, ada_b):
    depth, d, n6 = ada_w.shape
    cc = jnp.zeros((d, SUBLANES), F32).at[:, 0].set(c[0]).at[:, 1].set(c_ctx)
    tn = _pick(n6, 512, LANES)
    return pl.pallas_call(
        _adaln_kernel,
        out_shape=jax.ShapeDtypeStruct((depth, SUBLANES, n6), F32),
        grid=(depth, n6 // tn),
        in_specs=[pl.BlockSpec((d, SUBLANES), lambda l, j: (0, 0)),
                  pl.BlockSpec((None, d, tn), lambda l, j: (l, 0, j)),
                  pl.BlockSpec((None, 1, tn), lambda l, j: (l, 0, j))],
        out_specs=pl.BlockSpec((None, SUBLANES, tn), lambda l, j: (l, 0, j)),
        compiler_params=_params(2, 48),
        name="adaln",
    )(cc, ada_w, ada_b.reshape(depth, 1, n6))


def _normmod_kernel(x_ref, g_ref, sh_ref, sc_ref, o_ref):
    o_ref[...] = _norm_mod(x_ref[...], g_ref[...], sh_ref[...], sc_ref[...]).astype(o_ref.dtype)


def _normmod(xr, g, shift, scale, out_dtype):
    r, d = xr.shape
    tr = _pick(r, 512, BF16_ROWS)
    vec = pl.BlockSpec((1, d), lambda i: (0, 0))
    return pl.pallas_call(
        _normmod_kernel,
        out_shape=jax.ShapeDtypeStruct((r, d), out_dtype),
        grid=(r // tr,),
        in_specs=[pl.BlockSpec((tr, d), lambda i: (i, 0)), vec, vec, vec],
        out_specs=pl.BlockSpec((tr, d), lambda i: (i, 0)),
        compiler_params=_params(1, 48),
        name="normmod",
    )(xr, g.reshape(1, d), shift, scale)


def _mm_kernel(*refs, n_pairs, epilogue, scale, pair):
    a_refs, w_refs, rest = refs[:n_pairs], refs[n_pairs:2 * n_pairs], refs[2 * n_pairs:]
    acc = None
    for a_ref, w_ref in zip(a_refs, w_refs):
        part = jnp.dot(a_ref[...], w_ref[...], preferred_element_type=F32)
        acc = part if acc is None else acc + part
    if epilogue == "plain":
        (o_ref,) = rest
        o_ref[...] = (acc * scale if scale != 1.0 else acc).astype(o_ref.dtype)
    elif epilogue == "rope":
        cos_ref, sin_ref, o_ref = rest
        cos, sin = cos_ref[...], sin_ref[...]
        for h in range(acc.shape[1] // LANES):
            sl = slice(h * LANES, (h + 1) * LANES)
            r = _rope(acc[:, sl], cos, sin, pair)
            o_ref[:, sl] = (r * scale if scale != 1.0 else r).astype(o_ref.dtype)
    else:
        x_ref, gate_ref, o_ref = rest
        o_ref[...] = x_ref[...] + gate_ref[...] * acc


def _matmul(a_list, w_list, *, out_dtype, epilogue="plain", extras=(), scale=1.0, pair=0, bm=1024, bn=1024):
    m = a_list[0].shape[0]
    n = w_list[0].shape[1]
    bm = _pick(m, bm, BF16_ROWS)
    bn = _pick(n, bn, LANES)
    in_specs = [pl.BlockSpec((bm, a.shape[1]), lambda j, i: (i, 0)) for a in a_list]
    in_specs += [pl.BlockSpec((w.shape[0], bn), lambda j, i: (0, j)) for w in w_list]
    if epilogue == "rope":
        in_specs += [pl.BlockSpec((bm, LANES), lambda j, i: (i, 0))] * 2
    elif epilogue == "resid":
        in_specs += [pl.BlockSpec((bm, bn), lambda j, i: (i, j)), pl.BlockSpec((1, bn), lambda j, i: (0, j))]
    return pl.pallas_call(
        functools.partial(_mm_kernel, n_pairs=len(a_list), epilogue=epilogue, scale=scale, pair=pair),
        out_shape=jax.ShapeDtypeStruct((m, n), out_dtype),
        grid=(n // bn, m // bm),
        in_specs=in_specs,
        out_specs=pl.BlockSpec((bm, bn), lambda j, i: (i, j)),
        compiler_params=_params(2, 56),
        name="mm_" + epilogue,
    )(*a_list, *w_list, *extras)


def _gqa_kernel(sink_ref, *refs, groups, tq, local, n_tokens):
    if local:
        q_ref, k_ref, v_ref, ky_ref, vy_ref, o_ref = refs
    else:
        q_ref, ky_ref, vy_ref, o_ref = refs
    hk = pl.program_id(0)
    nt = (((1,), (1,)), ((), ()))
    def with_ones(v):
        return jnp.concatenate([v, jnp.ones(v.shape, BF16)], axis=1)

    ky = ky_ref[...]
    vy = with_ones(vy_ref[...])
    if local:
        n = pl.program_id(1)
        win = tq + 2 * A_WINDOW
        start = pl.multiple_of(jnp.clip(n * tq - A_WINDOW, 0, n_tokens - win), BF16_ROWS)
        kw = k_ref[pl.ds(start, win), :]
        vw = with_ones(v_ref[pl.ds(start, win), :])
        qpos = n * tq + lax.broadcasted_iota(jnp.int32, (tq, win), 0)
        kpos = start + lax.broadcasted_iota(jnp.int32, (tq, win), 1)
        band = jnp.abs(kpos - qpos) <= A_WINDOW
    for gi in range(groups):
        sl = slice(gi * HEAD_DIM, (gi + 1) * HEAD_DIM)
        qg = q_ref[:, sl]
        sink = sink_ref[hk * groups + gi] * LOG2_E
        s_ctx = lax.dot_general(qg, ky, nt, preferred_element_type=F32)
        m = jnp.maximum(jnp.max(s_ctx, axis=1, keepdims=True), sink)
        if local:
            s_loc = jnp.where(band, lax.dot_general(qg, kw, nt, preferred_element_type=F32), NEG_INF)
            m = jnp.maximum(m, jnp.max(s_loc, axis=1, keepdims=True))
        acc = jnp.dot(jnp.exp2(s_ctx - m).astype(BF16), vy, preferred_element_type=F32)
        if local:
            acc = acc + jnp.dot(jnp.exp2(s_loc - m).astype(BF16), vw, preferred_element_type=F32)
        den = acc[:, HEAD_DIM:] + jnp.exp2(sink - m)
        o_ref[:, sl] = (acc[:, :HEAD_DIM] / den).astype(o_ref.dtype)


def _gqa(q, kx, vsrc_x, v_col0, ky, vsrc_y, sink, *, local):
    r, aq = q.shape
    hkv = ky.shape[1] // HEAD_DIM
    groups = aq // HEAD_DIM // hkv
    lc = ky.shape[0]
    gw = groups * HEAD_DIM
    tq = _pick(r, 256, BF16_ROWS)
    n_tokens = kx.shape[0] if local else 0
    if local:
        assert n_tokens >= tq + 2 * A_WINDOW
    q_spec = pl.BlockSpec((tq, gw), lambda h, i, s: (i, h))
    ctx_specs = [pl.BlockSpec((lc, HEAD_DIM), lambda h, i, s: (0, h)),
                 pl.BlockSpec((lc, HEAD_DIM), lambda h, i, s: (0, v_col0 + h))]
    if local:
        in_specs = [q_spec,
                    pl.BlockSpec((n_tokens, HEAD_DIM), lambda h, i, s: (0, h)),
                    pl.BlockSpec((n_tokens, HEAD_DIM), lambda h, i, s: (0, v_col0 + h))] + ctx_specs
        operands = (q, kx, vsrc_x, ky, vsrc_y)
    else:
        in_specs = [q_spec] + ctx_specs
        operands = (q, ky, vsrc_y)
    return pl.pallas_call(
        functools.partial(_gqa_kernel, groups=groups, tq=tq, local=local, n_tokens=n_tokens),
        out_shape=jax.ShapeDtypeStruct((r, aq), BF16),
        grid_spec=pltpu.PrefetchScalarGridSpec(
            num_scalar_prefetch=1, grid=(hkv, r // tq), in_specs=in_specs,
            out_specs=pl.BlockSpec((tq, gw), lambda h, i, s: (i, h))),
        compiler_params=_params(2, 48),
        name="gqa_local" if local else "gqa_ctx",
    )(sink, *operands)


CONV_HALO = 16
CONV_ROWS = 32
CONV_LANES = 256


def _conv_kernel(a_ref, g_ref, ap_ref, gp_ref, an_ref, gn_ref, dw_ref, lg_ref, lb_ref, o_ref, u_ref, z_ref,
                 *, taps, n_blocks):
    i = pl.program_id(0)
    tr, ch = z_ref.shape
    pad = (taps - 1) // 2

    def glu(a, g):
        return a.astype(F32) * _sigmoid(g.astype(F32))

    u_ref[0, 0:CONV_HALO, :] = glu(ap_ref[...], gp_ref[...]) * (i > 0).astype(F32)
    u_ref[0, CONV_HALO:CONV_HALO + tr, :] = glu(a_ref[...], g_ref[...])
    u_ref[0, CONV_HALO + tr:, :] = glu(an_ref[...], gn_ref[...]) * (i < n_blocks - 1).astype(F32)
    span = u_ref.shape[1] - SUBLANES
    for s in range(1, SUBLANES):
        u_ref[s, 0:span, :] = u_ref[0, s:s + span, :]

    def lane_chunk(cc, carry):
        c0 = pl.multiple_of(cc * CONV_LANES, CONV_LANES)
        for r0 in range(0, tr, CONV_ROWS):
            acc = jnp.zeros((CONV_ROWS, CONV_LANES), F32)
            for j in range(taps):
                shift = j - pad + CONV_HALO
                acc = acc + (u_ref[shift % SUBLANES, pl.ds(r0 + shift - shift % SUBLANES, CONV_ROWS),
                                   pl.ds(c0, CONV_LANES)]
                             * dw_ref[pl.ds(j, 1), pl.ds(c0, CONV_LANES)])
            z_ref[pl.ds(r0, CONV_ROWS), pl.ds(c0, CONV_LANES)] = acc
        return carry

    lax.fori_loop(0, ch // CONV_LANES, lane_chunk, 0)
    z = z_ref[...]
    mu = jnp.mean(z, axis=-1, keepdims=True)
    zc = z - mu
    var = jnp.mean(zc * zc, axis=-1, keepdims=True)
    yv = zc * lax.rsqrt(var + NORM_EPS) * lg_ref[...] + lb_ref[...]
    o_ref[...] = (yv * _sigmoid(yv)).astype(o_ref.dtype)


def _conformer_conv(agv, dw, ln_g, ln_b):
    r = agv.shape[0]
    taps, ch = dw.shape
    assert (taps - 1) // 2 < CONV_HALO and ch % CONV_LANES == 0
    tr = _pick(r, 256, CONV_ROWS)
    nb = r // tr
    hb = tr // CONV_HALO
    last = r // CONV_HALO - 1
    dwp = jnp.zeros((2 * CONV_HALO, ch), F32).at[:taps].set(dw)
    cur = lambda col: pl.BlockSpec((tr, ch), lambda i: (i, col))
    prv = lambda col: pl.BlockSpec((CONV_HALO, ch), lambda i: (jnp.maximum(i * hb - 1, 0), col))
    nxt = lambda col: pl.BlockSpec((CONV_HALO, ch), lambda i: (jnp.minimum((i + 1) * hb, last), col))
    vec = pl.BlockSpec((1, ch), lambda i: (0, 0))
    return pl.pallas_call(
        functools.partial(_conv_kernel, taps=taps, n_blocks=nb),
        out_shape=jax.ShapeDtypeStruct((r, ch), BF16),
        grid=(nb,),
        in_specs=[cur(0), cur(1), prv(0), prv(1), nxt(0), nxt(1),
                  pl.BlockSpec((2 * CONV_HALO, ch), lambda i: (0, 0)), vec, vec],
        out_specs=pl.BlockSpec((tr, ch), lambda i: (i, 0)),
        scratch_shapes=[pltpu.VMEM((SUBLANES, tr + 2 * CONV_HALO, ch), F32), pltpu.VMEM((tr, ch), F32)],
        compiler_params=_params(1, 48),
        name="conformer_conv",
    )(agv, agv, agv, agv, agv, agv, dwp, ln_g.reshape(1, ch), ln_b.reshape(1, ch))


def _mla_down_kernel(h_ref, w_ref, qg_ref, kg_ref, cos_ref, sin_ref, q_ref, kv_ref, kr_ref, *, q_rank, kv_rank,
                     pair, use_rope):
    acc = jnp.dot(h_ref[...], w_ref[...], preferred_element_type=F32)

    def rms(v, g):
        return v * lax.rsqrt(jnp.mean(v * v, axis=-1, keepdims=True) + NORM_EPS) * g

    q_ref[...] = rms(acc[:, :q_rank], qg_ref[...]).astype(q_ref.dtype)
    kv_ref[...] = rms(acc[:, q_rank:q_rank + kv_rank], kg_ref[...]).astype(kv_ref.dtype)
    kr = acc[:, q_rank + kv_rank:]
    if use_rope:
        kr = _rope(kr, cos_ref[...], sin_ref[...], pair)
    kr_ref[...] = kr.astype(kr_ref.dtype)


def _mla_down(h, w_dn_p, q_g, kv_g, cos, sin, *, pair, use_rope):
    m, d = h.shape
    q_rank, kv_rank = q_g.shape[0], kv_g.shape[0]
    wn = w_dn_p.shape[1]
    assert wn == q_rank + kv_rank + LANES
    bm = _pick(m, 512, BF16_ROWS)
    row = lambda w: pl.BlockSpec((bm, w), lambda i: (i, 0))
    return pl.pallas_call(
        functools.partial(_mla_down_kernel, q_rank=q_rank, kv_rank=kv_rank, pair=pair, use_rope=use_rope),
        out_shape=(jax.ShapeDtypeStruct((m, q_rank), BF16), jax.ShapeDtypeStruct((m, kv_rank), BF16),
                   jax.ShapeDtypeStruct((m, LANES), BF16)),
        grid=(m // bm,),
        in_specs=[row(d), pl.BlockSpec((d, wn), lambda i: (0, 0)),
                  pl.BlockSpec((1, q_rank), lambda i: (0, 0)), pl.BlockSpec((1, kv_rank), lambda i: (0, 0)),
                  row(LANES), row(LANES)],
        out_specs=(row(q_rank), row(kv_rank), row(LANES)),
        compiler_params=_params(1, 56),
        name="mla_down",
    )(h, w_dn_p, q_g.reshape(1, q_rank), kv_g.reshape(1, kv_rank), cos, sin)


MLA_TQ = 1024
MLA_SUB = 256
MLA_KEY_CHUNK = 1280
MLA_STAGES = 12


def _mla_attn_kernel(qn_ref, qr_ref, kn_ref, kr_ref, v_ref, o_ref, kcat_ref, vcat_ref, qcat_ref, s_ref, *, tk, sub):
    nk = kcat_ref.shape[0]
    tq = qn_ref.shape[0]
    n_sub = tq // sub

    @pl.when(pl.program_id(1) == 0)
    def _():
        kcat_ref[:, :HEAD_DIM] = kn_ref[...]
        kcat_ref[:, HEAD_DIM:] = kr_ref[...]
        vcat_ref[:, :HEAD_DIM] = v_ref[...]
        vcat_ref[:, HEAD_DIM:] = jnp.ones((nk, HEAD_DIM), BF16)

    qcat_ref[:, :HEAD_DIM] = qn_ref[...]
    qcat_ref[:, HEAD_DIM:] = qr_ref[...]
    nt = (((1,), (1,)), ((), ()))
    n_chunks = nk // tk

    def scores(buf, u, off):
        s_ref[buf, u] = lax.dot_general(qcat_ref[u * sub:(u + 1) * sub, :], kcat_ref[pl.ds(off, tk), :], nt,
                                        preferred_element_type=F32)

    def absorb(buf, u, off, m, acc):
        s = s_ref[buf, u]
        m_new = jnp.maximum(m, jnp.max(s, axis=1, keepdims=True))
        p = jnp.exp2(s - m_new).astype(BF16)
        acc = jnp.exp2(m - m_new) * acc + jnp.dot(p, vcat_ref[pl.ds(off, tk), :], preferred_element_type=F32)
        return m_new, acc

    def stage(buf, off, carry):
        out = []
        for u in range(n_sub):
            out.append(absorb(buf, u, off, *carry[u]))
            scores(1 - buf, u, off + tk)
        return tuple(out)

    def trip(j, carry):
        off = pl.multiple_of(j * (MLA_STAGES * tk), tk)
        for k in range(MLA_STAGES):
            carry = stage(k % 2, off + k * tk, carry)
        return carry

    for u in range(n_sub):
        scores(0, u, 0)
    carry = tuple((jnp.full((sub, 1), NEG_INF, F32), jnp.zeros((sub, 2 * HEAD_DIM), F32)) for _ in range(n_sub))
    n_stages = n_chunks - 1
    carry = lax.fori_loop(0, n_stages // MLA_STAGES, trip, carry)
    for k in range(n_stages - n_stages % MLA_STAGES, n_stages):
        carry = stage(k % 2, k * tk, carry)
    for u in range(n_sub):
        _, acc = absorb(n_stages % 2, u, n_stages * tk, *carry[u])
        o_ref[u * sub:(u + 1) * sub, :] = (acc[:, :HEAD_DIM] / acc[:, HEAD_DIM:]).astype(o_ref.dtype)


def _mla_attn(qn, qr, kv, kr, *, n_heads, key_rows, key_block):
    r = qn.shape[0]
    tq = _pick(r, MLA_TQ, MLA_SUB) if r >= MLA_SUB else r
    sub = min(MLA_SUB, tq)
    tk = _pick(key_rows, MLA_KEY_CHUNK, LANES)
    once = pl.Buffered(1)
    kspec = lambda col0: pl.BlockSpec((key_rows, HEAD_DIM), lambda h, i: (key_block, col0 + h), pipeline_mode=once)
    qspec = pl.BlockSpec((tq, HEAD_DIM), lambda h, i: (i, h))
    return pl.pallas_call(
        functools.partial(_mla_attn_kernel, tk=tk, sub=sub),
        out_shape=jax.ShapeDtypeStruct((r, n_heads * HEAD_DIM), BF16),
        grid=(n_heads, r // tq),
        in_specs=[qspec, qspec, kspec(0),
                  pl.BlockSpec((key_rows, LANES), lambda h, i: (key_block, 0), pipeline_mode=once), kspec(n_heads)],
        out_specs=pl.BlockSpec((tq, HEAD_DIM), lambda h, i: (i, h)),
        scratch_shapes=[pltpu.VMEM((key_rows, 2 * HEAD_DIM), BF16), pltpu.VMEM((key_rows, 2 * HEAD_DIM), BF16),
                        pltpu.VMEM((tq, 2 * HEAD_DIM), BF16), pltpu.VMEM((2, tq // sub, sub, tk), F32)],
        compiler_params=_params(2, 60),
        name="mla_attn",
    )(qn, qr, kv, kr, kv)


def _router_kernel(x_ref, g_ref, sh_ref, sc_ref, wt_ref, aff_ref):
    h = _norm_mod(x_ref[...], g_ref[...], sh_ref[...], sc_ref[...])
    w = wt_ref[...]
    ne = w.shape[0]
    h_hi = h.astype(BF16)
    h_lo = (h - h_hi.astype(F32)).astype(BF16)
    w_hi = w.astype(BF16)
    w_lo = (w - w_hi.astype(F32)).astype(BF16)
    nt = (((1,), (1,)), ((), ()))
    both = lax.dot_general(jnp.concatenate([w_hi, w_lo], axis=0), h_hi, nt, preferred_element_type=F32)
    logits = both[:ne] + both[ne:] + lax.dot_general(w_hi, h_lo, nt, preferred_element_type=F32)
    e = jnp.exp(logits - jnp.max(logits, axis=0, keepdims=True))
    aff_ref[...] = e / jnp.sum(e, axis=0, keepdims=True)


def _router(xr, g, shift, scale, w_router_t):
    r, d = xr.shape
    ne = w_router_t.shape[0]
    tr = _pick(r, 512, LANES)
    vec = pl.BlockSpec((1, d), lambda i: (0, 0))
    return pl.pallas_call(
        _router_kernel,
        out_shape=jax.ShapeDtypeStruct((ne, r), F32),
        grid=(r // tr,),
        in_specs=[pl.BlockSpec((tr, d), lambda i: (i, 0)), vec, vec, vec, pl.BlockSpec((ne, d), lambda i: (0, 0))],
        out_specs=pl.BlockSpec((ne, tr), lambda i: (0, i)),
        compiler_params=_params(1, 48),
        name="router",
    )(xr, g.reshape(1, d), shift, scale, w_router_t)


def _lane_cumsum(v):
    lane = lax.broadcasted_iota(jnp.int32, v.shape, 1)
    k = 1
    while k < LANES:
        v = v + jnp.where(lane >= k, pltpu.roll(v, k, 1), 0)
        k *= 2
    return v


def _select_kernel(a_ref, idx_ref, thr_ref, *, cap):
    a = a_ref[...]
    ne, g, _ = a.shape
    n_slots = idx_ref.shape[2]
    bits = lax.bitcast_convert_type(a, jnp.int32)

    def count(mask):
        return jnp.sum(jnp.sum(mask.astype(F32), axis=2, keepdims=True), axis=1, keepdims=True)

    thr = jnp.zeros((ne, 1, 1), jnp.int32)
    for b in range(30, -1, -1):
        cand = thr | (1 << b)
        thr = jnp.where(count(bits >= cand) >= cap, cand, thr)
    thr_ref[...] = jnp.broadcast_to(thr, thr_ref.shape)
    before = lax.broadcasted_iota(jnp.int32, (g, g), 1) < lax.broadcasted_iota(jnp.int32, (g, g), 0)
    upto = lax.broadcasted_iota(jnp.int32, (g, g), 1) <= lax.broadcasted_iota(jnp.int32, (g, g), 0)
    row_id = lax.broadcasted_iota(jnp.int32, (g, n_slots), 0).astype(F32)
    slot = lax.broadcasted_iota(jnp.int32, (1, n_slots), 1).astype(F32)

    def over_rows(tri, per_row):
        return jnp.dot(tri.astype(F32), jnp.broadcast_to(per_row, (g, LANES)), precision=lax.Precision.HIGHEST,
                       preferred_element_type=F32)[:, 0:1]

    def per_expert(e, carry):
        bits_e = lax.bitcast_convert_type(a_ref[e], jnp.int32)
        thr_e = thr_ref[e][0:1, 0:1]
        above = bits_e > thr_e
        tie = bits_e == thr_e
        need = cap - jnp.sum(jnp.sum(above.astype(F32), axis=1, keepdims=True), axis=0, keepdims=True)
        tie_i = tie.astype(jnp.int32)
        incl = _lane_cumsum(tie_i)
        rank = (incl - tie_i).astype(F32) + over_rows(before, incl[:, LANES - 1:LANES].astype(F32))
        sel = above | (tie & (rank < need))
        cum = _lane_cumsum(sel.astype(jnp.int32)).astype(F32)
        row_cnt = cum[:, LANES - 1:LANES]
        row_end = over_rows(upto, row_cnt)
        g_of = jnp.sum((row_end <= slot).astype(F32), axis=0, keepdims=True)
        onehot = (row_id == g_of).astype(F32)
        start_of = jnp.sum(onehot * (row_end - row_cnt), axis=0, keepdims=True)
        cum_of = lax.dot_general(cum, onehot, (((0,), (0,)), ((), ())), preferred_element_type=F32)
        lane_of = jnp.sum((cum_of <= slot - start_of).astype(F32), axis=0, keepdims=True)
        idx_ref[e] = (g_of * LANES + lane_of).astype(jnp.int32)
        return carry

    lax.fori_loop(0, ne, per_expert, 0)


def _select(aff_t, cap):
    ne, r = aff_t.shape
    assert r % LANES == 0
    n_pad = max(r, SUBLANES * LANES)
    g = n_pad // LANES
    a3 = jnp.pad(aff_t, ((0, 0), (0, n_pad - r))).reshape(ne, g, LANES)
    n_slots = max(cap, LANES)
    idx = pl.pallas_call(
        functools.partial(_select_kernel, cap=cap),
        out_shape=jax.ShapeDtypeStruct((ne, 1, n_slots), jnp.int32),
        grid=(1,),
        in_specs=[pl.BlockSpec((ne, g, LANES), lambda i: (0, 0, 0))],
        out_specs=pl.BlockSpec((ne, 1, n_slots), lambda i: (0, 0, 0)),
        scratch_shapes=[pltpu.VMEM((ne, SUBLANES, LANES), jnp.int32)],
        compiler_params=_params(1, 32),
        name="ec_select",
    )(a3)[:, 0, :cap]
    return idx, jnp.take_along_axis(aff_t, idx, axis=1)


def _row_copy(src_hbm, buf, sem, slot, tok, r):
    return pltpu.make_async_copy(src_hbm.at[pl.ds(tok, 1), :], buf.at[slot, pl.ds(r, 1), :], sem.at[slot])


def _start_gather(idx_ref, src_hbm, buf, sem, step, slot, rows):
    for r in range(rows):
        _row_copy(src_hbm, buf, sem, slot, idx_ref[step * rows + r], r).start(priority=r % 2)


def _for_slot(slot, fn, n_slots=2):
    for s in range(n_slots):
        pl.when(slot == s)(functools.partial(fn, s))


def _wait_rows(src_hbm, buf, sem, slot, rows):
    for _ in range(rows):
        _row_copy(src_hbm, buf, sem, slot, 0, 0).wait()


def _moe_ffn_kernel(idx_ref, gate_ref, g_ref, sh_ref, sc_ref, wgu_ref, wd_ref, x_hbm, y_ref, xbuf, sem,
                    *, rows, n_steps):
    step = pl.program_id(0) * pl.num_programs(1) + pl.program_id(1)
    slot = step % 2

    @pl.when(step == 0)
    def _():
        _start_gather(idx_ref, x_hbm, xbuf, sem, 0, 0, rows)

    @pl.when(step + 1 < n_steps)
    def _():
        _for_slot(1 - slot, lambda s: _start_gather(idx_ref, x_hbm, xbuf, sem, step + 1, s, rows))

    _wait_rows(x_hbm, xbuf, sem, slot, rows)
    h = _norm_mod(xbuf[slot], g_ref[...], sh_ref[...], sc_ref[...]).astype(BF16)
    ff = wd_ref.shape[0]
    halves = []
    for hh in (h[:rows // 2], h[rows // 2:]):
        au = jnp.dot(hh, wgu_ref[...], preferred_element_type=F32)
        a, u = au[:, :ff], au[:, ff:]
        halves.append((a * _sigmoid(a) * u).astype(BF16))
    z = jnp.concatenate(halves, axis=0)
    y_ref[...] = jnp.dot(z, wd_ref[...], preferred_element_type=F32) * gate_ref[...]


def _moe_ffn(idx_flat, gate_col, xr, g, shift, scale, wgu, wd, layer, *, rows):
    r, d = xr.shape
    _, ne, ff, _ = wd.shape
    assert ff % LANES == 0
    slots = idx_flat.shape[0]
    nc = slots // ne // rows
    vec = pl.BlockSpec((1, d), lambda e, c, idx: (0, 0))
    return pl.pallas_call(
        functools.partial(_moe_ffn_kernel, rows=rows, n_steps=ne * nc),
        out_shape=jax.ShapeDtypeStruct((slots, d), F32),
        grid_spec=pltpu.PrefetchScalarGridSpec(
            num_scalar_prefetch=1, grid=(ne, nc),
            in_specs=[pl.BlockSpec((rows, 1), lambda e, c, idx: (e * nc + c, 0)), vec, vec, vec,
                      pl.BlockSpec((None, None, d, 2 * ff), lambda e, c, idx: (layer, e, 0, 0)),
                      pl.BlockSpec((None, None, ff, d), lambda e, c, idx: (layer, e, 0, 0)),
                      pl.BlockSpec(memory_space=pl.ANY)],
            out_specs=pl.BlockSpec((rows, d), lambda e, c, idx: (e * nc + c, 0)),
            scratch_shapes=[pltpu.VMEM((2, rows, d), F32), pltpu.SemaphoreType.DMA((2,))]),
        compiler_params=_params(2, 56),
        name="moe_ffn",
    )(idx_flat, gate_col, g.reshape(1, d), shift, scale, wgu, wd, xr)


MOE_ROWS = 256
COMBINE_BUFS = 3


def _moe_combine_kernel(idx_ref, y_ref, gate_ref, x_hbm, o_hbm, abuf, gsem, ssem, *, rows, n_steps, nc):
    del x_hbm
    c = pl.program_id(1)
    step = pl.program_id(0) * nc + c
    slot = step % COMBINE_BUFS
    nxt = (step + 1) % COMBINE_BUFS
    prv = (step + 2) % COMBINE_BUFS

    def scatter_copy(s, sl, r):
        return pltpu.make_async_copy(abuf.at[sl, pl.ds(r, 1), :], o_hbm.at[pl.ds(idx_ref[s * rows + r], 1), :],
                                     ssem.at[sl])

    def wait_scatter(sl):
        for _ in range(rows):
            pltpu.make_async_copy(abuf.at[sl, pl.ds(0, 1), :], o_hbm.at[pl.ds(0, 1), :], ssem.at[sl]).wait()

    def gather_next():
        _for_slot(nxt, lambda s: _start_gather(idx_ref, o_hbm, abuf, gsem, step + 1, s, rows), COMBINE_BUFS)

    @pl.when(step == 0)
    def _():
        _start_gather(idx_ref, o_hbm, abuf, gsem, 0, 0, rows)

    @pl.when(c < nc - 1)
    def _():
        @pl.when(c >= 2)
        def _():
            wait_scatter(nxt)
        gather_next()

    _wait_rows(o_hbm, abuf, gsem, slot, rows)
    abuf[slot] = abuf[slot] + gate_ref[...] * y_ref[...]

    def start_scatter(s):
        for r in range(rows):
            scatter_copy(step, s, r).start(priority=r % 2)
    _for_slot(slot, start_scatter, COMBINE_BUFS)

    @pl.when(c == nc - 1)
    def _():
        if nc >= 3:
            wait_scatter(nxt)
        if nc >= 2:
            wait_scatter(prv)
        wait_scatter(slot)

        @pl.when(step + 1 < n_steps)
        def _():
            gather_next()


def _moe_combine(idx_flat, y, gate_vec, xr, *, n_experts, rows):
    r, d = xr.shape
    slots = idx_flat.shape[0]
    nc = slots // n_experts // rows
    return pl.pallas_call(
        functools.partial(_moe_combine_kernel, rows=rows, n_steps=n_experts * nc, nc=nc),
        out_shape=jax.ShapeDtypeStruct((r, d), F32),
        grid_spec=pltpu.PrefetchScalarGridSpec(
            num_scalar_prefetch=1, grid=(n_experts, nc),
            in_specs=[pl.BlockSpec((rows, d), lambda e, c, idx: (e * nc + c, 0)),
                      pl.BlockSpec((1, d), lambda e, c, idx: (0, 0)),
                      pl.BlockSpec(memory_space=pl.ANY)],
            out_specs=pl.BlockSpec(memory_space=pl.ANY),
            scratch_shapes=[pltpu.VMEM((COMBINE_BUFS, rows, d), F32), pltpu.SemaphoreType.DMA((COMBINE_BUFS,)),
                            pltpu.SemaphoreType.DMA((COMBINE_BUFS,))]),
        input_output_aliases={3: 0},
        compiler_params=_params(2, 48),
        name="moe_combine",
    )(idx_flat, y, gate_vec, xr)


def _ec_moe(xr, g, shift, scale, out_gate, w_router_t, wgu, wd, layer):
    r = xr.shape[0]
    ne = w_router_t.shape[0]
    cap = EC_CAPACITY * r // ne
    rows = _pick(cap, MOE_ROWS, 2 * BF16_ROWS)
    aff_t = _router(xr, g, shift, scale, w_router_t)
    idx, gate = _select(aff_t, cap)
    idx_flat = idx.reshape(-1)
    y = _moe_ffn(idx_flat, gate.reshape(-1, 1), xr, g, shift, scale, wgu, wd, layer, rows=rows)
    return _moe_combine(idx_flat, y, out_gate, xr, n_experts=ne, rows=rows)


def _rope_tables(n, dim):
    half = dim // 2
    inv = ROPE_BASE ** (-jnp.arange(0, half, 2, dtype=F32) / half)
    rows = jnp.repeat(jnp.arange(n // GRID_W, dtype=F32), GRID_W)
    cols = (jnp.arange(n) % GRID_W).astype(F32)
    ar = rows[:, None] * inv[None, :]
    ac = cols[:, None] * inv[None, :]
    cos = jnp.concatenate([jnp.cos(ar), jnp.cos(ar), jnp.cos(ac), jnp.cos(ac)], axis=1)
    sin = jnp.concatenate([-jnp.sin(ar), jnp.sin(ar), -jnp.sin(ac), jnp.sin(ac)], axis=1)
    if dim < LANES:
        cos = jnp.concatenate([cos, jnp.ones((n, LANES - dim), F32)], axis=1)
        sin = jnp.concatenate([sin, jnp.zeros((n, LANES - dim), F32)], axis=1)
    return cos, sin


def _even_mixer(hx, hy, x, y, gate_x, gate_y, w_in, sink, dw, ln_g, ln_b, w_out, tables, need_ctx):
    ch = dw.shape[1]
    aq = w_out.shape[0] - ch
    akv = (w_in.shape[1] - aq - 2 * ch) // 2
    cos, sin = tables
    scale = HEAD_DIM ** -0.5 * LOG2_E
    pair = HEAD_DIM // 4
    wq = w_in[:, :aq].astype(BF16)
    wk = w_in[:, aq:aq + akv].astype(BF16)
    w_agv = jnp.concatenate([w_in[:, aq + 2 * akv:], w_in[:, aq + akv:aq + 2 * akv]], axis=1).astype(BF16)
    w_o1 = w_out[:aq].astype(BF16)
    w_o2 = w_out[aq:].astype(BF16)
    v_col0 = 2 * ch // HEAD_DIM

    qx = _matmul([hx], [wq], out_dtype=BF16, epilogue="rope", extras=(cos, sin), scale=scale, pair=pair)
    kx = _matmul([hx], [wk], out_dtype=BF16, epilogue="rope", extras=(cos, sin), pair=pair)
    agv_x = _matmul([hx], [w_agv], out_dtype=BF16)
    ky = _matmul([hy], [wk], out_dtype=BF16)
    agv_y = _matmul([hy], [w_agv], out_dtype=BF16)

    att_x = _gqa(qx, kx, agv_x, v_col0, ky, agv_y, sink, local=True)
    conv_x = _conformer_conv(agv_x, dw, ln_g, ln_b)
    x = _matmul([att_x, conv_x], [w_o1, w_o2], out_dtype=F32, epilogue="resid", extras=(x, gate_x), bn=1024)
    if need_ctx:
        qy = _matmul([hy], [wq], out_dtype=BF16, scale=scale)
        att_y = _gqa(qy, None, None, v_col0, ky, agv_y, sink, local=False)
        conv_y = _conformer_conv(agv_y, dw, ln_g, ln_b)
        y = _matmul([att_y, conv_y], [w_o1, w_o2], out_dtype=F32, epilogue="resid", extras=(y, gate_y), bn=512)
    return x, y


def _mla_mixer(hx, hy, x, y, gate_x, gate_y, w_dn, q_g, kv_g, w_uq, w_ukv, w_o, tables, need_ctx):
    n, d = hx.shape
    lc = hy.shape[0]
    q_rank, kv_rank = q_g.shape[0], kv_g.shape[0]
    n_heads = w_o.shape[0] // HEAD_DIM
    rope_dim = w_dn.shape[1] - q_rank - kv_rank
    assert n % lc == 0 and rope_dim <= LANES
    cos, sin = tables
    pair = rope_dim // 4
    scale = (HEAD_DIM + rope_dim) ** -0.5 * LOG2_E
    w_dn_p = jnp.concatenate([w_dn, jnp.zeros((d, LANES - rope_dim), F32)], axis=1).astype(BF16)
    w_uq3 = w_uq.reshape(q_rank, n_heads, HEAD_DIM + rope_dim)
    w_qn = w_uq3[:, :, :HEAD_DIM].reshape(q_rank, n_heads * HEAD_DIM).astype(BF16)
    w_qr = jnp.concatenate([w_uq3[:, :, HEAD_DIM:], jnp.zeros((q_rank, n_heads, LANES - rope_dim), F32)],
                           axis=2).reshape(q_rank, n_heads * LANES).astype(BF16)
    w_kv3 = w_ukv.reshape(kv_rank, n_heads, 2 * HEAD_DIM)
    w_kv = jnp.concatenate([w_kv3[:, :, :HEAD_DIM].reshape(kv_rank, -1), w_kv3[:, :, HEAD_DIM:].reshape(kv_rank, -1)],
                           axis=1).astype(BF16)
    w_ob = w_o.astype(BF16)

    qlat_x, ckv_x, kr_x = _mla_down(hx, w_dn_p, q_g, kv_g, cos, sin, pair=pair, use_rope=True)
    qlat_y, ckv_y, kr_y = _mla_down(hy, w_dn_p, q_g, kv_g, cos[:lc], sin[:lc], pair=pair, use_rope=False)
    kv_all = _matmul([jnp.concatenate([ckv_x, ckv_y], axis=0)], [w_kv], out_dtype=BF16, bm=1280)
    kr_all = jnp.concatenate([kr_x, kr_y], axis=0)

    qn_x = _matmul([qlat_x], [w_qn], out_dtype=BF16, scale=scale)
    qr_x = _matmul([qlat_x], [w_qr], out_dtype=BF16, epilogue="rope", extras=(cos, sin), scale=scale, pair=pair)
    o_x = _mla_attn(qn_x, qr_x, kv_all, kr_all, n_heads=n_heads, key_rows=n + lc, key_block=0)
    x = _matmul([o_x], [w_ob], out_dtype=F32, epilogue="resid", extras=(x, gate_x), bn=1024)
    if need_ctx:
        qn_y = _matmul([qlat_y], [w_qn], out_dtype=BF16, scale=scale)
        qr_y = _matmul([qlat_y], [w_qr], out_dtype=BF16, scale=scale)
        o_y = _mla_attn(qn_y, qr_y, kv_all, kr_all, n_heads=n_heads, key_rows=lc, key_block=n // lc)
        y = _matmul([o_y], [w_ob], out_dtype=F32, epilogue="resid", extras=(y, gate_y), bn=512)
    return x, y


def kernel(x, c, ctx, c_ctx, ada_w, ada_b, norm1_g, norm2_g, ev_w_in, ev_sink, ev_dw, ev_ln_g, ev_ln_b, ev_w_out, od_w_dn, od_q_norm_g, od_kv_norm_g, od_w_uq, od_w_ukv, od_w_o, moe_router, moe_w_gate, moe_w_up, moe_w_down, final_g):
    b, n, d = x.shape
    assert b == 1 and n % GRID_W == 0
    depth = ada_w.shape[0]
    xr, yr = x[0], ctx[0]
    mods = _adaln(c, c_ctx, ada_w, ada_b)
    tables_a = _rope_tables(n, HEAD_DIM)
    rope_dim = od_w_dn.shape[2] - od_q_norm_g.shape[1] - od_kv_norm_g.shape[1]
    tables_c = _rope_tables(n, rope_dim)
    wgu = jnp.concatenate([moe_w_gate, moe_w_up], axis=3).astype(BF16)
    wd = moe_w_down.astype(BF16)
    for l in range(depth):
        need_ctx = l < depth - 1
        mx = [mods[l, 0:1, i * d:(i + 1) * d] for i in range(6)]
        my = [mods[l, 1:2, i * d:(i + 1) * d] for i in range(6)]
        hx = _normmod(xr, norm1_g[l], mx[0], mx[1], BF16)
        hy = _normmod(yr, norm1_g[l], my[0], my[1], BF16)
        i = l // 2
        if l % 2 == 0:
            xr, yr = _even_mixer(hx, hy, xr, yr, mx[2], my[2], ev_w_in[i], ev_sink[i], ev_dw[i], ev_ln_g[i],
                                 ev_ln_b[i], ev_w_out[i], tables_a, need_ctx)
        else:
            xr, yr = _mla_mixer(hx, hy, xr, yr, mx[2], my[2], od_w_dn[i], od_q_norm_g[i], od_kv_norm_g[i],
                                od_w_uq[i], od_w_ukv[i], od_w_o[i], tables_c, need_ctx)
        w_rt = moe_router[l].T
        xr = _ec_moe(xr, norm2_g[l], mx[3], mx[4], mx[5], w_rt, wgu, wd, l)
        if need_ctx:
            yr = _ec_moe(yr, norm2_g[l], my[3], my[4], my[5], w_rt, wgu, wd, l)
    zero = jnp.zeros((1, d), F32)
    return _normmod(xr, final_g, zero, zero, F32)[None]
```
